```python
import math
import jax, jax.numpy as jnp
from jax import lax
import numpy as np

D_MODEL = 2048
BATCH = 4
SEQ = 2048
DEPTH = 4
DEC_BATCH = 128
DEC_SEQ = 4
PAST_LEN = 16384
PAGE_SIZE = 128

MIX_WIDTH = D_MODEL
GROUP_WIDTH = MIX_WIDTH // 4
HEAD_DIM = 128
N_HEADS_GRP = GROUP_WIDTH // HEAD_DIM
CHUNK_A = 128
POOL_WINDOWS = (2, 4, 8, 16)
POOL_STATE = max(POOL_WINDOWS) - 1
MLSTM_CHUNK = 64
CONV_W = 3
D_FF = ((8 * D_MODEL // 3) + 127) // 128 * 128
N_MEM = 256
MEM_HEADS = 4
MEM_HEAD_DIM = 128
MEM_WIDTH = MEM_HEADS * MEM_HEAD_DIM
N_IN = 10 * GROUP_WIDTH + 2 * N_HEADS_GRP
EPS = 1e-6

kernel_name = 'hybrid_headgroup_decoder_step'


def rms_norm(x, g):
    xf = x.astype(jnp.float32)
    y = xf * lax.rsqrt(jnp.mean(xf * xf, axis=-1, keepdims=True) + EPS)
    return (y * g.astype(jnp.float32)).astype(x.dtype)


def chunk_spatial_gate(u, v, w_s, b_s):
    n, t, h, d = v.shape
    t_pad = -(-t // CHUNK_A) * CHUNK_A
    vp = jnp.pad(v, ((0, 0), (0, t_pad - t), (0, 0), (0, 0)))
    vc = vp.reshape(n, t_pad // CHUNK_A, CHUNK_A, h, d)
    mask = jnp.tril(jnp.ones((CHUNK_A, CHUNK_A), dtype=bool))
    w = jnp.where(mask, w_s, 0).astype(v.dtype)
    mixed = jnp.einsum('hts,ncshd->ncthd', w, vc) + b_s.T[None, None, :, :, None].astype(v.dtype)
    mixed = mixed.reshape(n, t_pad, h, d)[:, :t]
    return u * mixed


def multiscale_pool(x, prefix, n_valid):
    n, t, c = x.shape
    p = prefix.shape[1]
    g_w = c // len(POOL_WINDOWS)
    xx = jnp.concatenate([prefix.astype(x.dtype), x], axis=1)
    xf = xx.astype(jnp.float32).reshape(n, p + t, len(POOL_WINDOWS), g_w)
    cs = jnp.concatenate([jnp.zeros_like(xf[:, :1]), jnp.cumsum(xf, axis=1)], axis=1)
    run_end = cs[:, p + 1:p + 1 + t]
    avail = jnp.arange(t, dtype=jnp.float32)[None, :, None] + (1.0 + n_valid)
    outs = []
    for g, w in enumerate(POOL_WINDOWS):
        run_start = cs[:, p + 1 - w:p + 1 - w + t, g]
        outs.append((run_end[:, :, g] - run_start) / jnp.minimum(avail, float(w)))
    mean = jnp.stack(outs, axis=2).reshape(n, t, c)
    return (mean - x.astype(jnp.float32)).astype(x.dtype), xx[:, -POOL_STATE:]


def causal_dwconv(x, prefix, w):
    k = w.shape[0]
    t = x.shape[1]
    xx = jnp.concatenate([prefix.astype(x.dtype), x], axis=1)
    y = xx[:, 0:t] * w[0]
    for j in range(1, k):
        y = y + xx[:, j:j + t] * w[j]
    return y, xx[:, -(k - 1):]


def mlstm_chunkwise(q, k, v, i_pre, log_f, c0, n0, m0):
    nb, t, h, d = q.shape
    L = math.gcd(t, MLSTM_CHUNK)
    nc = t // L

    def to_chunks(a):
        return jnp.moveaxis(a.reshape((nb, nc, L) + a.shape[2:]), 1, 0)

    xs = (to_chunks(q), to_chunks(k), to_chunks(v), to_chunks(i_pre), to_chunks(log_f))
    causal = jnp.tril(jnp.ones((L, L), dtype=bool))

    def step(carry, inp):
        c, nv, m = carry
        qc, kc, vc, ic, fc = inp
        b = jnp.cumsum(fc, axis=1)
        log_d = b[:, :, None, :] - b[:, None, :, :] + ic[:, None, :, :]
        log_d = jnp.where(causal[None, :, :, None], log_d, -jnp.inf)
        log_inter = b + m[:, None, :]
        m_t = jnp.maximum(log_inter, jnp.max(log_d, axis=2))
        dmat = jnp.exp(log_d - m_t[:, :, None, :])
        inter = jnp.exp(log_inter - m_t)
        s = jnp.einsum('nthd,nshd->ntsh', qc, kc) * dmat
        num = jnp.einsum('ntsh,nshd->nthd', s, vc) + inter[..., None] * jnp.einsum('nthd,nhed->nthe', qc, c)
        den = jnp.sum(s, axis=2) + inter * jnp.einsum('nthd,nhd->nth', qc, nv)
        hout = num / jnp.maximum(jnp.abs(den), jnp.exp(-m_t))[..., None]
        m_new = m_t[:, -1]
        decay = jnp.exp(b[:, -1] + m - m_new)
        wts = jnp.exp(b[:, -1:, :] - b + ic - m_new[:, None, :])
        c_new = decay[..., None, None] * c + jnp.einsum('nsh,nshe,nshd->nhed', wts, vc, kc)
        n_new = decay[..., None] * nv + jnp.einsum('nsh,nshd->nhd', wts, kc)
        return (c_new, n_new, m_new), hout

    (c, nv, m), hs = lax.scan(step, (c0, n0, m0), xs)
    hs = jnp.moveaxis(hs, 0, 1).reshape(nb, t, h, d)
    return hs, c, nv, m


def mix_block(hn, w_in, w_s, b_s, w_pool, pool_scale, b_gate, mlstm_norm, w_dconv, w_out,
              pool_prefix, pool_valid, conv_prefix, c0, n0, m0):
    f32 = jnp.float32
    nb, t, _ = hn.shape
    gw = GROUP_WIDTH
    z = hn @ w_in
    a_u, a_v, b_x, c_q, c_k, c_v, c_o, d_b, d_c, d_x = [z[..., j * gw:(j + 1) * gw] for j in range(10)]
    gates = (z[..., 10 * gw:] + b_gate).astype(f32)
    i_pre, f_pre = gates[..., :N_HEADS_GRP], gates[..., N_HEADS_GRP:]

    def heads(a):
        return a.reshape(nb, t, N_HEADS_GRP, HEAD_DIM)

    v_rows = jax.nn.gelu(a_v)
    y_a = chunk_spatial_gate(heads(jax.nn.gelu(a_u)), heads(v_rows), w_s, b_s).reshape(nb, t, gw)
    pooled, pool_state = multiscale_pool(b_x, pool_prefix, pool_valid)
    y_b = jnp.einsum('ntgc,gce->ntge', heads(pooled), w_pool).reshape(nb, t, gw) * pool_scale
    q = heads(c_q).astype(f32)
    k = heads(c_k).astype(f32) * HEAD_DIM ** -0.5
    v = heads(c_v).astype(f32)
    h_c, c1, n1, m1 = mlstm_chunkwise(q, k, v, i_pre, jax.nn.log_sigmoid(f_pre), c0, n0, m0)
    h_c = h_c * lax.rsqrt(jnp.mean(h_c * h_c, axis=-1, keepdims=True) + EPS)
    y_c = (jax.nn.sigmoid(c_o.astype(f32)) * h_c.reshape(nb, t, gw) * mlstm_norm.astype(f32)).astype(hn.dtype)
    conv_out, conv_state = causal_dwconv(d_c * d_x, conv_prefix, w_dconv)
    y_d = d_b * conv_out
    out = jnp.concatenate([y_a, y_b, y_c, y_d], axis=-1) @ w_out
    return out, v_rows, pool_state, conv_state, c1, n1, m1


def mem_kv(mem, g, w_kv):
    nb, m, _ = mem.shape
    kv = rms_norm(mem, g) @ w_kv
    k, v = jnp.split(kv, 2, axis=-1)
    return k.reshape(nb, m, MEM_HEADS, MEM_HEAD_DIM), v.reshape(nb, m, MEM_HEADS, MEM_HEAD_DIM)


def cross_attend(hn, k, v, w_q, w_o):
    nb, t, _ = hn.shape
    q = (hn @ w_q).reshape(nb, t, MEM_HEADS, MEM_HEAD_DIM)
    s = jnp.einsum('nthd,nmhd->nhtm', q, k.astype(q.dtype), preferred_element_type=jnp.float32) * MEM_HEAD_DIM ** -0.5
    p = jax.nn.softmax(s, axis=-1).astype(hn.dtype)
    o = jnp.einsum('nhtm,nmhd->nthd', p, v.astype(hn.dtype))
    return o.reshape(nb, t, MEM_WIDTH) @ w_o


def conv_ffn(hn, w_up, w_fconv, w_down, prefix):
    up = hn @ w_up
    up_c, state = causal_dwconv(up, prefix, w_fconv)
    g, u = jnp.split(up_c, 2, axis=-1)
    return (jax.nn.silu(g) * u) @ w_down, state


def decoder_layer(x, mk, mv, pool_prefix, pool_valid, conv_prefix, c0, n0, m0, ffn_prefix, lw):
    (norm_mix, w_in, w_s, b_s, w_pool, pool_scale, b_gate, mlstm_norm, w_dconv, w_out,
     norm_cross, w_q, w_o, norm_ffn, w_up, w_fconv, w_down) = lw
    mix, v_rows, pool_st, conv_st, c1, n1, m1 = mix_block(
        rms_norm(x, norm_mix), w_in, w_s, b_s, w_pool, pool_scale, b_gate, mlstm_norm, w_dconv, w_out,
        pool_prefix, pool_valid, conv_prefix, c0, n0, m0)
    x = x + mix
    x = x + cross_attend(rms_norm(x, norm_cross), mk, mv, w_q, w_o)
    ffn, ffn_st = conv_ffn(rms_norm(x, norm_ffn), w_up, w_fconv, w_down, ffn_prefix)
    x = x + ffn
    return x, v_rows, pool_st, conv_st, c1, n1, m1, ffn_st


def setup_inputs(seed: int = 0) -> dict:
    key = jax.random.key(seed)
    ks = iter(jax.random.split(key, 40))

    def nrm(shape, scale):
        return scale * jax.random.normal(next(ks), shape, jnp.float32)

    H, Dh, gw = N_HEADS_GRP, HEAD_DIM, GROUP_WIDTH
    return {
        'x_prompt': nrm((BATCH, SEQ, D_MODEL), 1.0),
        'x_sample': nrm((DEC_BATCH, DEC_SEQ, D_MODEL), 1.0),
        'mem_prompt': nrm((BATCH, N_MEM, D_MODEL), 1.0),
        'state_pool': nrm((DEPTH, DEC_BATCH, POOL_STATE, gw), 1.0),
        'state_conv': nrm((DEPTH, DEC_BATCH, CONV_W - 1, gw), 1.0),
        'state_mlstm_c': nrm((DEPTH, DEC_BATCH, H, Dh, Dh), 0.3),
        'state_mlstm_n': nrm((DEPTH, DEC_BATCH, H, Dh), 0.3),
        'state_mlstm_m': nrm((DEPTH, DEC_BATCH, H), 0.5),
        'state_ffn_conv': nrm((DEPTH, DEC_BATCH, CONV_W - 1, 2 * D_FF), 1.0),
        'cache_mem_k': nrm((DEPTH, DEC_BATCH, N_MEM, MEM_HEADS, MEM_HEAD_DIM), 1.0),
        'cache_mem_v': nrm((DEPTH, DEC_BATCH, N_MEM, MEM_HEADS, MEM_HEAD_DIM), 1.0),
        'norm_mix': 1.0 + nrm((DEPTH, D_MODEL), 0.01),
        'w_in': nrm((DEPTH, D_MODEL, N_IN), D_MODEL ** -0.5),
        'w_s': nrm((DEPTH, H, CHUNK_A, CHUNK_A), 0.5 * CHUNK_A ** -0.5),
        'b_s': 1.0 + nrm((DEPTH, H, CHUNK_A), 0.01),
        'w_pool': nrm((DEPTH, H, Dh, Dh), Dh ** -0.5),
        'pool_scale': 0.5 + nrm((DEPTH, gw), 0.01),
        'b_gate': jnp.concatenate([nrm((DEPTH, H), 0.1), 3.0 + nrm((DEPTH, H), 0.1)], axis=-1),
        'mlstm_norm': 1.0 + nrm((DEPTH, gw), 0.01),
        'w_dconv': nrm((DEPTH, CONV_W, gw), CONV_W ** -0.5),
        'w_out': nrm((DEPTH, MIX_WIDTH, D_MODEL), MIX_WIDTH ** -0.5),
        'mem_norm': 1.0 + nrm((DEPTH, D_MODEL), 0.01),
        'w_kv': nrm((DEPTH, D_MODEL, 2 * MEM_WIDTH), D_MODEL ** -0.5),
        'norm_cross': 1.0 + nrm((DEPTH, D_MODEL), 0.01),
        'w_q': nrm((DEPTH, D_MODEL, MEM_WIDTH), D_MODEL ** -0.5),
        'w_o': nrm((DEPTH, MEM_WIDTH, D_MODEL), MEM_WIDTH ** -0.5),
        'norm_ffn': 1.0 + nrm((DEPTH, D_MODEL), 0.01),
        'w_up': nrm((DEPTH, D_MODEL, 2 * D_FF), D_MODEL ** -0.5),
        'w_fconv': nrm((DEPTH, CONV_W, 2 * D_FF), CONV_W ** -0.5),
        'w_down': nrm((DEPTH, D_FF, D_MODEL), D_FF ** -0.5),
        'final_norm': 1.0 + nrm((D_MODEL,), 0.01),
    }


def reference(x_prompt, x_sample, mem_prompt, state_pool, state_conv, state_mlstm_c, state_mlstm_n,
              state_mlstm_m, state_ffn_conv, cache_mem_k, cache_mem_v, norm_mix, w_in, w_s, b_s, w_pool,
              pool_scale, b_gate, mlstm_norm, w_dconv, w_out, mem_norm, w_kv, norm_cross, w_q, w_o,
              norm_ffn, w_up, w_fconv, w_down, final_norm):
    f32 = jnp.float32
    nb = x_prompt.shape[0]
    pdt = x_prompt.dtype
    xp, xs = x_prompt, x_sample
    p_pool, p_conv, p_c, p_n, p_m, p_ffn, p_mk, p_mv = [], [], [], [], [], [], [], []
    s_v, s_pool, s_conv, s_c, s_n, s_m, s_ffn = [], [], [], [], [], [], []
    for l in range(DEPTH):
        lw = (norm_mix[l], w_in[l], w_s[l], b_s[l], w_pool[l], pool_scale[l], b_gate[l], mlstm_norm[l],
              w_dconv[l], w_out[l], norm_cross[l], w_q[l], w_o[l], norm_ffn[l], w_up[l], w_fconv[l], w_down[l])
        mk, mv = mem_kv(mem_prompt, mem_norm[l], w_kv[l])
        xp, _, pool_st, conv_st, c1, n1, m1, ffn_st = decoder_layer(
            xp, mk, mv,
            jnp.zeros((nb, POOL_STATE, GROUP_WIDTH), pdt), 0,
            jnp.zeros((nb, CONV_W - 1, GROUP_WIDTH), pdt),
            jnp.zeros((nb, N_HEADS_GRP, HEAD_DIM, HEAD_DIM), f32),
            jnp.zeros((nb, N_HEADS_GRP, HEAD_DIM), f32),
            jnp.zeros((nb, N_HEADS_GRP), f32),
            jnp.zeros((nb, CONV_W - 1, 2 * D_FF), pdt), lw)
        p_pool.append(pool_st); p_conv.append(conv_st); p_ffn.append(ffn_st)
        p_c.append(c1.astype(pdt)); p_n.append(n1.astype(pdt)); p_m.append(m1.astype(pdt))
        p_mk.append(mk); p_mv.append(mv)
        xs, v_rows, pool_st, conv_st, c1, n1, m1, ffn_st = decoder_layer(
            xs, cache_mem_k[l], cache_mem_v[l],
            state_pool[l], POOL_STATE, state_conv[l],
            state_mlstm_c[l].astype(f32), state_mlstm_n[l].astype(f32), state_mlstm_m[l].astype(f32),
            state_ffn_conv[l], lw)
        s_v.append(v_rows); s_pool.append(pool_st); s_conv.append(conv_st); s_ffn.append(ffn_st)
        s_c.append(c1.astype(state_mlstm_c.dtype)); s_n.append(n1.astype(state_mlstm_n.dtype))
        s_m.append(m1.astype(state_mlstm_m.dtype))
    y_prompt = rms_norm(xp, final_norm)
    y_sample = rms_norm(xs, final_norm)
    return (y_prompt, y_sample,
            jnp.stack(p_pool), jnp.stack(p_conv), jnp.stack(p_c), jnp.stack(p_n), jnp.stack(p_m),
            jnp.stack(p_ffn), jnp.stack(p_mk), jnp.stack(p_mv),
            jnp.stack(s_v), jnp.stack(s_pool), jnp.stack(s_conv), jnp.stack(s_c), jnp.stack(s_n),
            jnp.stack(s_m), jnp.stack(s_ffn))
```

```python
import functools

import jax
import jax.numpy as jnp
from jax import lax
from jax.experimental import pallas as pl
from jax.experimental.pallas import tpu as pltpu

F32 = jnp.float32
BF16 = jnp.bfloat16
EPS = 1e-6

LANES = 128
SUBLANES = 8
GROUP_WIDTH = 512
HEAD_DIM = 128
N_HEADS = GROUP_WIDTH // HEAD_DIM
N_SLICES = 10
POOL_WINDOWS = (2, 4, 8, 16)
POOL_STATE = max(POOL_WINDOWS) - 1
N_MEM = 256
FF_TILE = 256
DOWN_TILE = 1024
VMEM_LIMIT = 60 * 1024 * 1024
HIGHEST = lax.Precision.HIGHEST

_NT = (((1,), (1,)), ((), ()))


def _params(n_axes, vmem=VMEM_LIMIT):
    return pltpu.CompilerParams(dimension_semantics=("arbitrary",) * n_axes, vmem_limit_bytes=vmem)


def _rms(x, g):
    return x * lax.rsqrt(jnp.mean(x * x, axis=-1, keepdims=True) + EPS) * g


def _gelu(x):
    return 0.5 * x * (1.0 + jnp.tanh(0.7978845608028654 * (x + 0.044715 * (x * x * x))))


def _sigmoid(x):
    return 1.0 / (1.0 + jnp.exp(-x))


def _log_sigmoid(x):
    return jnp.minimum(x, 0.0) - jnp.log(1.0 + jnp.exp(-jnp.abs(x)))


def _lane_col(x, lane):
    idx = lax.broadcasted_iota(jnp.int32, x.shape, 1)
    return jnp.sum(jnp.where(idx == lane, x, 0.0), axis=-1, keepdims=True)


def _const_spec(shape):
    nd = len(shape)
    return pl.BlockSpec(shape, lambda *_: (0,) * nd, pipeline_mode=pl.Buffered(1))


def _proj_in_kernel(x_ref, g_ref, w_ref, wg_ref, z_ref, gate_ref, hn_sc):
    @pl.when(pl.program_id(1) == 0)
    def _():
        hn_sc[...] = _rms(x_ref[...], g_ref[...]).astype(BF16)
        gate_ref[...] = jnp.dot(hn_sc[...], wg_ref[...], preferred_element_type=F32)

    z_ref[...] = jnp.dot(hn_sc[...], w_ref[...], preferred_element_type=F32)


def _proj_in(x, g, w, wg, tm, tn=1024):
    m, d = x.shape
    n = w.shape[1]
    return pl.pallas_call(
        _proj_in_kernel,
        grid=(m // tm, n // tn),
        in_specs=[
            pl.BlockSpec((tm, d), lambda i, j: (i, 0)),
            pl.BlockSpec((1, d), lambda i, j: (0, 0)),
            pl.BlockSpec((d, tn), lambda i, j: (0, j)),
            pl.BlockSpec((d, LANES), lambda i, j: (0, 0)),
        ],
        out_specs=[
            pl.BlockSpec((tm, tn), lambda i, j: (i, j)),
            pl.BlockSpec((tm, LANES), lambda i, j: (i, 0)),
        ],
        out_shape=[jax.ShapeDtypeStruct((m, n), F32), jax.ShapeDtypeStruct((m, LANES), F32)],
        scratch_shapes=[pltpu.VMEM((tm, d), BF16)],
        compiler_params=_params(2),
        name="proj_in",
    )(x, g, w, wg)


def _norm_mm_kernel(x_ref, g_ref, w_ref, o_ref, hn_sc):
    @pl.when(pl.program_id(1) == 0)
    def _():
        hn_sc[...] = _rms(x_ref[...], g_ref[...]).astype(BF16)

    o_ref[...] = jnp.dot(hn_sc[...], w_ref[...], preferred_element_type=F32)


def _norm_mm(x, g, w, tm, tn):
    m, d = x.shape
    n = w.shape[1]
    return pl.pallas_call(
        _norm_mm_kernel,
        grid=(m // tm, n // tn),
        in_specs=[
            pl.BlockSpec((tm, d), lambda i, j: (i, 0)),
            pl.BlockSpec((1, d), lambda i, j: (0, 0)),
            pl.BlockSpec((d, tn), lambda i, j: (0, j)),
        ],
        out_specs=pl.BlockSpec((tm, tn), lambda i, j: (i, j)),
        out_shape=jax.ShapeDtypeStruct((m, n), F32),
        scratch_shapes=[pltpu.VMEM((tm, d), BF16)],
        compiler_params=_params(2),
        name="norm_mm",
    )(x, g, w)


def _mix_prompt_kernel(au_ref, av_ref, bx_ref, cq_ref, ck_ref, cv_ref, co_ref, db_ref, dc_ref, dx_ref,
                       gc_ref, gr_ref, ws_ref, bs_ref, wp_ref, ps_ref, mn_ref, bgr_ref, bgc_ref, wd_ref,
                       y_ref, cs_ref, c_ref, n_ref, m_ref,
                       c_sc, n_sc, m_sc, xx_sc, xd_sc):
    t = pl.program_id(1)
    rows = y_ref.shape[0]
    gw = GROUP_WIDTH
    pool_pad = 2 * SUBLANES
    conv_pad = SUBLANES

    @pl.when(t == 0)
    def _():
        c_sc[...] = jnp.zeros_like(c_sc)
        n_sc[...] = jnp.zeros_like(n_sc)
        m_sc[...] = jnp.zeros_like(m_sc)
        xx_sc[0:pool_pad, :] = jnp.zeros((pool_pad, gw), F32)
        xd_sc[0:conv_pad, :] = jnp.zeros((conv_pad, gw), F32)

    ri = lax.broadcasted_iota(jnp.int32, (HEAD_DIM, HEAD_DIM), 0)
    ci = lax.broadcasted_iota(jnp.int32, (HEAD_DIM, HEAD_DIM), 1)

    u = _gelu(au_ref[...])
    v = _gelu(av_ref[...]).astype(BF16)
    for h in range(N_HEADS):
        hs = slice(h * HEAD_DIM, (h + 1) * HEAD_DIM)
        w_tril = jnp.where(ri >= ci, ws_ref[h], 0.0).astype(BF16)
        for c in range(rows // HEAD_DIM):
            rs = slice(c * HEAD_DIM, (c + 1) * HEAD_DIM)
            mixed = jnp.dot(w_tril, v[rs, hs], preferred_element_type=F32) + bs_ref[h]
            y_ref[rs, hs] = (u[rs, hs] * mixed).astype(y_ref.dtype)

    bx = bx_ref[...]
    xx_sc[pool_pad:pool_pad + rows, :] = bx
    avail = (t * rows + 1 + lax.broadcasted_iota(jnp.int32, (rows, 1), 0)).astype(F32)
    for g, win in enumerate(POOL_WINDOWS):
        gs = slice(g * HEAD_DIM, (g + 1) * HEAD_DIM)
        acc = bx[:, gs]
        for k in range(1, win):
            acc = acc + xx_sc[pool_pad - k:pool_pad - k + rows, gs]
        pooled = acc / jnp.minimum(avail, float(win)) - bx[:, gs]
        yb = jnp.dot(pooled.astype(BF16), wp_ref[g].astype(BF16), preferred_element_type=F32) * ps_ref[:, gs]
        y_ref[:, gw + g * HEAD_DIM:gw + (g + 1) * HEAD_DIM] = yb.astype(y_ref.dtype)
    xx_sc[0:pool_pad, :] = xx_sc[rows:rows + pool_pad, :]

    dcx = dc_ref[...] * dx_ref[...]
    xd_sc[conv_pad:conv_pad + rows, :] = dcx
    conv = (wd_ref[0:1, :] * xd_sc[conv_pad - 2:conv_pad - 2 + rows, :]
            + wd_ref[1:2, :] * xd_sc[conv_pad - 1:conv_pad - 1 + rows, :]
            + wd_ref[2:3, :] * dcx)
    y_ref[:, 3 * gw:4 * gw] = (db_ref[...] * conv).astype(y_ref.dtype)
    cs_ref[...] = xd_sc[conv_pad + rows - 2:conv_pad + rows, :]
    xd_sc[0:conv_pad, :] = xd_sc[rows:rows + conv_pad, :]

    gc = gc_ref[...] + bgr_ref[...]
    gr = gr_ref[...] + bgc_ref[...]
    rr = lax.broadcasted_iota(jnp.int32, (rows, rows), 0)
    cc = lax.broadcasted_iota(jnp.int32, (rows, rows), 1)
    causal = rr >= cc
    b_cols = jnp.dot(causal.astype(F32), _log_sigmoid(gc), precision=HIGHEST, preferred_element_type=F32)
    b_rows = jnp.dot(_log_sigmoid(gr), (rr <= cc).astype(F32), precision=HIGHEST, preferred_element_type=F32)
    scale = HEAD_DIM ** -0.5
    for h in range(N_HEADS):
        hs = slice(h * HEAD_DIM, (h + 1) * HEAD_DIM)
        bc = _lane_col(b_cols, N_HEADS + h)
        ic = _lane_col(gc, h)
        br = b_rows[N_HEADS + h:N_HEADS + h + 1, :]
        ir = gr[h:h + 1, :]
        m_prev = m_sc[h:h + 1, 0:1]
        log_d = jnp.where(causal, bc - br + ir, -jnp.inf)
        log_inter = bc + m_prev
        m_t = jnp.maximum(log_inter, jnp.max(log_d, axis=-1, keepdims=True))
        dmat = jnp.exp(log_d - m_t)
        inter = jnp.exp(log_inter - m_t)
        qf = cq_ref[:, hs]
        kf = ck_ref[:, hs] * scale
        vf = cv_ref[:, hs]
        q = qf.astype(BF16)
        k = kf.astype(BF16)
        s = lax.dot_general(q, k, _NT, preferred_element_type=F32) * dmat
        c_prev = c_sc[h]
        n_prev = n_sc[h:h + 1, :]
        num = (jnp.dot(s.astype(BF16), vf.astype(BF16), preferred_element_type=F32)
               + inter * lax.dot_general(q, c_prev.astype(BF16), _NT, preferred_element_type=F32))
        den = jnp.sum(s, axis=-1, keepdims=True) + inter * jnp.sum(qf * n_prev, axis=-1, keepdims=True)
        hout = num / jnp.maximum(jnp.abs(den), jnp.exp(-m_t))
        hout = hout * lax.rsqrt(jnp.mean(hout * hout, axis=-1, keepdims=True) + EPS)
        yc = _sigmoid(co_ref[:, hs]) * hout * mn_ref[:, hs]
        y_ref[:, 2 * gw + h * HEAD_DIM:2 * gw + (h + 1) * HEAD_DIM] = yc.astype(y_ref.dtype)
        m_new = m_t[rows - 1:rows, :]
        b_last = bc[rows - 1:rows, :]
        decay = jnp.exp(b_last + m_prev - m_new)
        wts = jnp.exp(b_last - bc + ic - m_new)
        wv_t = jnp.transpose(wts * vf).astype(BF16)
        c_sc[h] = decay * c_prev + jnp.dot(wv_t, k, preferred_element_type=F32)
        n_sc[h:h + 1, :] = decay * n_prev + jnp.sum(wts * kf, axis=0, keepdims=True)
        m_sc[h:h + 1, :] = jnp.broadcast_to(m_new, (1, LANES))

    @pl.when(t == pl.num_programs(1) - 1)
    def _():
        c_ref[...] = c_sc[...]
        n_ref[...] = n_sc[...]
        m_ref[...] = m_sc[...]


def _mix_prompt(z, gates, gates_t, lw, n_seq, seq, rows=256):
    gw = GROUP_WIDTH
    tps = seq // rows
    zspecs = [pl.BlockSpec((rows, gw), functools.partial(lambda n, t, j: (n * tps + t, j), j=j))
              for j in range(N_SLICES)]
    in_specs = zspecs + [
        pl.BlockSpec((rows, LANES), lambda n, t: (n * tps + t, 0)),
        pl.BlockSpec((SUBLANES, rows), lambda n, t: (0, n * tps + t)),
        _const_spec((N_HEADS, HEAD_DIM, HEAD_DIM)),
        _const_spec((N_HEADS, HEAD_DIM, 1)),
        _const_spec((N_HEADS, HEAD_DIM, HEAD_DIM)),
        _const_spec((1, gw)),
        _const_spec((1, gw)),
        _const_spec((1, LANES)),
        _const_spec((SUBLANES, 1)),
        _const_spec((3, gw)),
    ]
    out_shape = [
        jax.ShapeDtypeStruct((n_seq * seq, 4 * gw), BF16),
        jax.ShapeDtypeStruct((n_seq, 2, gw), F32),
        jax.ShapeDtypeStruct((n_seq, N_HEADS, HEAD_DIM, HEAD_DIM), F32),
        jax.ShapeDtypeStruct((n_seq, SUBLANES, HEAD_DIM), F32),
        jax.ShapeDtypeStruct((n_seq, SUBLANES, LANES), F32),
    ]
    out_specs = [
        pl.BlockSpec((rows, 4 * gw), lambda n, t: (n * tps + t, 0)),
        pl.BlockSpec((None, 2, gw), lambda n, t: (n, 0, 0)),
        pl.BlockSpec((None, N_HEADS, HEAD_DIM, HEAD_DIM), lambda n, t: (n, 0, 0, 0)),
        pl.BlockSpec((None, SUBLANES, HEAD_DIM), lambda n, t: (n, 0, 0)),
        pl.BlockSpec((None, SUBLANES, LANES), lambda n, t: (n, 0, 0)),
    ]
    scratch = [
        pltpu.VMEM((N_HEADS, HEAD_DIM, HEAD_DIM), F32),
        pltpu.VMEM((SUBLANES, HEAD_DIM), F32),
        pltpu.VMEM((SUBLANES, LANES), F32),
        pltpu.VMEM((2 * SUBLANES + rows, gw), F32),
        pltpu.VMEM((SUBLANES + rows, gw), F32),
    ]
    return pl.pallas_call(
        _mix_prompt_kernel,
        grid=(n_seq, tps),
        in_specs=in_specs,
        out_specs=out_specs,
        out_shape=out_shape,
        scratch_shapes=scratch,
        compiler_params=_params(2),
        name="mix_prompt",
    )(*([z] * N_SLICES), gates, gates_t, lw["w_s"], lw["b_s_col"], lw["w_pool"], lw["pool_scale"],
      lw["mlstm_norm"], lw["bg_row"], lw["bg_col"], lw["w_dconv"])


def _mix_sample_kernel(au_ref, av_ref, bx_ref, cq_ref, ck_ref, cv_ref, co_ref, db_ref, dc_ref, dx_ref,
                       g_ref, wst_ref, bst_ref, wp_ref, ps_ref, mn_ref, bgr_ref, wd_ref,
                       pp_ref, cp_ref, c0_ref, n0_ref, m0_ref,
                       y_ref, vr_ref, cs_ref, c1_ref, n1_ref, m1_ref):
    ts = au_ref.shape[0]
    nb = au_ref.shape[1]
    gw = GROUP_WIDTH

    v = [_gelu(av_ref[t]) for t in range(ts)]
    for t in range(ts):
        vr_ref[t] = v[t]
        mixed = bst_ref[t:t + 1, :]
        for s in range(t + 1):
            mixed = mixed + wst_ref[t * ts + s:t * ts + s + 1, :] * v[s]
        y_ref[t, :, 0:gw] = _gelu(au_ref[t]) * mixed

    bx = [bx_ref[t] for t in range(ts)]
    hist = [pp_ref[r] for r in range(POOL_STATE)] + bx
    pooled = []
    for t in range(ts):
        parts = []
        for g, win in enumerate(POOL_WINDOWS):
            gs = slice(g * HEAD_DIM, (g + 1) * HEAD_DIM)
            acc = hist[POOL_STATE + t][:, gs]
            for k in range(1, win):
                acc = acc + hist[POOL_STATE + t - k][:, gs]
            parts.append(acc / float(win) - bx[t][:, gs])
        pooled.append(parts)
    for g in range(len(POOL_WINDOWS)):
        gs = slice(g * HEAD_DIM, (g + 1) * HEAD_DIM)
        pg = jnp.concatenate([pooled[t][g] for t in range(ts)], axis=0).astype(BF16)
        yb = jnp.dot(pg, wp_ref[g].astype(BF16), preferred_element_type=F32) * ps_ref[:, gs]
        for t in range(ts):
            y_ref[t, :, gw + g * HEAD_DIM:gw + (g + 1) * HEAD_DIM] = yb[t * nb:(t + 1) * nb, :]

    dcx = [dc_ref[t] * dx_ref[t] for t in range(ts)]
    xx = [cp_ref[0], cp_ref[1]] + dcx
    for t in range(ts):
        conv = wd_ref[0:1, :] * xx[t] + wd_ref[1:2, :] * xx[t + 1] + wd_ref[2:3, :] * xx[t + 2]
        y_ref[t, :, 3 * gw:4 * gw] = db_ref[t] * conv
    cs_ref[0] = xx[ts]
    cs_ref[1] = xx[ts + 1]

    gate = [g_ref[t] + bgr_ref[...] for t in range(ts)]
    lf = [pltpu.roll(_log_sigmoid(gate[t]), LANES - N_HEADS, 1) for t in range(ts)]
    m0 = m0_ref[...]
    b = []
    for t in range(ts):
        b.append(lf[t] if t == 0 else b[t - 1] + lf[t])
    log_d = [[b[t] - b[s] + gate[s] for s in range(t + 1)] for t in range(ts)]
    m_t = []
    for t in range(ts):
        mt = b[t] + m0
        for s in range(t + 1):
            mt = jnp.maximum(mt, log_d[t][s])
        m_t.append(mt)
    dmat = [[jnp.exp(log_d[t][s] - m_t[t]) for s in range(t + 1)] for t in range(ts)]
    inter = [jnp.exp(b[t] + m0 - m_t[t]) for t in range(ts)]
    floor = [jnp.exp(-m_t[t]) for t in range(ts)]
    m_new = m_t[ts - 1]
    decay = jnp.exp(b[ts - 1] + m0 - m_new)
    wts = [jnp.exp(b[ts - 1] - b[s] + gate[s] - m_new) for s in range(ts)]
    m1_ref[...] = m_new

    scale = HEAD_DIM ** -0.5
    row_seq = lax.broadcasted_iota(jnp.int32, (ts * nb, HEAD_DIM), 0) % nb
    for h in range(N_HEADS):
        hs = slice(h * HEAD_DIM, (h + 1) * HEAD_DIM)
        q = [cq_ref[t][:, hs] for t in range(ts)]
        k = [ck_ref[t][:, hs] * scale for t in range(ts)]
        vv = [cv_ref[t][:, hs] for t in range(ts)]
        n_prev = n0_ref[:, hs]
        q_blk = jnp.concatenate(q, axis=0).astype(BF16)
        k_blk = jnp.concatenate(k, axis=0).astype(BF16)
        dec_col = _lane_col(decay, h)
        wt_col = [_lane_col(wts[s], h) for s in range(ts)]
        wv_blk = jnp.concatenate([wt_col[s] * vv[s] for s in range(ts)], axis=0)
        qc = jnp.zeros((ts * nb, HEAD_DIM), F32)
        for i in range(nb):
            c_prev = c0_ref[i, h]
            r = lax.dot_general(q_blk, c_prev.astype(BF16), _NT, preferred_element_type=F32)
            qc = jnp.where(row_seq == i, r, qc)
            wv_i = jnp.where(row_seq == i, wv_blk, 0.0).astype(BF16)
            upd = lax.dot_general(wv_i, k_blk, (((0,), (0,)), ((), ())), preferred_element_type=F32)
            c1_ref[i, h] = dec_col[i:i + 1, :] * c_prev + upd
        n_new = dec_col * n_prev
        for s in range(ts):
            n_new = n_new + wt_col[s] * k[s]
        n1_ref[:, hs] = n_new
        for t in range(ts):
            int_col = _lane_col(inter[t], h)
            num = int_col * qc[t * nb:(t + 1) * nb, :]
            den = int_col * jnp.sum(q[t] * n_prev, axis=-1, keepdims=True)
            for s in range(t + 1):
                s_ts = jnp.sum(q[t] * k[s], axis=-1, keepdims=True) * _lane_col(dmat[t][s], h)
                num = num + s_ts * vv[s]
                den = den + s_ts
            hout = num / jnp.maximum(jnp.abs(den), _lane_col(floor[t], h))
            hout = hout * lax.rsqrt(jnp.mean(hout * hout, axis=-1, keepdims=True) + EPS)
            y_ref[t, :, 2 * gw + h * HEAD_DIM:2 * gw + (h + 1) * HEAD_DIM] = (
                _sigmoid(co_ref[t][:, hs]) * hout * mn_ref[:, hs])


def _mix_sample(z3, gates3, lw, pool_prefix, conv_prefix, c_all, n_all, m_all, layer, nb=8):
    ts, n_seq, _ = z3.shape
    gw = GROUP_WIDTH
    zspecs = [pl.BlockSpec((ts, nb, gw), functools.partial(lambda i, j: (0, i, j), j=j)) for j in range(N_SLICES)]
    in_specs = zspecs + [
        pl.BlockSpec((ts, nb, LANES), lambda i: (0, i, 0)),
        _const_spec((ts * ts, gw)),
        _const_spec((ts, gw)),
        _const_spec((N_HEADS, HEAD_DIM, HEAD_DIM)),
        _const_spec((1, gw)),
        _const_spec((1, gw)),
        _const_spec((1, LANES)),
        _const_spec((3, gw)),
        pl.BlockSpec((POOL_STATE, nb, gw), lambda i: (0, i, 0)),
        pl.BlockSpec((2, nb, gw), lambda i: (0, i, 0)),
        pl.BlockSpec((None, nb, N_HEADS, HEAD_DIM, HEAD_DIM), lambda i: (layer, i, 0, 0, 0)),
        pl.BlockSpec((None, nb, gw), lambda i: (layer, i, 0)),
        pl.BlockSpec((None, nb, LANES), lambda i: (layer, i, 0)),
    ]
    out_shape = [
        jax.ShapeDtypeStruct((ts, n_seq, 4 * gw), F32),
        jax.ShapeDtypeStruct((ts, n_seq, gw), F32),
        jax.ShapeDtypeStruct((2, n_seq, gw), F32),
        jax.ShapeDtypeStruct((n_seq, N_HEADS, HEAD_DIM, HEAD_DIM), F32),
        jax.ShapeDtypeStruct((n_seq, gw), F32),
        jax.ShapeDtypeStruct((n_seq, LANES), F32),
    ]
    out_specs = [
        pl.BlockSpec((ts, nb, 4 * gw), lambda i: (0, i, 0)),
        pl.BlockSpec((ts, nb, gw), lambda i: (0, i, 0)),
        pl.BlockSpec((2, nb, gw), lambda i: (0, i, 0)),
        pl.BlockSpec((nb, N_HEADS, HEAD_DIM, HEAD_DIM), lambda i: (i, 0, 0, 0)),
        pl.BlockSpec((nb, gw), lambda i: (i, 0)),
        pl.BlockSpec((nb, LANES), lambda i: (i, 0)),
    ]
    return pl.pallas_call(
        _mix_sample_kernel,
        grid=(n_seq // nb,),
        in_specs=in_specs,
        out_specs=out_specs,
        out_shape=out_shape,
        compiler_params=_params(1),
        name="mix_sample",
    )(*([z3] * N_SLICES), gates3, lw["ws_tab"], lw["bs_tab"], lw["w_pool"], lw["pool_scale"],
      lw["mlstm_norm"], lw["bg_row"], lw["w_dconv"], pool_prefix, conv_prefix, c_all, n_all, m_all)


def _out_q_kernel(x_ref, y_ref, wo_ref, g_ref, wq_ref, x1_ref, q_ref):
    x1 = x_ref[...] + jnp.dot(y_ref[...].astype(BF16), wo_ref[...], preferred_element_type=F32)
    x1_ref[...] = x1
    q_ref[...] = jnp.dot(_rms(x1, g_ref[...]).astype(BF16), wq_ref[...], preferred_element_type=F32)


def _out_q(x, y, w_out, g, w_q, tm=512):
    m, d = x.shape
    tm = min(tm, m)
    nq = w_q.shape[1]
    return pl.pallas_call(
        _out_q_kernel,
        grid=(m // tm,),
        in_specs=[
            pl.BlockSpec((tm, d), lambda i: (i, 0)),
            pl.BlockSpec((tm, y.shape[1]), lambda i: (i, 0)),
            _const_spec(w_out.shape),
            _const_spec((1, d)),
            _const_spec(w_q.shape),
        ],
        out_specs=[pl.BlockSpec((tm, d), lambda i: (i, 0)), pl.BlockSpec((tm, nq), lambda i: (i, 0))],
        out_shape=[jax.ShapeDtypeStruct((m, d), F32), jax.ShapeDtypeStruct((m, nq), F32)],
        compiler_params=_params(1),
        name="out_q",
    )(x, y, w_out, g, w_q)


def _softmax_rows(s):
    e = jnp.exp(s - jnp.max(s, axis=-1, keepdims=True))
    return e / jnp.sum(e, axis=-1, keepdims=True)


def _attn_prompt_kernel(q_ref, k_ref, v_ref, o_ref):
    scale = HEAD_DIM ** -0.5
    for h in range(N_HEADS):
        hs = slice(h * HEAD_DIM, (h + 1) * HEAD_DIM)
        q = q_ref[:, hs].astype(BF16)
        s = lax.dot_general(q, k_ref[:, hs].astype(BF16), _NT, preferred_element_type=F32) * scale
        p = _softmax_rows(s).astype(BF16)
        o_ref[:, hs] = jnp.dot(p, v_ref[:, hs].astype(BF16), preferred_element_type=F32)


def _attn_prompt(q, k, v, n_seq, seq, tq=512):
    tps = seq // tq
    w = q.shape[1]
    return pl.pallas_call(
        _attn_prompt_kernel,
        grid=(n_seq, tps),
        in_specs=[
            pl.BlockSpec((tq, w), lambda n, t: (n * tps + t, 0)),
            pl.BlockSpec((None, N_MEM, w), lambda n, t: (n, 0, 0)),
            pl.BlockSpec((None, N_MEM, w), lambda n, t: (n, 0, 0)),
        ],
        out_specs=pl.BlockSpec((tq, w), lambda n, t: (n * tps + t, 0)),
        out_shape=jax.ShapeDtypeStruct(q.shape, F32),
        compiler_params=_params(2),
        name="attn_prompt",
    )(q, k, v)


def _attn_sample_kernel(q_ref, k_ref, v_ref, o_ref):
    ts, nb, w = q_ref.shape
    scale = HEAD_DIM ** -0.5
    row_seq = lax.broadcasted_iota(jnp.int32, (ts * nb, HEAD_DIM), 0) % nb
    q_all = jnp.concatenate([q_ref[t] for t in range(ts)], axis=0).astype(BF16)
    for h in range(N_HEADS):
        hs = slice(h * HEAD_DIM, (h + 1) * HEAD_DIM)
        acc = jnp.zeros((ts * nb, HEAD_DIM), F32)
        for i in range(nb):
            s = lax.dot_general(q_all[:, hs], k_ref[i, :, hs].astype(BF16), _NT, preferred_element_type=F32) * scale
            p = _softmax_rows(s).astype(BF16)
            o = jnp.dot(p, v_ref[i, :, hs].astype(BF16), preferred_element_type=F32)
            acc = jnp.where(row_seq == i, o, acc)
        for t in range(ts):
            o_ref[t, :, hs] = acc[t * nb:(t + 1) * nb, :]


def _attn_sample(q3, k_all, v_all, layer, nb=8):
    ts, n_seq, w = q3.shape
    return pl.pallas_call(
        _attn_sample_kernel,
        grid=(n_seq // nb,),
        in_specs=[
            pl.BlockSpec((ts, nb, w), lambda i: (0, i, 0)),
            pl.BlockSpec((None, nb, N_MEM, w), lambda i: (layer, i, 0, 0)),
            pl.BlockSpec((None, nb, N_MEM, w), lambda i: (layer, i, 0, 0)),
        ],
        out_specs=pl.BlockSpec((ts, nb, w), lambda i: (0, i, 0)),
        out_shape=jax.ShapeDtypeStruct(q3.shape, F32),
        compiler_params=_params(1),
        name="attn_sample",
    )(q3, k_all, v_all)


def _ffn_kernel(x1_ref, o_ref, wo_ref, g_ref, wg_ref, wu_ref, fg_ref, fu_ref, wd_ref, pg_ref, pu_ref,
                out_ref, tg_ref, tu_ref, hn_sc, eg_sc, eu_sc, cg_sc, cu_sc, *, shift, tiles_per_seq):
    i = pl.program_id(0)
    j = pl.program_id(1)
    tm = out_ref.shape[0]
    pad = pg_ref.shape[0]

    @pl.when(j == 0)
    def _():
        x2 = x1_ref[...] + jnp.dot(o_ref[...].astype(BF16), wo_ref[...], preferred_element_type=F32)
        out_ref[...] = x2
        hn_sc[...] = _rms(x2, g_ref[...]).astype(BF16)

    first = (i % tiles_per_seq) == 0

    @pl.when(first)
    def _():
        eg_sc[0:pad, :] = pg_ref[...]
        eu_sc[0:pad, :] = pu_ref[...]

    @pl.when(jnp.logical_not(first))
    def _():
        eg_sc[0:pad, :] = cg_sc[j]
        eu_sc[0:pad, :] = cu_sc[j]

    hn = hn_sc[...]
    eg_sc[pad:pad + tm, :] = jnp.dot(hn, wg_ref[...], preferred_element_type=F32)
    eu_sc[pad:pad + tm, :] = jnp.dot(hn, wu_ref[...], preferred_element_type=F32)
    tail_g = eg_sc[tm:tm + pad, :]
    tail_u = eu_sc[tm:tm + pad, :]
    cg_sc[j] = tail_g
    cu_sc[j] = tail_u
    tg_ref[...] = tail_g
    tu_ref[...] = tail_u

    def conv(e_sc, f_ref):
        return (f_ref[0:1, :] * e_sc[pad - 2 * shift:pad - 2 * shift + tm, :]
                + f_ref[1:2, :] * e_sc[pad - shift:pad - shift + tm, :]
                + f_ref[2:3, :] * e_sc[pad:pad + tm, :])

    gate = conv(eg_sc, fg_ref)
    act = (gate * _sigmoid(gate) * conv(eu_sc, fu_ref)).astype(BF16)
    for c in range(out_ref.shape[1] // DOWN_TILE):
        cs = slice(c * DOWN_TILE, (c + 1) * DOWN_TILE)
        out_ref[:, cs] += jnp.dot(act, wd_ref[:, cs], preferred_element_type=F32)


def _ffn(x1, o, w_o, g, w_up, f_conv, w_down, prefix_g, prefix_u, tm, shift, tiles_per_seq):
    m, d = x1.shape
    dfp = w_down.shape[0]
    tn = FF_TILE
    nj = dfp // tn
    pad = prefix_g.shape[0]
    n_tiles = m // tm
    return pl.pallas_call(
        functools.partial(_ffn_kernel, shift=shift, tiles_per_seq=tiles_per_seq),
        grid=(n_tiles, nj),
        in_specs=[
            pl.BlockSpec((tm, d), lambda i, j: (i, 0), pipeline_mode=pl.Buffered(1)),
            pl.BlockSpec((tm, o.shape[1]), lambda i, j: (i, 0)),
            _const_spec(w_o.shape),
            _const_spec((1, d)),
            pl.BlockSpec((d, tn), lambda i, j: (0, j)),
            pl.BlockSpec((d, tn), lambda i, j: (0, nj + j)),
            pl.BlockSpec((3, tn), lambda i, j: (0, j)),
            pl.BlockSpec((3, tn), lambda i, j: (0, nj + j)),
            pl.BlockSpec((tn, d), lambda i, j: (j, 0)),
            pl.BlockSpec((pad, tn), lambda i, j: (0, j)),
            pl.BlockSpec((pad, tn), lambda i, j: (0, j)),
        ],
        out_specs=[
            pl.BlockSpec((tm, d), lambda i, j: (i, 0)),
            pl.BlockSpec((pad, tn), lambda i, j: (i, j)),
            pl.BlockSpec((pad, tn), lambda i, j: (i, j)),
        ],
        out_shape=[
            jax.ShapeDtypeStruct((m, d), F32),
            jax.ShapeDtypeStruct((n_tiles * pad, dfp), F32),
            jax.ShapeDtypeStruct((n_tiles * pad, dfp), F32),
        ],
        scratch_shapes=[
            pltpu.VMEM((tm, d), BF16),
            pltpu.VMEM((pad + tm, tn), F32),
            pltpu.VMEM((pad + tm, tn), F32),
            pltpu.VMEM((nj, pad, tn), F32),
            pltpu.VMEM((nj, pad, tn), F32),
        ],
        compiler_params=_params(2),
        name="ffn",
    )(x1, o, w_o, g, w_up, w_up, f_conv, f_conv, w_down, prefix_g, prefix_u)


def _final_norm_kernel(x_ref, g_ref, o_ref):
    o_ref[...] = _rms(x_ref[...], g_ref[...])


def _final_norm(x, g, tm=512):
    m, d = x.shape
    tm = min(tm, m)
    return pl.pallas_call(
        _final_norm_kernel,
        grid=(m // tm,),
        in_specs=[pl.BlockSpec((tm, d), lambda i: (i, 0)), _const_spec((1, d))],
        out_specs=pl.BlockSpec((tm, d), lambda i: (i, 0)),
        out_shape=jax.ShapeDtypeStruct((m, d), F32),
        compiler_params=_params(1),
        name="final_norm",
    )(x, g)


def _pad_cols(a, width):
    return jnp.pad(a, [(0, 0)] * (a.ndim - 1) + [(0, width - a.shape[-1])])


def _layer_weights(l, ts, norm_mix, w_in, w_s, b_s, w_pool, pool_scale, b_gate, mlstm_norm, w_dconv, w_out,
                   norm_cross, w_q, w_o, norm_ffn, w_up, w_fconv, w_down):
    n_main = N_SLICES * GROUP_WIDTH
    d_ff = w_down.shape[1]
    dfp = -(-d_ff // FF_TILE) * FF_TILE
    up_g, up_u = w_up[l, :, :d_ff], w_up[l, :, d_ff:]
    fc_g, fc_u = w_fconv[l, :, :d_ff], w_fconv[l, :, d_ff:]
    ws_l, bs_l = w_s[l], b_s[l]
    return {
        "norm_mix": norm_mix[l][None, :],
        "w_in": w_in[l, :, :n_main].astype(BF16),
        "w_gate": _pad_cols(w_in[l, :, n_main:], LANES).astype(BF16),
        "w_s": ws_l,
        "b_s_col": bs_l[:, :, None],
        "ws_tab": jnp.repeat(jnp.transpose(ws_l[:, :ts, :ts], (1, 2, 0)).reshape(ts * ts, N_HEADS), HEAD_DIM, axis=1),
        "bs_tab": jnp.repeat(jnp.transpose(bs_l[:, :ts], (1, 0)), HEAD_DIM, axis=1),
        "w_pool": w_pool[l],
        "pool_scale": pool_scale[l][None, :],
        "mlstm_norm": mlstm_norm[l][None, :],
        "bg_row": _pad_cols(b_gate[l][None, :], LANES),
        "bg_col": b_gate[l][:, None],
        "w_dconv": w_dconv[l],
        "w_out": w_out[l].astype(BF16),
        "norm_cross": norm_cross[l][None, :],
        "w_q": w_q[l].astype(BF16),
        "w_o": w_o[l].astype(BF16),
        "norm_ffn": norm_ffn[l][None, :],
        "w_up": jnp.concatenate([_pad_cols(up_g, dfp), _pad_cols(up_u, dfp)], axis=1).astype(BF16),
        "f_conv": jnp.concatenate([_pad_cols(fc_g, dfp), _pad_cols(fc_u, dfp)], axis=1),
        "w_down": jnp.pad(w_down[l], ((0, dfp - d_ff), (0, 0))).astype(BF16),
        "d_ff": d_ff,
        "dfp": dfp,
    }


def kernel(x_prompt, x_sample, mem_prompt, state_pool, state_conv, state_mlstm_c, state_mlstm_n, state_mlstm_m,
           state_ffn_conv, cache_mem_k, cache_mem_v, norm_mix, w_in, w_s, b_s, w_pool, pool_scale, b_gate,
           mlstm_norm, w_dconv, w_out, mem_norm, w_kv, norm_cross, w_q, w_o, norm_ffn, w_up, w_fconv, w_down,
           final_norm):
    depth = w_in.shape[0]
    nbp, seq, d = x_prompt.shape
    nbs, ts, _ = x_sample.shape
    gw = GROUP_WIDTH
    mem_w = w_q.shape[2]

    xp = x_prompt.reshape(nbp * seq, d)
    xs = jnp.transpose(x_sample, (1, 0, 2)).reshape(ts * nbs, d)
    mem2 = mem_prompt.reshape(nbp * N_MEM, d)
    kc_all = cache_mem_k.reshape(depth, nbs, N_MEM, mem_w)
    vc_all = cache_mem_v.reshape(depth, nbs, N_MEM, mem_w)
    n_all = state_mlstm_n.reshape(depth, nbs, gw)
    m_all = _pad_cols(state_mlstm_m, LANES)

    tm_p = min(1024, seq)
    tm_s = ts * nbs
    pad_p = SUBLANES
    outs = {k: [] for k in ("p_pool", "p_conv", "p_c", "p_n", "p_m", "p_ffn", "p_mk", "p_mv",
                            "s_v", "s_pool", "s_conv", "s_c", "s_n", "s_m", "s_ffn")}
    for l in range(depth):
        lw = _layer_weights(l, ts, norm_mix, w_in, w_s, b_s, w_pool, pool_scale, b_gate, mlstm_norm, w_dconv,
                            w_out, norm_cross, w_q, w_o, norm_ffn, w_up, w_fconv, w_down)
        d_ff, dfp = lw["d_ff"], lw["dfp"]

        kv = _norm_mm(mem2, mem_norm[l][None, :], w_kv[l].astype(BF16), tm=mem2.shape[0], tn=512)
        mk, mv = kv[:, :mem_w], kv[:, mem_w:]
        outs["p_mk"].append(mk.reshape(nbp, N_MEM, N_HEADS, HEAD_DIM))
        outs["p_mv"].append(mv.reshape(nbp, N_MEM, N_HEADS, HEAD_DIM))

        z, gates = _proj_in(xp, lw["norm_mix"], lw["w_in"], lw["w_gate"], tm=tm_p)
        gates_t = jnp.transpose(gates[:, :SUBLANES])
        y, conv_st, c1, n1, m1 = _mix_prompt(z, gates, gates_t, lw, nbp, seq)
        z3 = z.reshape(nbp, seq, N_SLICES * gw)
        outs["p_pool"].append(z3[:, seq - POOL_STATE:, 2 * gw:3 * gw])
        outs["p_conv"].append(conv_st)
        outs["p_c"].append(c1)
        outs["p_n"].append(n1[:, :N_HEADS, :])
        outs["p_m"].append(m1[:, :N_HEADS, 0])
        x1, q = _out_q(xp, y, lw["w_out"], lw["norm_cross"], lw["w_q"])
        o = _attn_prompt(q, mk.reshape(nbp, N_MEM, mem_w), mv.reshape(nbp, N_MEM, mem_w), nbp, seq)
        zeros_pre = jnp.zeros((pad_p, dfp), F32)
        xp, tail_g, tail_u = _ffn(x1, o, lw["w_o"], lw["norm_ffn"], lw["w_up"], lw["f_conv"], lw["w_down"],
                                  zeros_pre, zeros_pre, tm=tm_p, shift=1, tiles_per_seq=seq // tm_p)
        tg = tail_g.reshape(nbp, seq // tm_p, pad_p, dfp)[:, -1, pad_p - 2:, :d_ff]
        tu = tail_u.reshape(nbp, seq // tm_p, pad_p, dfp)[:, -1, pad_p - 2:, :d_ff]
        outs["p_ffn"].append(jnp.concatenate([tg, tu], axis=-1))

        z, gates = _proj_in(xs, lw["norm_mix"], lw["w_in"], lw["w_gate"], tm=tm_s)
        z3 = z.reshape(ts, nbs, N_SLICES * gw)
        pool_prefix = jnp.transpose(state_pool[l], (1, 0, 2))
        conv_prefix = jnp.transpose(state_conv[l], (1, 0, 2))
        y3, v_rows, conv_st, c1, n1, m1 = _mix_sample(
            z3, gates.reshape(ts, nbs, LANES), lw, pool_prefix, conv_prefix, state_mlstm_c, n_all, m_all, l)
        outs["s_v"].append(jnp.transpose(v_rows, (1, 0, 2)))
        bx_new = jnp.transpose(z3[:, :, 2 * gw:3 * gw], (1, 0, 2))
        outs["s_pool"].append(jnp.concatenate([state_pool[l], bx_new], axis=1)[:, -POOL_STATE:])
        outs["s_conv"].append(jnp.transpose(conv_st, (1, 0, 2)))
        outs["s_c"].append(c1)
        outs["s_n"].append(n1.reshape(nbs, N_HEADS, HEAD_DIM))
        outs["s_m"].append(m1[:, :N_HEADS])
        x1, q = _out_q(xs, y3.reshape(ts * nbs, 4 * gw), lw["w_out"], lw["norm_cross"], lw["w_q"])
        o3 = _attn_sample(q.reshape(ts, nbs, mem_w), kc_all, vc_all, l)
        ffn_pre = jnp.transpose(state_ffn_conv[l], (1, 0, 2)).reshape(2 * nbs, 2 * d_ff)
        xs, tail_g, tail_u = _ffn(x1, o3.reshape(ts * nbs, mem_w), lw["w_o"], lw["norm_ffn"], lw["w_up"],
                                  lw["f_conv"], lw["w_down"], _pad_cols(ffn_pre[:, :d_ff], dfp),
                                  _pad_cols(ffn_pre[:, d_ff:], dfp), tm=tm_s, shift=nbs, tiles_per_seq=1)
        tail = jnp.concatenate([tail_g[:, :d_ff], tail_u[:, :d_ff]], axis=-1).reshape(2, nbs, 2 * d_ff)
        outs["s_ffn"].append(jnp.transpose(tail, (1, 0, 2)))

    y_prompt = _final_norm(xp, final_norm[None, :]).reshape(nbp, seq, d)
    y_sample = jnp.transpose(_final_norm(xs, final_norm[None, :]).reshape(ts, nbs, d), (1, 0, 2))
    st = {k: jnp.stack(v) for k, v in outs.items()}
    return (y_prompt, y_sample,
            st["p_pool"], st["p_conv"], st["p_c"], st["p_n"], st["p_m"], st["p_ffn"], st["p_mk"], st["p_mv"],
            st["s_v"], st["s_pool"], st["s_conv"], st["s_c"], st["s_n"], st["s_m"], st["s_ffn"])
```

```python
import functools

import jax
import jax.numpy as jnp
from jax import lax
from jax.experimental import pallas as pl
from jax.experimental.pallas import tpu as pltpu

F32 = jnp.float32
BF16 = jnp.bfloat16
EPS = 1e-6

LANES = 128
SUBLANES = 8
GROUP_WIDTH = 512
HEAD_DIM = 128
N_HEADS = GROUP_WIDTH // HEAD_DIM
N_SLICES = 10
POOL_WINDOWS = (2, 4, 8, 16)
POOL_STATE = max(POOL_WINDOWS) - 1
N_MEM = 256
FF_TILE = 512
VMEM_LIMIT = 60 * 1024 * 1024
HIGHEST = lax.Precision.HIGHEST

_NT = (((1,), (1,)), ((), ()))


def _params(n_axes, vmem=VMEM_LIMIT):
    return pltpu.CompilerParams(dimension_semantics=("arbitrary",) * n_axes, vmem_limit_bytes=vmem)


def _rms(x, g):
    return x * lax.rsqrt(jnp.mean(x * x, axis=-1, keepdims=True) + EPS) * g


def _gelu(x):
    return 0.5 * x * (1.0 + jnp.tanh(0.7978845608028654 * (x + 0.044715 * (x * x * x))))


def _sigmoid(x):
    return 1.0 / (1.0 + jnp.exp(-x))


def _log_sigmoid(x):
    return jnp.minimum(x, 0.0) - jnp.log(1.0 + jnp.exp(-jnp.abs(x)))


def _lane_col(x, lane):
    idx = lax.broadcasted_iota(jnp.int32, x.shape, 1)
    return jnp.sum(jnp.where(idx == lane, x, 0.0), axis=-1, keepdims=True)


def _const_spec(shape):
    nd = len(shape)
    return pl.BlockSpec(shape, lambda *_: (0,) * nd, pipeline_mode=pl.Buffered(1))


def _proj_in_kernel(x_ref, g_ref, w_ref, wg_ref, z_ref, gate_ref, hn_sc):
    @pl.when(pl.program_id(1) == 0)
    def _():
        hn_sc[...] = _rms(x_ref[...], g_ref[...]).astype(BF16)
        gate_ref[...] = jnp.dot(hn_sc[...], wg_ref[...], preferred_element_type=F32)

    z_ref[...] = jnp.dot(hn_sc[...], w_ref[...], preferred_element_type=F32)


def _proj_in(x, g, w_all, wg_all, layer, n, tm, tn=1024):
    m, d = x.shape
    return pl.pallas_call(
        _proj_in_kernel,
        grid=(m // tm, n // tn),
        in_specs=[
            pl.BlockSpec((tm, d), lambda i, j: (i, 0)),
            _const_spec((1, d)),
            pl.BlockSpec((None, d, tn), lambda i, j: (layer, 0, j)),
            pl.BlockSpec((None, d, LANES), lambda i, j: (layer, 0, 0), pipeline_mode=pl.Buffered(1)),
        ],
        out_specs=[
            pl.BlockSpec((tm, tn), lambda i, j: (i, j)),
            pl.BlockSpec((tm, LANES), lambda i, j: (i, 0)),
        ],
        out_shape=[jax.ShapeDtypeStruct((m, n), F32), jax.ShapeDtypeStruct((m, LANES), F32)],
        scratch_shapes=[pltpu.VMEM((tm, d), BF16)],
        compiler_params=_params(2),
        name="proj_in",
    )(x, g, w_all, wg_all)


def _norm_mm_kernel(x_ref, g_ref, w_ref, o_ref, hn_sc):
    @pl.when(pl.program_id(1) == 0)
    def _():
        hn_sc[...] = _rms(x_ref[...], g_ref[...]).astype(BF16)

    o_ref[...] = jnp.dot(hn_sc[...], w_ref[...], preferred_element_type=F32)


def _norm_mm(x, g, w_all, layer, tm, tn):
    m, d = x.shape
    n = w_all.shape[2]
    return pl.pallas_call(
        _norm_mm_kernel,
        grid=(m // tm, n // tn),
        in_specs=[
            pl.BlockSpec((tm, d), lambda i, j: (i, 0)),
            _const_spec((1, d)),
            pl.BlockSpec((None, d, tn), lambda i, j: (layer, 0, j)),
        ],
        out_specs=pl.BlockSpec((tm, tn), lambda i, j: (i, j)),
        out_shape=jax.ShapeDtypeStruct((m, n), F32),
        scratch_shapes=[pltpu.VMEM((tm, d), BF16)],
        compiler_params=_params(2),
        name="norm_mm",
    )(x, g, w_all)


def _mix_prompt_kernel(au_ref, av_ref, bx_ref, cq_ref, ck_ref, cv_ref, co_ref, db_ref, dc_ref, dx_ref,
                       gc_ref, gr_ref, ws_ref, bs_ref, wp_ref, ps_ref, mn_ref, bgr_ref, bgc_ref, wd_ref,
                       y_ref, cs_ref, c_ref, n_ref, m_ref,
                       c_sc, n_sc, m_sc, xx_sc, xd_sc):
    t = pl.program_id(1)
    rows = y_ref.shape[0]
    gw = GROUP_WIDTH
    pool_pad = 2 * SUBLANES
    conv_pad = SUBLANES

    @pl.when(t == 0)
    def _():
        c_sc[...] = jnp.zeros_like(c_sc)
        n_sc[...] = jnp.zeros_like(n_sc)
        m_sc[...] = jnp.zeros_like(m_sc)
        xx_sc[0:pool_pad, :] = jnp.zeros((pool_pad, gw), F32)
        xd_sc[0:conv_pad, :] = jnp.zeros((conv_pad, gw), F32)

    ri = lax.broadcasted_iota(jnp.int32, (HEAD_DIM, HEAD_DIM), 0)
    ci = lax.broadcasted_iota(jnp.int32, (HEAD_DIM, HEAD_DIM), 1)

    u = _gelu(au_ref[...])
    v = _gelu(av_ref[...]).astype(BF16)
    for h in range(N_HEADS):
        hs = slice(h * HEAD_DIM, (h + 1) * HEAD_DIM)
        w_tril = jnp.where(ri >= ci, ws_ref[h], 0.0).astype(BF16)
        for c in range(rows // HEAD_DIM):
            rs = slice(c * HEAD_DIM, (c + 1) * HEAD_DIM)
            mixed = jnp.dot(w_tril, v[rs, hs], preferred_element_type=F32) + bs_ref[h]
            y_ref[rs, hs] = (u[rs, hs] * mixed).astype(y_ref.dtype)

    bx = bx_ref[...]
    xx_sc[pool_pad:pool_pad + rows, :] = bx
    avail = (t * rows + 1 + lax.broadcasted_iota(jnp.int32, (rows, 1), 0)).astype(F32)
    for g, win in enumerate(POOL_WINDOWS):
        gs = slice(g * HEAD_DIM, (g + 1) * HEAD_DIM)
        acc = bx[:, gs]
        for k in range(1, win):
            acc = acc + xx_sc[pool_pad - k:pool_pad - k + rows, gs]
        pooled = acc / jnp.minimum(avail, float(win)) - bx[:, gs]
        yb = jnp.dot(pooled.astype(BF16), wp_ref[g].astype(BF16), preferred_element_type=F32) * ps_ref[:, gs]
        y_ref[:, gw + g * HEAD_DIM:gw + (g + 1) * HEAD_DIM] = yb.astype(y_ref.dtype)
    xx_sc[0:pool_pad, :] = xx_sc[rows:rows + pool_pad, :]

    dcx = dc_ref[...] * dx_ref[...]
    xd_sc[conv_pad:conv_pad + rows, :] = dcx
    conv = (wd_ref[0:1, :] * xd_sc[conv_pad - 2:conv_pad - 2 + rows, :]
            + wd_ref[1:2, :] * xd_sc[conv_pad - 1:conv_pad - 1 + rows, :]
            + wd_ref[2:3, :] * dcx)
    y_ref[:, 3 * gw:4 * gw] = (db_ref[...] * conv).astype(y_ref.dtype)
    cs_ref[...] = xd_sc[conv_pad + rows - 2:conv_pad + rows, :]
    xd_sc[0:conv_pad, :] = xd_sc[rows:rows + conv_pad, :]

    gc = gc_ref[...] + bgr_ref[...]
    gr = gr_ref[...] + bgc_ref[...]
    rr = lax.broadcasted_iota(jnp.int32, (rows, rows), 0)
    cc = lax.broadcasted_iota(jnp.int32, (rows, rows), 1)
    causal = rr >= cc
    b_cols = jnp.dot(causal.astype(F32), _log_sigmoid(gc), precision=HIGHEST, preferred_element_type=F32)
    b_rows = jnp.dot(_log_sigmoid(gr), (rr <= cc).astype(F32), precision=HIGHEST, preferred_element_type=F32)
    scale = HEAD_DIM ** -0.5
    for h in range(N_HEADS):
        hs = slice(h * HEAD_DIM, (h + 1) * HEAD_DIM)
        bc = _lane_col(b_cols, N_HEADS + h)
        ic = _lane_col(gc, h)
        br = b_rows[N_HEADS + h:N_HEADS + h + 1, :]
        ir = gr[h:h + 1, :]
        m_prev = m_sc[h:h + 1, 0:1]
        log_d = jnp.where(causal, bc - br + ir, -jnp.inf)
        log_inter = bc + m_prev
        m_t = jnp.maximum(log_inter, jnp.max(log_d, axis=-1, keepdims=True))
        dmat = jnp.exp(log_d - m_t)
        inter = jnp.exp(log_inter - m_t)
        qf = cq_ref[:, hs]
        kf = ck_ref[:, hs] * scale
        vf = cv_ref[:, hs]
        q = qf.astype(BF16)
        k = kf.astype(BF16)
        s = lax.dot_general(q, k, _NT, preferred_element_type=F32) * dmat
        c_prev = c_sc[h]
        n_prev = n_sc[h:h + 1, :]
        num = (jnp.dot(s.astype(BF16), vf.astype(BF16), preferred_element_type=F32)
               + inter * lax.dot_general(q, c_prev.astype(BF16), _NT, preferred_element_type=F32))
        den = jnp.sum(s, axis=-1, keepdims=True) + inter * jnp.sum(qf * n_prev, axis=-1, keepdims=True)
        hout = num / jnp.maximum(jnp.abs(den), jnp.exp(-m_t))
        hout = hout * lax.rsqrt(jnp.mean(hout * hout, axis=-1, keepdims=True) + EPS)
        yc = _sigmoid(co_ref[:, hs]) * hout * mn_ref[:, hs]
        y_ref[:, 2 * gw + h * HEAD_DIM:2 * gw + (h + 1) * HEAD_DIM] = yc.astype(y_ref.dtype)
        m_new = m_t[rows - 1:rows, :]
        b_last = bc[rows - 1:rows, :]
        decay = jnp.exp(b_last + m_prev - m_new)
        wts = jnp.exp(b_last - bc + ic - m_new)
        wv_t = jnp.transpose(wts * vf).astype(BF16)
        c_sc[h] = decay * c_prev + jnp.dot(wv_t, k, preferred_element_type=F32)
        n_sc[h:h + 1, :] = decay * n_prev + jnp.sum(wts * kf, axis=0, keepdims=True)
        m_sc[h:h + 1, :] = jnp.broadcast_to(m_new, (1, LANES))

    @pl.when(t == pl.num_programs(1) - 1)
    def _():
        c_ref[...] = c_sc[...]
        n_ref[...] = n_sc[...]
        m_ref[...] = m_sc[...]


def _mix_prompt(z, gates, gates_t, lw, n_seq, seq, rows=256):
    gw = GROUP_WIDTH
    tps = seq // rows
    zspecs = [pl.BlockSpec((rows, gw), functools.partial(lambda n, t, j: (n * tps + t, j), j=j))
              for j in range(N_SLICES)]
    in_specs = zspecs + [
        pl.BlockSpec((rows, LANES), lambda n, t: (n * tps + t, 0)),
        pl.BlockSpec((SUBLANES, rows), lambda n, t: (0, n * tps + t)),
        _const_spec((N_HEADS, HEAD_DIM, HEAD_DIM)),
        _const_spec((N_HEADS, HEAD_DIM, 1)),
        _const_spec((N_HEADS, HEAD_DIM, HEAD_DIM)),
        _const_spec((1, gw)),
        _const_spec((1, gw)),
        _const_spec((1, LANES)),
        _const_spec((SUBLANES, 1)),
        _const_spec((3, gw)),
    ]
    out_shape = [
        jax.ShapeDtypeStruct((n_seq * seq, 4 * gw), BF16),
        jax.ShapeDtypeStruct((n_seq, 2, gw), F32),
        jax.ShapeDtypeStruct((n_seq, N_HEADS, HEAD_DIM, HEAD_DIM), F32),
        jax.ShapeDtypeStruct((n_seq, SUBLANES, HEAD_DIM), F32),
        jax.ShapeDtypeStruct((n_seq, SUBLANES, LANES), F32),
    ]
    out_specs = [
        pl.BlockSpec((rows, 4 * gw), lambda n, t: (n * tps + t, 0)),
        pl.BlockSpec((None, 2, gw), lambda n, t: (n, 0, 0)),
        pl.BlockSpec((None, N_HEADS, HEAD_DIM, HEAD_DIM), lambda n, t: (n, 0, 0, 0)),
        pl.BlockSpec((None, SUBLANES, HEAD_DIM), lambda n, t: (n, 0, 0)),
        pl.BlockSpec((None, SUBLANES, LANES), lambda n, t: (n, 0, 0)),
    ]
    scratch = [
        pltpu.VMEM((N_HEADS, HEAD_DIM, HEAD_DIM), F32),
        pltpu.VMEM((SUBLANES, HEAD_DIM), F32),
        pltpu.VMEM((SUBLANES, LANES), F32),
        pltpu.VMEM((2 * SUBLANES + rows, gw), F32),
        pltpu.VMEM((SUBLANES + rows, gw), F32),
    ]
    return pl.pallas_call(
        _mix_prompt_kernel,
        grid=(n_seq, tps),
        in_specs=in_specs,
        out_specs=out_specs,
        out_shape=out_shape,
        scratch_shapes=scratch,
        compiler_params=_params(2),
        name="mix_prompt",
    )(*([z] * N_SLICES), gates, gates_t, lw["w_s"], lw["b_s_col"], lw["w_pool"], lw["pool_scale"],
      lw["mlstm_norm"], lw["bg_row"], lw["bg_col"], lw["w_dconv"])


def _mix_sample_kernel(au_ref, av_ref, bx_ref, cq_ref, ck_ref, cv_ref, co_ref, db_ref, dc_ref, dx_ref,
                       g_ref, wst_ref, bst_ref, wp_ref, ps_ref, mn_ref, bgr_ref, wd_ref,
                       pp_ref, cp_ref, c0_ref, n0_ref, m0_ref,
                       y_ref, vr_ref, cs_ref, c1_ref, n1_ref, m1_ref):
    ts = au_ref.shape[0]
    nb = au_ref.shape[1]
    gw = GROUP_WIDTH

    v = [_gelu(av_ref[t]) for t in range(ts)]
    for t in range(ts):
        vr_ref[t] = v[t]
        mixed = bst_ref[t:t + 1, :]
        for s in range(t + 1):
            mixed = mixed + wst_ref[t * ts + s:t * ts + s + 1, :] * v[s]
        y_ref[t, :, 0:gw] = _gelu(au_ref[t]) * mixed

    bx = [bx_ref[t] for t in range(ts)]
    hist = [pp_ref[r] for r in range(POOL_STATE)] + bx
    pooled = []
    for t in range(ts):
        parts = []
        for g, win in enumerate(POOL_WINDOWS):
            gs = slice(g * HEAD_DIM, (g + 1) * HEAD_DIM)
            acc = hist[POOL_STATE + t][:, gs]
            for k in range(1, win):
                acc = acc + hist[POOL_STATE + t - k][:, gs]
            parts.append(acc / float(win) - bx[t][:, gs])
        pooled.append(parts)
    for g in range(len(POOL_WINDOWS)):
        gs = slice(g * HEAD_DIM, (g + 1) * HEAD_DIM)
        pg = jnp.concatenate([pooled[t][g] for t in range(ts)], axis=0).astype(BF16)
        yb = jnp.dot(pg, wp_ref[g].astype(BF16), preferred_element_type=F32) * ps_ref[:, gs]
        for t in range(ts):
            y_ref[t, :, gw + g * HEAD_DIM:gw + (g + 1) * HEAD_DIM] = yb[t * nb:(t + 1) * nb, :]

    dcx = [dc_ref[t] * dx_ref[t] for t in range(ts)]
    xx = [cp_ref[0], cp_ref[1]] + dcx
    for t in range(ts):
        conv = wd_ref[0:1, :] * xx[t] + wd_ref[1:2, :] * xx[t + 1] + wd_ref[2:3, :] * xx[t + 2]
        y_ref[t, :, 3 * gw:4 * gw] = db_ref[t] * conv
    cs_ref[0] = xx[ts]
    cs_ref[1] = xx[ts + 1]

    gate = [g_ref[t] + bgr_ref[...] for t in range(ts)]
    lf = [pltpu.roll(_log_sigmoid(gate[t]), LANES - N_HEADS, 1) for t in range(ts)]
    m0 = m0_ref[...]
    b = []
    for t in range(ts):
        b.append(lf[t] if t == 0 else b[t - 1] + lf[t])
    log_d = [[b[t] - b[s] + gate[s] for s in range(t + 1)] for t in range(ts)]
    m_t = []
    for t in range(ts):
        mt = b[t] + m0
        for s in range(t + 1):
            mt = jnp.maximum(mt, log_d[t][s])
        m_t.append(mt)
    dmat = [[jnp.exp(log_d[t][s] - m_t[t]) for s in range(t + 1)] for t in range(ts)]
    inter = [jnp.exp(b[t] + m0 - m_t[t]) for t in range(ts)]
    floor = [jnp.exp(-m_t[t]) for t in range(ts)]
    m_new = m_t[ts - 1]
    decay = jnp.exp(b[ts - 1] + m0 - m_new)
    wts = [jnp.exp(b[ts - 1] - b[s] + gate[s] - m_new) for s in range(ts)]
    m1_ref[...] = m_new

    scale = HEAD_DIM ** -0.5
    row_seq = lax.broadcasted_iota(jnp.int32, (ts * nb, HEAD_DIM), 0) % nb
    for h in range(N_HEADS):
        hs = slice(h * HEAD_DIM, (h + 1) * HEAD_DIM)
        q = [cq_ref[t][:, hs] for t in range(ts)]
        k = [ck_ref[t][:, hs] * scale for t in range(ts)]
        vv = [cv_ref[t][:, hs] for t in range(ts)]
        n_prev = n0_ref[:, hs]
        q_blk = jnp.concatenate(q, axis=0).astype(BF16)
        k_blk = jnp.concatenate(k, axis=0).astype(BF16)
        dec_col = _lane_col(decay, h)
        wt_col = [_lane_col(wts[s], h) for s in range(ts)]
        wv_blk = jnp.concatenate([wt_col[s] * vv[s] for s in range(ts)], axis=0)
        qc = jnp.zeros((ts * nb, HEAD_DIM), F32)
        for i in range(nb):
            c_prev = c0_ref[i, h]
            r = lax.dot_general(q_blk, c_prev.astype(BF16), _NT, preferred_element_type=F32)
            qc = jnp.where(row_seq == i, r, qc)
            wv_i = jnp.where(row_seq == i, wv_blk, 0.0).astype(BF16)
            upd = lax.dot_general(wv_i, k_blk, (((0,), (0,)), ((), ())), preferred_element_type=F32)
            c1_ref[i, h] = dec_col[i:i + 1, :] * c_prev + upd
        n_new = dec_col * n_prev
        for s in range(ts):
            n_new = n_new + wt_col[s] * k[s]
        n1_ref[:, hs] = n_new
        for t in range(ts):
            int_col = _lane_col(inter[t], h)
            num = int_col * qc[t * nb:(t + 1) * nb, :]
            den = int_col * jnp.sum(q[t] * n_prev, axis=-1, keepdims=True)
            for s in range(t + 1):
                s_ts = jnp.sum(q[t] * k[s], axis=-1, keepdims=True) * _lane_col(dmat[t][s], h)
                num = num + s_ts * vv[s]
                den = den + s_ts
            hout = num / jnp.maximum(jnp.abs(den), _lane_col(floor[t], h))
            hout = hout * lax.rsqrt(jnp.mean(hout * hout, axis=-1, keepdims=True) + EPS)
            y_ref[t, :, 2 * gw + h * HEAD_DIM:2 * gw + (h + 1) * HEAD_DIM] = (
                _sigmoid(co_ref[t][:, hs]) * hout * mn_ref[:, hs])


def _mix_sample(z3, gates3, lw, pool_prefix, conv_prefix, c_all, n_all, m_all, layer, nb=8):
    ts, n_seq, _ = z3.shape
    gw = GROUP_WIDTH
    zspecs = [pl.BlockSpec((ts, nb, gw), functools.partial(lambda i, j: (0, i, j), j=j)) for j in range(N_SLICES)]
    in_specs = zspecs + [
        pl.BlockSpec((ts, nb, LANES), lambda i: (0, i, 0)),
        _const_spec((ts * ts, gw)),
        _const_spec((ts, gw)),
        _const_spec((N_HEADS, HEAD_DIM, HEAD_DIM)),
        _const_spec((1, gw)),
        _const_spec((1, gw)),
        _const_spec((1, LANES)),
        _const_spec((3, gw)),
        pl.BlockSpec((POOL_STATE, nb, gw), lambda i: (0, i, 0)),
        pl.BlockSpec((2, nb, gw), lambda i: (0, i, 0)),
        pl.BlockSpec((None, nb, N_HEADS, HEAD_DIM, HEAD_DIM), lambda i: (layer, i, 0, 0, 0)),
        pl.BlockSpec((None, nb, gw), lambda i: (layer, i, 0)),
        pl.BlockSpec((None, nb, LANES), lambda i: (layer, i, 0)),
    ]
    out_shape = [
        jax.ShapeDtypeStruct((ts, n_seq, 4 * gw), F32),
        jax.ShapeDtypeStruct((ts, n_seq, gw), F32),
        jax.ShapeDtypeStruct((2, n_seq, gw), F32),
        jax.ShapeDtypeStruct((n_seq, N_HEADS, HEAD_DIM, HEAD_DIM), F32),
        jax.ShapeDtypeStruct((n_seq, gw), F32),
        jax.ShapeDtypeStruct((n_seq, LANES), F32),
    ]
    out_specs = [
        pl.BlockSpec((ts, nb, 4 * gw), lambda i: (0, i, 0)),
        pl.BlockSpec((ts, nb, gw), lambda i: (0, i, 0)),
        pl.BlockSpec((2, nb, gw), lambda i: (0, i, 0)),
        pl.BlockSpec((nb, N_HEADS, HEAD_DIM, HEAD_DIM), lambda i: (i, 0, 0, 0)),
        pl.BlockSpec((nb, gw), lambda i: (i, 0)),
        pl.BlockSpec((nb, LANES), lambda i: (i, 0)),
    ]
    return pl.pallas_call(
        _mix_sample_kernel,
        grid=(n_seq // nb,),
        in_specs=in_specs,
        out_specs=out_specs,
        out_shape=out_shape,
        compiler_params=_params(1),
        name="mix_sample",
    )(*([z3] * N_SLICES), gates3, lw["ws_tab"], lw["bs_tab"], lw["w_pool"], lw["pool_scale"],
      lw["mlstm_norm"], lw["bg_row"], lw["w_dconv"], pool_prefix, conv_prefix, c_all, n_all, m_all)


def _out_q_kernel(x_ref, y_ref, wo_ref, g_ref, wq_ref, x1_ref, q_ref):
    x1 = x_ref[...] + jnp.dot(y_ref[...].astype(BF16), wo_ref[...], preferred_element_type=F32)
    x1_ref[...] = x1
    q_ref[...] = jnp.dot(_rms(x1, g_ref[...]).astype(BF16), wq_ref[...], preferred_element_type=F32)


def _layer_spec(shape, layer):
    nd = len(shape)
    return pl.BlockSpec((None,) + tuple(shape), lambda *_: (layer,) + (0,) * nd, pipeline_mode=pl.Buffered(1))


def _out_q(x, y, w_out_all, g, w_q_all, layer, tm=512):
    m, d = x.shape
    tm = min(tm, m)
    nq = w_q_all.shape[2]
    return pl.pallas_call(
        _out_q_kernel,
        grid=(m // tm,),
        in_specs=[
            pl.BlockSpec((tm, d), lambda i: (i, 0)),
            pl.BlockSpec((tm, y.shape[1]), lambda i: (i, 0)),
            _layer_spec(w_out_all.shape[1:], layer),
            _const_spec((1, d)),
            _layer_spec(w_q_all.shape[1:], layer),
        ],
        out_specs=[pl.BlockSpec((tm, d), lambda i: (i, 0)), pl.BlockSpec((tm, nq), lambda i: (i, 0))],
        out_shape=[jax.ShapeDtypeStruct((m, d), F32), jax.ShapeDtypeStruct((m, nq), F32)],
        compiler_params=_params(1),
        name="out_q",
    )(x, y, w_out_all, g, w_q_all)


def _softmax_rows(s):
    e = jnp.exp(s - jnp.max(s, axis=-1, keepdims=True))
    return e / jnp.sum(e, axis=-1, keepdims=True)


def _attn_prompt_kernel(q_ref, k_ref, v_ref, o_ref):
    scale = HEAD_DIM ** -0.5
    for h in range(N_HEADS):
        hs = slice(h * HEAD_DIM, (h + 1) * HEAD_DIM)
        q = q_ref[:, hs].astype(BF16)
        s = lax.dot_general(q, k_ref[:, hs].astype(BF16), _NT, preferred_element_type=F32) * scale
        p = _softmax_rows(s).astype(BF16)
        o_ref[:, hs] = jnp.dot(p, v_ref[:, hs].astype(BF16), preferred_element_type=F32)


def _attn_prompt(q, k, v, n_seq, seq, tq=512):
    tps = seq // tq
    w = q.shape[1]
    return pl.pallas_call(
        _attn_prompt_kernel,
        grid=(n_seq, tps),
        in_specs=[
            pl.BlockSpec((tq, w), lambda n, t: (n * tps + t, 0)),
            pl.BlockSpec((None, N_MEM, w), lambda n, t: (n, 0, 0)),
            pl.BlockSpec((None, N_MEM, w), lambda n, t: (n, 0, 0)),
        ],
        out_specs=pl.BlockSpec((tq, w), lambda n, t: (n * tps + t, 0)),
        out_shape=jax.ShapeDtypeStruct(q.shape, F32),
        compiler_params=_params(2),
        name="attn_prompt",
    )(q, k, v)


def _attn_sample_kernel(q_ref, k_ref, v_ref, o_ref):
    nb, rows, _ = q_ref.shape
    n_kv = k_ref.shape[1]
    scale = HEAD_DIM ** -0.5
    row_head = lax.broadcasted_iota(jnp.int32, (rows, n_kv), 0) % N_HEADS
    col_head = lax.broadcasted_iota(jnp.int32, (rows, n_kv), 1) % N_HEADS
    same_head = row_head == col_head
    scores = []
    for i in range(nb):
        s = lax.dot_general(q_ref[i].astype(BF16), k_ref[i].astype(BF16), _NT, preferred_element_type=F32) * scale
        scores.append(jnp.where(same_head, s, -jnp.inf))
    p = _softmax_rows(jnp.concatenate(scores, axis=0)).astype(BF16)
    for i in range(nb):
        o_ref[i] = jnp.dot(p[i * rows:(i + 1) * rows, :], v_ref[i].astype(BF16), preferred_element_type=F32)


def _attn_sample(q, k_all, v_all, layer, nb=8):
    n_seq, rows, hd = q.shape
    n_kv = k_all.shape[2]
    return pl.pallas_call(
        _attn_sample_kernel,
        grid=(n_seq // nb,),
        in_specs=[
            pl.BlockSpec((nb, rows, hd), lambda i: (i, 0, 0)),
            pl.BlockSpec((None, nb, n_kv, hd), lambda i: (layer, i, 0, 0)),
            pl.BlockSpec((None, nb, n_kv, hd), lambda i: (layer, i, 0, 0)),
        ],
        out_specs=pl.BlockSpec((nb, rows, hd), lambda i: (i, 0, 0)),
        out_shape=jax.ShapeDtypeStruct(q.shape, F32),
        compiler_params=_params(1),
        name="attn_sample",
    )(q, k_all, v_all)


def _ffn_kernel(x1_ref, o_ref, wo_ref, g_ref, wg_ref, wu_ref, fg_ref, fu_ref, wd_ref, pg_ref, pu_ref,
                out_ref, tg_ref, tu_ref, hn_sc, eg_sc, eu_sc, cg_sc, cu_sc, *, shift, tiles_per_seq, chunk):
    i = pl.program_id(0)
    j = pl.program_id(1)
    tm = out_ref.shape[0]
    pad = pg_ref.shape[0]

    @pl.when(j == 0)
    def _():
        for r in range(tm // chunk):
            rs = slice(r * chunk, (r + 1) * chunk)
            x2 = x1_ref[rs, :] + jnp.dot(o_ref[rs, :].astype(BF16), wo_ref[...], preferred_element_type=F32)
            out_ref[rs, :] = x2
            hn_sc[rs, :] = _rms(x2, g_ref[...]).astype(BF16)

    first = (i % tiles_per_seq) == 0

    @pl.when(first)
    def _():
        eg_sc[0:pad, :] = pg_ref[...]
        eu_sc[0:pad, :] = pu_ref[...]

    @pl.when(jnp.logical_not(first))
    def _():
        eg_sc[0:pad, :] = cg_sc[j]
        eu_sc[0:pad, :] = cu_sc[j]

    def conv(e_sc, f_ref, r0):
        return (f_ref[0:1, :] * e_sc[pad - 2 * shift + r0:pad - 2 * shift + r0 + chunk, :]
                + f_ref[1:2, :] * e_sc[pad - shift + r0:pad - shift + r0 + chunk, :]
                + f_ref[2:3, :] * e_sc[pad + r0:pad + r0 + chunk, :])

    def up(r0):
        hn = hn_sc[r0:r0 + chunk, :]
        eg_sc[pad + r0:pad + r0 + chunk, :] = jnp.dot(hn, wg_ref[...], preferred_element_type=F32)
        eu_sc[pad + r0:pad + r0 + chunk, :] = jnp.dot(hn, wu_ref[...], preferred_element_type=F32)

    n_chunks = tm // chunk
    up(0)
    for r in range(n_chunks):
        r0 = r * chunk
        if r + 1 < n_chunks:
            up(r0 + chunk)
        gate = conv(eg_sc, fg_ref, r0)
        act = (gate * _sigmoid(gate) * conv(eu_sc, fu_ref, r0)).astype(BF16)
        out_ref[r0:r0 + chunk, :] += jnp.dot(act, wd_ref[...], preferred_element_type=F32)

    tail_g = eg_sc[tm:tm + pad, :]
    tail_u = eu_sc[tm:tm + pad, :]
    cg_sc[j] = tail_g
    cu_sc[j] = tail_u
    tg_ref[...] = tail_g
    tu_ref[...] = tail_u


def _ffn(x1, o, w_o_all, g, w_up_all, f_conv, w_down_all, prefix_all, layer, pre_layer, tm, shift, tiles_per_seq,
         chunk=256):
    m, d = x1.shape
    dfp = w_down_all.shape[1]
    tn = FF_TILE
    nj = dfp // tn
    pad = prefix_all.shape[1]
    n_tiles = m // tm
    return pl.pallas_call(
        functools.partial(_ffn_kernel, shift=shift, tiles_per_seq=tiles_per_seq, chunk=min(chunk, tm)),
        grid=(n_tiles, nj),
        in_specs=[
            pl.BlockSpec((tm, d), lambda i, j: (i, 0), pipeline_mode=pl.Buffered(1)),
            pl.BlockSpec((tm, o.shape[1]), lambda i, j: (i, 0)),
            _layer_spec(w_o_all.shape[1:], layer),
            _const_spec((1, d)),
            pl.BlockSpec((None, d, tn), lambda i, j: (layer, 0, j)),
            pl.BlockSpec((None, d, tn), lambda i, j: (layer, 0, nj + j)),
            pl.BlockSpec((3, tn), lambda i, j: (0, j)),
            pl.BlockSpec((3, tn), lambda i, j: (0, nj + j)),
            pl.BlockSpec((None, tn, d), lambda i, j: (layer, j, 0)),
            pl.BlockSpec((None, pad, tn), lambda i, j: (pre_layer, 0, j)),
            pl.BlockSpec((None, pad, tn), lambda i, j: (pre_layer, 0, nj + j)),
        ],
        out_specs=[
            pl.BlockSpec((tm, d), lambda i, j: (i, 0)),
            pl.BlockSpec((pad, tn), lambda i, j: (i, j)),
            pl.BlockSpec((pad, tn), lambda i, j: (i, j)),
        ],
        out_shape=[
            jax.ShapeDtypeStruct((m, d), F32),
            jax.ShapeDtypeStruct((n_tiles * pad, dfp), F32),
            jax.ShapeDtypeStruct((n_tiles * pad, dfp), F32),
        ],
        scratch_shapes=[
            pltpu.VMEM((tm, d), BF16),
            pltpu.VMEM((pad + tm, tn), F32),
            pltpu.VMEM((pad + tm, tn), F32),
            pltpu.VMEM((nj, pad, tn), F32),
            pltpu.VMEM((nj, pad, tn), F32),
        ],
        compiler_params=_params(2),
        name="ffn",
    )(x1, o, w_o_all, g, w_up_all, w_up_all, f_conv, f_conv, w_down_all, prefix_all, prefix_all)


def _final_norm_kernel(x_ref, g_ref, o_ref):
    o_ref[...] = _rms(x_ref[...], g_ref[...])


def _final_norm(x, g, tm=512):
    m, d = x.shape
    tm = min(tm, m)
    return pl.pallas_call(
        _final_norm_kernel,
        grid=(m // tm,),
        in_specs=[pl.BlockSpec((tm, d), lambda i: (i, 0)), _const_spec((1, d))],
        out_specs=pl.BlockSpec((tm, d), lambda i: (i, 0)),
        out_shape=jax.ShapeDtypeStruct((m, d), F32),
        compiler_params=_params(1),
        name="final_norm",
    )(x, g)


def _cast_up_kernel(w_ref, o_ref):
    d_ff = w_ref.shape[1] // 2
    dfp = o_ref.shape[1] // 2
    rows = w_ref.shape[0]
    for half in range(2):
        o_ref[:, half * dfp:half * dfp + d_ff] = w_ref[:, half * d_ff:(half + 1) * d_ff].astype(BF16)
        o_ref[:, half * dfp + d_ff:(half + 1) * dfp] = jnp.zeros((rows, dfp - d_ff), BF16)


def _cast_up(w_up, dfp, tr=128):
    depth, d, two_dff = w_up.shape
    return pl.pallas_call(
        _cast_up_kernel,
        grid=(depth, d // tr),
        in_specs=[pl.BlockSpec((None, tr, two_dff), lambda l, i: (l, i, 0))],
        out_specs=pl.BlockSpec((None, tr, 2 * dfp), lambda l, i: (l, i, 0)),
        out_shape=jax.ShapeDtypeStruct((depth, d, 2 * dfp), BF16),
        compiler_params=_params(2),
        name="cast_up",
    )(w_up)


def _cast_down_kernel(w_ref, rem_ref, o_ref):
    i = pl.program_id(1)
    last = pl.num_programs(1) - 1
    rem = rem_ref.shape[0]

    @pl.when(i < last)
    def _():
        o_ref[...] = w_ref[...].astype(BF16)

    @pl.when(i == last)
    def _():
        o_ref[0:rem, :] = rem_ref[...].astype(BF16)
        o_ref[rem:, :] = jnp.zeros((o_ref.shape[0] - rem, o_ref.shape[1]), BF16)


def _cast_down(w_down, dfp, tr=256):
    depth, d_ff, d = w_down.shape
    n_full = d_ff // tr
    rem = d_ff - n_full * tr
    assert 0 < rem < tr and d_ff % rem == 0 and dfp == (n_full + 1) * tr
    return pl.pallas_call(
        _cast_down_kernel,
        grid=(depth, n_full + 1),
        in_specs=[
            pl.BlockSpec((None, tr, d), lambda l, i: (l, jnp.minimum(i, n_full - 1), 0)),
            pl.BlockSpec((None, rem, d), lambda l, i: (l, d_ff // rem - 1, 0)),
        ],
        out_specs=pl.BlockSpec((None, tr, d), lambda l, i: (l, i, 0)),
        out_shape=jax.ShapeDtypeStruct((depth, dfp, d), BF16),
        compiler_params=_params(2),
        name="cast_down",
    )(w_down, w_down)


def _pad_cols(a, width):
    return jnp.pad(a, [(0, 0)] * (a.ndim - 1) + [(0, width - a.shape[-1])])


def _layer_params(l, ts, d_ff, dfp, norm_mix, w_s, b_s, w_pool, pool_scale, b_gate, mlstm_norm, w_dconv, norm_cross,
                  norm_ffn, w_fconv):
    ws_l, bs_l, fc = w_s[l], b_s[l], w_fconv[l]
    return {
        "norm_mix": norm_mix[l][None, :],
        "w_s": ws_l,
        "b_s_col": bs_l[:, :, None],
        "ws_tab": jnp.repeat(jnp.transpose(ws_l[:, :ts, :ts], (1, 2, 0)).reshape(ts * ts, N_HEADS), HEAD_DIM, axis=1),
        "bs_tab": jnp.repeat(jnp.transpose(bs_l[:, :ts], (1, 0)), HEAD_DIM, axis=1),
        "w_pool": w_pool[l],
        "pool_scale": pool_scale[l][None, :],
        "mlstm_norm": mlstm_norm[l][None, :],
        "bg_row": _pad_cols(b_gate[l][None, :], LANES),
        "bg_col": b_gate[l][:, None],
        "w_dconv": w_dconv[l],
        "norm_cross": norm_cross[l][None, :],
        "norm_ffn": norm_ffn[l][None, :],
        "f_conv": jnp.concatenate([_pad_cols(fc[:, :d_ff], dfp), _pad_cols(fc[:, d_ff:], dfp)], axis=1),
    }


def kernel(x_prompt, x_sample, mem_prompt, state_pool, state_conv, state_mlstm_c, state_mlstm_n, state_mlstm_m,
           state_ffn_conv, cache_mem_k, cache_mem_v, norm_mix, w_in, w_s, b_s, w_pool, pool_scale, b_gate,
           mlstm_norm, w_dconv, w_out, mem_norm, w_kv, norm_cross, w_q, w_o, norm_ffn, w_up, w_fconv, w_down,
           final_norm):
    depth = w_in.shape[0]
    nbp, seq, d = x_prompt.shape
    nbs, ts, _ = x_sample.shape
    gw = GROUP_WIDTH
    mem_w = w_q.shape[2]
    n_main = N_SLICES * gw
    d_ff = w_down.shape[1]
    dfp = -(-d_ff // FF_TILE) * FF_TILE

    w_in_b = w_in.astype(BF16)
    w_gate_b = _pad_cols(w_in[:, :, n_main:], LANES).astype(BF16)
    w_out_b = w_out.astype(BF16)
    w_q_b = w_q.astype(BF16)
    w_o_b = w_o.astype(BF16)
    w_kv_b = w_kv.astype(BF16)
    w_up_b = _cast_up(w_up, dfp)
    w_down_b = _cast_down(w_down, dfp)

    xp = x_prompt.reshape(nbp * seq, d)
    xs = jnp.transpose(x_sample, (1, 0, 2)).reshape(ts * nbs, d)
    mem2 = mem_prompt.reshape(nbp * N_MEM, d)
    kc_all = cache_mem_k.reshape(depth, nbs, N_MEM * N_HEADS, HEAD_DIM)
    vc_all = cache_mem_v.reshape(depth, nbs, N_MEM * N_HEADS, HEAD_DIM)
    n_all = state_mlstm_n.reshape(depth, nbs, gw)
    m_all = _pad_cols(state_mlstm_m, LANES)
    pool_pre_all = jnp.transpose(state_pool, (0, 2, 1, 3))
    conv_pre_all = jnp.transpose(state_conv, (0, 2, 1, 3))
    ffn_pre_all = _pad_cols(jnp.transpose(state_ffn_conv.reshape(depth, nbs, 2, 2, d_ff), (0, 2, 1, 3, 4)),
                            dfp).reshape(depth, 2 * nbs, 2 * dfp)
    pad_p = SUBLANES
    zeros_pre = jnp.zeros((1, pad_p, 2 * dfp), F32)

    tm_p = min(1024, seq)
    tm_s = ts * nbs
    outs = {k: [] for k in ("p_pool", "p_conv", "p_c", "p_n", "p_m", "p_tg", "p_tu", "p_kv",
                            "s_v", "s_bx", "s_conv", "s_c", "s_n", "s_m", "s_tg", "s_tu")}
    for l in range(depth):
        lw = _layer_params(l, ts, d_ff, dfp, norm_mix, w_s, b_s, w_pool, pool_scale, b_gate, mlstm_norm, w_dconv,
                           norm_cross, norm_ffn, w_fconv)

        kv = _norm_mm(mem2, mem_norm[l][None, :], w_kv_b, l, tm=mem2.shape[0], tn=512)
        outs["p_kv"].append(kv)
        mk = kv[:, :mem_w].reshape(nbp, N_MEM, mem_w)
        mv = kv[:, mem_w:].reshape(nbp, N_MEM, mem_w)

        z, gates = _proj_in(xp, lw["norm_mix"], w_in_b, w_gate_b, l, n_main, tm=tm_p)
        gates_t = jnp.transpose(gates[:, :SUBLANES])
        y, conv_st, c1, n1, m1 = _mix_prompt(z, gates, gates_t, lw, nbp, seq)
        outs["p_pool"].append(z.reshape(nbp, seq, n_main)[:, seq - POOL_STATE:, 2 * gw:3 * gw])
        outs["p_conv"].append(conv_st)
        outs["p_c"].append(c1)
        outs["p_n"].append(n1[:, :N_HEADS, :])
        outs["p_m"].append(m1[:, :N_HEADS, 0])
        x1, q = _out_q(xp, y, w_out_b, lw["norm_cross"], w_q_b, l)
        o = _attn_prompt(q, mk, mv, nbp, seq)
        xp, tail_g, tail_u = _ffn(x1, o, w_o_b, lw["norm_ffn"], w_up_b, lw["f_conv"], w_down_b, zeros_pre, l, 0,
                                  tm=tm_p, shift=1, tiles_per_seq=seq // tm_p)
        outs["p_tg"].append(tail_g)
        outs["p_tu"].append(tail_u)

        z, gates = _proj_in(xs, lw["norm_mix"], w_in_b, w_gate_b, l, n_main, tm=tm_s)
        z3 = z.reshape(ts, nbs, n_main)
        y3, v_rows, conv_st, c1, n1, m1 = _mix_sample(
            z3, gates.reshape(ts, nbs, LANES), lw, pool_pre_all[l], conv_pre_all[l], state_mlstm_c, n_all, m_all, l)
        outs["s_v"].append(v_rows)
        outs["s_bx"].append(z3[:, :, 2 * gw:3 * gw])
        outs["s_conv"].append(conv_st)
        outs["s_c"].append(c1)
        outs["s_n"].append(n1)
        outs["s_m"].append(m1)
        x1, q = _out_q(xs, y3.reshape(ts * nbs, 4 * gw), w_out_b, lw["norm_cross"], w_q_b, l)
        q_seq = jnp.transpose(q.reshape(ts, nbs, N_HEADS, HEAD_DIM), (1, 0, 2, 3)).reshape(nbs, ts * N_HEADS, HEAD_DIM)
        o_seq = _attn_sample(q_seq, kc_all, vc_all, l)
        o = jnp.transpose(o_seq.reshape(nbs, ts, N_HEADS, HEAD_DIM), (1, 0, 2, 3)).reshape(ts * nbs, mem_w)
        xs, tail_g, tail_u = _ffn(x1, o, w_o_b, lw["norm_ffn"], w_up_b, lw["f_conv"], w_down_b, ffn_pre_all, l, l,
                                  tm=tm_s, shift=nbs, tiles_per_seq=1)
        outs["s_tg"].append(tail_g)
        outs["s_tu"].append(tail_u)

    y_prompt = _final_norm(xp, final_norm[None, :]).reshape(nbp, seq, d)
    y_sample = jnp.transpose(_final_norm(xs, final_norm[None, :]).reshape(ts, nbs, d), (1, 0, 2))

    st = {k: jnp.stack(v) for k, v in outs.items()}
    tps = seq // tm_p

    def prompt_tail(t):
        return t.reshape(depth, nbp, tps, pad_p, dfp)[:, :, -1, pad_p - 2:, :d_ff]

    p_ffn = jnp.concatenate([prompt_tail(st["p_tg"]), prompt_tail(st["p_tu"])], axis=-1)
    p_kv = st["p_kv"].reshape(depth, nbp, N_MEM, 2, N_HEADS, HEAD_DIM)

    def sample_tail(t):
        return jnp.transpose(t.reshape(depth, 2, nbs, dfp)[..., :d_ff], (0, 2, 1, 3))

    s_ffn = jnp.concatenate([sample_tail(st["s_tg"]), sample_tail(st["s_tu"])], axis=-1)
    s_pool = jnp.concatenate([state_pool, jnp.transpose(st["s_bx"], (0, 2, 1, 3))], axis=2)[:, :, -POOL_STATE:]
    return (y_prompt, y_sample,
            st["p_pool"], st["p_conv"], st["p_c"], st["p_n"], st["p_m"], p_ffn, p_kv[:, :, :, 0], p_kv[:, :, :, 1],
            jnp.transpose(st["s_v"], (0, 2, 1, 3)), s_pool, jnp.transpose(st["s_conv"], (0, 2, 1, 3)),
            st["s_c"], st["s_n"].reshape(depth, nbs, N_HEADS, HEAD_DIM), st["s_m"][:, :, :N_HEADS], s_ffn)
```

```python
import functools

import jax
import jax.numpy as jnp
from jax import lax
from jax.experimental import pallas as pl
from jax.experimental.pallas import tpu as pltpu

F32 = jnp.float32
BF16 = jnp.bfloat16
EPS = 1e-6

LANES = 128
SUBLANES = 8
GROUP_WIDTH = 512
HEAD_DIM = 128
N_HEADS = GROUP_WIDTH // HEAD_DIM
N_SLICES = 10
POOL_WINDOWS = (2, 4, 8, 16)
POOL_STATE = max(POOL_WINDOWS) - 1
N_MEM = 256
FF_TILE = 512
VMEM_LIMIT = 60 * 1024 * 1024
HIGHEST = lax.Precision.HIGHEST

_NT = (((1,), (1,)), ((), ()))


def _params(n_axes, vmem=VMEM_LIMIT):
    return pltpu.CompilerParams(dimension_semantics=("arbitrary",) * n_axes, vmem_limit_bytes=vmem)


def _rms(x, g):
    return x * lax.rsqrt(jnp.mean(x * x, axis=-1, keepdims=True) + EPS) * g


def _gelu(x):
    return 0.5 * x * (1.0 + jnp.tanh(0.7978845608028654 * (x + 0.044715 * (x * x * x))))


def _sigmoid(x):
    return 1.0 / (1.0 + jnp.exp(-x))


def _log_sigmoid(x):
    return jnp.minimum(x, 0.0) - jnp.log(1.0 + jnp.exp(-jnp.abs(x)))


def _lane_col(x, lane):
    idx = lax.broadcasted_iota(jnp.int32, x.shape, 1)
    return jnp.sum(jnp.where(idx == lane, x, 0.0), axis=-1, keepdims=True)


def _const_spec(shape):
    nd = len(shape)
    return pl.BlockSpec(shape, lambda *_: (0,) * nd, pipeline_mode=pl.Buffered(1))


def _proj_in_kernel(x_ref, g_ref, w_ref, wg_ref, z_ref, gate_ref, hn_sc):
    @pl.when(pl.program_id(1) == 0)
    def _():
        hn_sc[...] = _rms(x_ref[...], g_ref[...]).astype(BF16)
        gate_ref[...] = jnp.dot(hn_sc[...], wg_ref[...], preferred_element_type=F32)

    z_ref[...] = jnp.dot(hn_sc[...], w_ref[...], preferred_element_type=F32)


def _proj_in(x, g, w_all, wg_all, layer, n, tm, tn=1024):
    m, d = x.shape
    return pl.pallas_call(
        _proj_in_kernel,
        grid=(m // tm, n // tn),
        in_specs=[
            pl.BlockSpec((tm, d), lambda i, j: (i, 0)),
            _const_spec((1, d)),
            pl.BlockSpec((None, d, tn), lambda i, j: (layer, 0, j)),
            pl.BlockSpec((None, d, LANES), lambda i, j: (layer, 0, 0), pipeline_mode=pl.Buffered(1)),
        ],
        out_specs=[
            pl.BlockSpec((tm, tn), lambda i, j: (i, j)),
            pl.BlockSpec((tm, LANES), lambda i, j: (i, 0)),
        ],
        out_shape=[jax.ShapeDtypeStruct((m, n), F32), jax.ShapeDtypeStruct((m, LANES), F32)],
        scratch_shapes=[pltpu.VMEM((tm, d), BF16)],
        compiler_params=_params(2),
        name="proj_in",
    )(x, g, w_all, wg_all)


def _norm_mm_kernel(x_ref, g_ref, w_ref, o_ref, hn_sc):
    @pl.when(pl.program_id(1) == 0)
    def _():
        hn_sc[...] = _rms(x_ref[...], g_ref[...]).astype(BF16)

    o_ref[...] = jnp.dot(hn_sc[...], w_ref[...], preferred_element_type=F32)


def _norm_mm(x, g, w_all, layer, tm, tn):
    m, d = x.shape
    n = w_all.shape[2]
    return pl.pallas_call(
        _norm_mm_kernel,
        grid=(m // tm, n // tn),
        in_specs=[
            pl.BlockSpec((tm, d), lambda i, j: (i, 0)),
            _const_spec((1, d)),
            pl.BlockSpec((None, d, tn), lambda i, j: (layer, 0, j)),
        ],
        out_specs=pl.BlockSpec((tm, tn), lambda i, j: (i, j)),
        out_shape=jax.ShapeDtypeStruct((m, n), F32),
        scratch_shapes=[pltpu.VMEM((tm, d), BF16)],
        compiler_params=_params(2),
        name="norm_mm",
    )(x, g, w_all)


def _mix_prompt_kernel(au_ref, av_ref, bx_ref, cq_ref, ck_ref, cv_ref, co_ref, db_ref, dc_ref, dx_ref,
                       gc_ref, gr_ref, ws_ref, bs_ref, wp_ref, ps_ref, mn_ref, bgr_ref, bgc_ref, wd_ref,
                       y_ref, cs_ref, c_ref, n_ref, m_ref,
                       c_sc, n_sc, m_sc, xx_sc, xd_sc):
    t = pl.program_id(1)
    rows = y_ref.shape[0]
    gw = GROUP_WIDTH
    pool_pad = 2 * SUBLANES
    conv_pad = SUBLANES

    @pl.when(t == 0)
    def _():
        c_sc[...] = jnp.zeros_like(c_sc)
        n_sc[...] = jnp.zeros_like(n_sc)
        m_sc[...] = jnp.zeros_like(m_sc)
        xx_sc[0:pool_pad, :] = jnp.zeros((pool_pad, gw), F32)
        xd_sc[0:conv_pad, :] = jnp.zeros((conv_pad, gw), F32)

    def mixer_a():
        ri = lax.broadcasted_iota(jnp.int32, (HEAD_DIM, HEAD_DIM), 0)
        ci = lax.broadcasted_iota(jnp.int32, (HEAD_DIM, HEAD_DIM), 1)
        for h in range(N_HEADS):
            hs = slice(h * HEAD_DIM, (h + 1) * HEAD_DIM)
            w_tril = jnp.where(ri >= ci, ws_ref[h], 0.0).astype(BF16)
            for c in range(rows // HEAD_DIM):
                rs = slice(c * HEAD_DIM, (c + 1) * HEAD_DIM)
                v = _gelu(av_ref[rs, hs]).astype(BF16)
                mixed = jnp.dot(w_tril, v, preferred_element_type=F32) + bs_ref[h]
                y_ref[rs, hs] = (_gelu(au_ref[rs, hs]) * mixed).astype(y_ref.dtype)

    def mixer_b():
        xx_sc[pool_pad:pool_pad + rows, :] = bx_ref[...]
        avail = (t * rows + 1 + lax.broadcasted_iota(jnp.int32, (rows, 1), 0)).astype(F32)
        for g, win in enumerate(POOL_WINDOWS):
            gs = slice(g * HEAD_DIM, (g + 1) * HEAD_DIM)
            acc = bx_ref[:, gs]
            for k in range(1, win):
                acc = acc + xx_sc[pool_pad - k:pool_pad - k + rows, gs]
            pooled = acc / jnp.minimum(avail, float(win)) - bx_ref[:, gs]
            yb = jnp.dot(pooled.astype(BF16), wp_ref[g].astype(BF16), preferred_element_type=F32) * ps_ref[:, gs]
            y_ref[:, gw + g * HEAD_DIM:gw + (g + 1) * HEAD_DIM] = yb.astype(y_ref.dtype)
        xx_sc[0:pool_pad, :] = xx_sc[rows:rows + pool_pad, :]

    def mixer_d():
        for g in range(gw // LANES):
            gs = slice(g * LANES, (g + 1) * LANES)
            dcx = dc_ref[:, gs] * dx_ref[:, gs]
            xd_sc[conv_pad:conv_pad + rows, gs] = dcx
            conv = (wd_ref[0:1, gs] * xd_sc[conv_pad - 2:conv_pad - 2 + rows, gs]
                    + wd_ref[1:2, gs] * xd_sc[conv_pad - 1:conv_pad - 1 + rows, gs]
                    + wd_ref[2:3, gs] * dcx)
            y_ref[:, 3 * gw + g * LANES:3 * gw + (g + 1) * LANES] = (db_ref[:, gs] * conv).astype(y_ref.dtype)
        cs_ref[...] = xd_sc[conv_pad + rows - 2:conv_pad + rows, :]
        xd_sc[0:conv_pad, :] = xd_sc[rows:rows + conv_pad, :]

    gc = gc_ref[...] + bgr_ref[...]
    gr = gr_ref[...] + bgc_ref[...]
    rr = lax.broadcasted_iota(jnp.int32, (rows, rows), 0)
    cc = lax.broadcasted_iota(jnp.int32, (rows, rows), 1)
    causal = rr >= cc
    b_cols = jnp.dot(causal.astype(F32), _log_sigmoid(gc), precision=HIGHEST, preferred_element_type=F32)
    b_rows = jnp.dot(_log_sigmoid(gr), (rr <= cc).astype(F32), precision=HIGHEST, preferred_element_type=F32)
    scale = HEAD_DIM ** -0.5

    def mixer_c(h):
        hs = slice(h * HEAD_DIM, (h + 1) * HEAD_DIM)
        bc = _lane_col(b_cols, N_HEADS + h)
        ic = _lane_col(gc, h)
        br = b_rows[N_HEADS + h:N_HEADS + h + 1, :]
        ir = gr[h:h + 1, :]
        m_prev = m_sc[h:h + 1, 0:1]
        log_d = jnp.where(causal, bc - br + ir, -jnp.inf)
        log_inter = bc + m_prev
        m_t = jnp.maximum(log_inter, jnp.max(log_d, axis=-1, keepdims=True))
        dmat = jnp.exp(log_d - m_t)
        inter = jnp.exp(log_inter - m_t)
        qf = cq_ref[:, hs]
        kf = ck_ref[:, hs] * scale
        vf = cv_ref[:, hs]
        q = qf.astype(BF16)
        k = kf.astype(BF16)
        s = lax.dot_general(q, k, _NT, preferred_element_type=F32) * dmat
        c_prev = c_sc[h]
        n_prev = n_sc[h:h + 1, :]
        num = (jnp.dot(s.astype(BF16), vf.astype(BF16), preferred_element_type=F32)
               + inter * lax.dot_general(q, c_prev.astype(BF16), _NT, preferred_element_type=F32))
        den = jnp.sum(s, axis=-1, keepdims=True) + inter * jnp.sum(qf * n_prev, axis=-1, keepdims=True)
        hout = num / jnp.maximum(jnp.abs(den), jnp.exp(-m_t))
        hout = hout * lax.rsqrt(jnp.mean(hout * hout, axis=-1, keepdims=True) + EPS)
        yc = _sigmoid(co_ref[:, hs]) * hout * mn_ref[:, hs]
        y_ref[:, 2 * gw + h * HEAD_DIM:2 * gw + (h + 1) * HEAD_DIM] = yc.astype(y_ref.dtype)
        m_new = m_t[rows - 1:rows, :]
        b_last = bc[rows - 1:rows, :]
        decay = jnp.exp(b_last + m_prev - m_new)
        wts = jnp.exp(b_last - bc + ic - m_new)
        wv_t = jnp.transpose(wts * vf).astype(BF16)
        c_sc[h] = decay * c_prev + jnp.dot(wv_t, k, preferred_element_type=F32)
        n_sc[h:h + 1, :] = decay * n_prev + jnp.sum(wts * kf, axis=0, keepdims=True)
        m_sc[h:h + 1, :] = jnp.broadcast_to(m_new, (1, LANES))

    mixer_c(0)
    mixer_a()
    mixer_c(1)
    mixer_b()
    mixer_c(2)
    mixer_d()
    mixer_c(3)

    @pl.when(t == pl.num_programs(1) - 1)
    def _():
        c_ref[...] = c_sc[...]
        n_ref[...] = n_sc[...]
        m_ref[...] = m_sc[...]


def _mix_prompt(z, gates, gates_t, lw, n_seq, seq, rows=256):
    gw = GROUP_WIDTH
    tps = seq // rows
    zspecs = [pl.BlockSpec((rows, gw), functools.partial(lambda n, t, j: (n * tps + t, j), j=j))
              for j in range(N_SLICES)]
    in_specs = zspecs + [
        pl.BlockSpec((rows, LANES), lambda n, t: (n * tps + t, 0)),
        pl.BlockSpec((SUBLANES, rows), lambda n, t: (0, n * tps + t)),
        _const_spec((N_HEADS, HEAD_DIM, HEAD_DIM)),
        _const_spec((N_HEADS, HEAD_DIM, 1)),
        _const_spec((N_HEADS, HEAD_DIM, HEAD_DIM)),
        _const_spec((1, gw)),
        _const_spec((1, gw)),
        _const_spec((1, LANES)),
        _const_spec((SUBLANES, 1)),
        _const_spec((3, gw)),
    ]
    out_shape = [
        jax.ShapeDtypeStruct((n_seq * seq, 4 * gw), BF16),
        jax.ShapeDtypeStruct((n_seq, 2, gw), F32),
        jax.ShapeDtypeStruct((n_seq, N_HEADS, HEAD_DIM, HEAD_DIM), F32),
        jax.ShapeDtypeStruct((n_seq, SUBLANES, HEAD_DIM), F32),
        jax.ShapeDtypeStruct((n_seq, SUBLANES, LANES), F32),
    ]
    out_specs = [
        pl.BlockSpec((rows, 4 * gw), lambda n, t: (n * tps + t, 0)),
        pl.BlockSpec((None, 2, gw), lambda n, t: (n, 0, 0)),
        pl.BlockSpec((None, N_HEADS, HEAD_DIM, HEAD_DIM), lambda n, t: (n, 0, 0, 0)),
        pl.BlockSpec((None, SUBLANES, HEAD_DIM), lambda n, t: (n, 0, 0)),
        pl.BlockSpec((None, SUBLANES, LANES), lambda n, t: (n, 0, 0)),
    ]
    scratch = [
        pltpu.VMEM((N_HEADS, HEAD_DIM, HEAD_DIM), F32),
        pltpu.VMEM((SUBLANES, HEAD_DIM), F32),
        pltpu.VMEM((SUBLANES, LANES), F32),
        pltpu.VMEM((2 * SUBLANES + rows, gw), F32),
        pltpu.VMEM((SUBLANES + rows, gw), F32),
    ]
    return pl.pallas_call(
        _mix_prompt_kernel,
        grid=(n_seq, tps),
        in_specs=in_specs,
        out_specs=out_specs,
        out_shape=out_shape,
        scratch_shapes=scratch,
        compiler_params=_params(2),
        name="mix_prompt",
    )(*([z] * N_SLICES), gates, gates_t, lw["w_s"], lw["b_s_col"], lw["w_pool"], lw["pool_scale"],
      lw["mlstm_norm"], lw["bg_row"], lw["bg_col"], lw["w_dconv"])


def _mix_sample_kernel(au_ref, av_ref, bx_ref, cq_ref, ck_ref, cv_ref, co_ref, db_ref, dc_ref, dx_ref,
                       g_ref, wst_ref, bst_ref, wp_ref, ps_ref, mn_ref, bgr_ref, wd_ref,
                       pp_ref, cp_ref, c0_ref, n0_ref, m0_ref,
                       y_ref, vr_ref, cs_ref, c1_ref, n1_ref, m1_ref):
    ts = au_ref.shape[0]
    nb = au_ref.shape[1]
    gw = GROUP_WIDTH

    v = [_gelu(av_ref[t]) for t in range(ts)]
    for t in range(ts):
        vr_ref[t] = v[t]
        mixed = bst_ref[t:t + 1, :]
        for s in range(t + 1):
            mixed = mixed + wst_ref[t * ts + s:t * ts + s + 1, :] * v[s]
        y_ref[t, :, 0:gw] = _gelu(au_ref[t]) * mixed

    bx = [bx_ref[t] for t in range(ts)]
    hist = [pp_ref[r] for r in range(POOL_STATE)] + bx
    pooled = []
    for t in range(ts):
        parts = []
        for g, win in enumerate(POOL_WINDOWS):
            gs = slice(g * HEAD_DIM, (g + 1) * HEAD_DIM)
            acc = hist[POOL_STATE + t][:, gs]
            for k in range(1, win):
                acc = acc + hist[POOL_STATE + t - k][:, gs]
            parts.append(acc / float(win) - bx[t][:, gs])
        pooled.append(parts)
    for g in range(len(POOL_WINDOWS)):
        gs = slice(g * HEAD_DIM, (g + 1) * HEAD_DIM)
        pg = jnp.concatenate([pooled[t][g] for t in range(ts)], axis=0).astype(BF16)
        yb = jnp.dot(pg, wp_ref[g].astype(BF16), preferred_element_type=F32) * ps_ref[:, gs]
        for t in range(ts):
            y_ref[t, :, gw + g * HEAD_DIM:gw + (g + 1) * HEAD_DIM] = yb[t * nb:(t + 1) * nb, :]

    dcx = [dc_ref[t] * dx_ref[t] for t in range(ts)]
    xx = [cp_ref[0], cp_ref[1]] + dcx
    for t in range(ts):
        conv = wd_ref[0:1, :] * xx[t] + wd_ref[1:2, :] * xx[t + 1] + wd_ref[2:3, :] * xx[t + 2]
        y_ref[t, :, 3 * gw:4 * gw] = db_ref[t] * conv
    cs_ref[0] = xx[ts]
    cs_ref[1] = xx[ts + 1]

    gate = [g_ref[t] + bgr_ref[...] for t in range(ts)]
    lf = [pltpu.roll(_log_sigmoid(gate[t]), LANES - N_HEADS, 1) for t in range(ts)]
    m0 = m0_ref[...]
    b = []
    for t in range(ts):
        b.append(lf[t] if t == 0 else b[t - 1] + lf[t])
    log_d = [[b[t] - b[s] + gate[s] for s in range(t + 1)] for t in range(ts)]
    m_t = []
    for t in range(ts):
        mt = b[t] + m0
        for s in range(t + 1):
            mt = jnp.maximum(mt, log_d[t][s])
        m_t.append(mt)
    dmat = [[jnp.exp(log_d[t][s] - m_t[t]) for s in range(t + 1)] for t in range(ts)]
    inter = [jnp.exp(b[t] + m0 - m_t[t]) for t in range(ts)]
    floor = [jnp.exp(-m_t[t]) for t in range(ts)]
    m_new = m_t[ts - 1]
    decay = jnp.exp(b[ts - 1] + m0 - m_new)
    wts = [jnp.exp(b[ts - 1] - b[s] + gate[s] - m_new) for s in range(ts)]
    m1_ref[...] = m_new

    scale = HEAD_DIM ** -0.5
    row_seq = lax.broadcasted_iota(jnp.int32, (ts * nb, HEAD_DIM), 0) % nb
    for h in range(N_HEADS):
        hs = slice(h * HEAD_DIM, (h + 1) * HEAD_DIM)
        q = [cq_ref[t][:, hs] for t in range(ts)]
        k = [ck_ref[t][:, hs] * scale for t in range(ts)]
        vv = [cv_ref[t][:, hs] for t in range(ts)]
        n_prev = n0_ref[:, hs]
        q_blk = jnp.concatenate(q, axis=0).astype(BF16)
        k_blk = jnp.concatenate(k, axis=0).astype(BF16)
        dec_col = _lane_col(decay, h)
        wt_col = [_lane_col(wts[s], h) for s in range(ts)]
        wv_blk = jnp.concatenate([wt_col[s] * vv[s] for s in range(ts)], axis=0)
        qc = jnp.zeros((ts * nb, HEAD_DIM), F32)
        for i in range(nb):
            c_prev = c0_ref[i, h]
            r = lax.dot_general(q_blk, c_prev.astype(BF16), _NT, preferred_element_type=F32)
            qc = jnp.where(row_seq == i, r, qc)
            wv_i = jnp.where(row_seq == i, wv_blk, 0.0).astype(BF16)
            upd = lax.dot_general(wv_i, k_blk, (((0,), (0,)), ((), ())), preferred_element_type=F32)
            c1_ref[i, h] = dec_col[i:i + 1, :] * c_prev + upd
        n_new = dec_col * n_prev
        for s in range(ts):
            n_new = n_new + wt_col[s] * k[s]
        n1_ref[:, hs] = n_new
        for t in range(ts):
            int_col = _lane_col(inter[t], h)
            num = int_col * qc[t * nb:(t + 1) * nb, :]
            den = int_col * jnp.sum(q[t] * n_prev, axis=-1, keepdims=True)
            for s in range(t + 1):
                s_ts = jnp.sum(q[t] * k[s], axis=-1, keepdims=True) * _lane_col(dmat[t][s], h)
                num = num + s_ts * vv[s]
                den = den + s_ts
            hout = num / jnp.maximum(jnp.abs(den), _lane_col(floor[t], h))
            hout = hout * lax.rsqrt(jnp.mean(hout * hout, axis=-1, keepdims=True) + EPS)
            y_ref[t, :, 2 * gw + h * HEAD_DIM:2 * gw + (h + 1) * HEAD_DIM] = (
                _sigmoid(co_ref[t][:, hs]) * hout * mn_ref[:, hs])


def _mix_sample(z3, gates3, lw, pool_prefix, conv_prefix, c_all, n_all, m_all, layer, nb=8):
    ts, n_seq, _ = z3.shape
    gw = GROUP_WIDTH
    zspecs = [pl.BlockSpec((ts, nb, gw), functools.partial(lambda i, j: (0, i, j), j=j)) for j in range(N_SLICES)]
    in_specs = zspecs + [
        pl.BlockSpec((ts, nb, LANES), lambda i: (0, i, 0)),
        _const_spec((ts * ts, gw)),
        _const_spec((ts, gw)),
        _const_spec((N_HEADS, HEAD_DIM, HEAD_DIM)),
        _const_spec((1, gw)),
        _const_spec((1, gw)),
        _const_spec((1, LANES)),
        _const_spec((3, gw)),
        pl.BlockSpec((POOL_STATE, nb, gw), lambda i: (0, i, 0)),
        pl.BlockSpec((2, nb, gw), lambda i: (0, i, 0)),
        pl.BlockSpec((None, nb, N_HEADS, HEAD_DIM, HEAD_DIM), lambda i: (layer, i, 0, 0, 0)),
        pl.BlockSpec((None, nb, gw), lambda i: (layer, i, 0)),
        pl.BlockSpec((None, nb, LANES), lambda i: (layer, i, 0)),
    ]
    out_shape = [
        jax.ShapeDtypeStruct((ts, n_seq, 4 * gw), F32),
        jax.ShapeDtypeStruct((ts, n_seq, gw), F32),
        jax.ShapeDtypeStruct((2, n_seq, gw), F32),
        jax.ShapeDtypeStruct((n_seq, N_HEADS, HEAD_DIM, HEAD_DIM), F32),
        jax.ShapeDtypeStruct((n_seq, gw), F32),
        jax.ShapeDtypeStruct((n_seq, LANES), F32),
    ]
    out_specs = [
        pl.BlockSpec((ts, nb, 4 * gw), lambda i: (0, i, 0)),
        pl.BlockSpec((ts, nb, gw), lambda i: (0, i, 0)),
        pl.BlockSpec((2, nb, gw), lambda i: (0, i, 0)),
        pl.BlockSpec((nb, N_HEADS, HEAD_DIM, HEAD_DIM), lambda i: (i, 0, 0, 0)),
        pl.BlockSpec((nb, gw), lambda i: (i, 0)),
        pl.BlockSpec((nb, LANES), lambda i: (i, 0)),
    ]
    return pl.pallas_call(
        _mix_sample_kernel,
        grid=(n_seq // nb,),
        in_specs=in_specs,
        out_specs=out_specs,
        out_shape=out_shape,
        compiler_params=_params(1),
        name="mix_sample",
    )(*([z3] * N_SLICES), gates3, lw["ws_tab"], lw["bs_tab"], lw["w_pool"], lw["pool_scale"],
      lw["mlstm_norm"], lw["bg_row"], lw["w_dconv"], pool_prefix, conv_prefix, c_all, n_all, m_all)


def _out_q_kernel(x_ref, y_ref, wo_ref, g_ref, wq_ref, x1_ref, q_ref):
    x1 = x_ref[...] + jnp.dot(y_ref[...].astype(BF16), wo_ref[...], preferred_element_type=F32)
    x1_ref[...] = x1
    q_ref[...] = jnp.dot(_rms(x1, g_ref[...]).astype(BF16), wq_ref[...], preferred_element_type=F32)


def _layer_spec(shape, layer):
    nd = len(shape)
    return pl.BlockSpec((None,) + tuple(shape), lambda *_: (layer,) + (0,) * nd, pipeline_mode=pl.Buffered(1))


def _out_q(x, y, w_out_all, g, w_q_all, layer, tm=512):
    m, d = x.shape
    tm = min(tm, m)
    nq = w_q_all.shape[2]
    return pl.pallas_call(
        _out_q_kernel,
        grid=(m // tm,),
        in_specs=[
            pl.BlockSpec((tm, d), lambda i: (i, 0)),
            pl.BlockSpec((tm, y.shape[1]), lambda i: (i, 0)),
            _layer_spec(w_out_all.shape[1:], layer),
            _const_spec((1, d)),
            _layer_spec(w_q_all.shape[1:], layer),
        ],
        out_specs=[pl.BlockSpec((tm, d), lambda i: (i, 0)), pl.BlockSpec((tm, nq), lambda i: (i, 0))],
        out_shape=[jax.ShapeDtypeStruct((m, d), F32), jax.ShapeDtypeStruct((m, nq), F32)],
        compiler_params=_params(1),
        name="out_q",
    )(x, y, w_out_all, g, w_q_all)


def _softmax_rows(s):
    e = jnp.exp(s - jnp.max(s, axis=-1, keepdims=True))
    return e / jnp.sum(e, axis=-1, keepdims=True)


def _attn_prompt_kernel(q_ref, k_ref, v_ref, o_ref):
    scale = HEAD_DIM ** -0.5
    for h in range(N_HEADS):
        hs = slice(h * HEAD_DIM, (h + 1) * HEAD_DIM)
        q = q_ref[:, hs].astype(BF16)
        s = lax.dot_general(q, k_ref[:, hs].astype(BF16), _NT, preferred_element_type=F32) * scale
        p = _softmax_rows(s).astype(BF16)
        o_ref[:, hs] = jnp.dot(p, v_ref[:, hs].astype(BF16), preferred_element_type=F32)


def _attn_prompt(q, k, v, n_seq, seq, tq=512):
    tps = seq // tq
    w = q.shape[1]
    return pl.pallas_call(
        _attn_prompt_kernel,
        grid=(n_seq, tps),
        in_specs=[
            pl.BlockSpec((tq, w), lambda n, t: (n * tps + t, 0)),
            pl.BlockSpec((None, N_MEM, w), lambda n, t: (n, 0, 0)),
            pl.BlockSpec((None, N_MEM, w), lambda n, t: (n, 0, 0)),
        ],
        out_specs=pl.BlockSpec((tq, w), lambda n, t: (n * tps + t, 0)),
        out_shape=jax.ShapeDtypeStruct(q.shape, F32),
        compiler_params=_params(2),
        name="attn_prompt",
    )(q, k, v)


def _attn_sample_kernel(q_ref, k_ref, v_ref, o_ref):
    nb, rows, _ = q_ref.shape
    n_kv = k_ref.shape[1]
    scale = HEAD_DIM ** -0.5
    row_head = lax.broadcasted_iota(jnp.int32, (rows, n_kv), 0) % N_HEADS
    col_head = lax.broadcasted_iota(jnp.int32, (rows, n_kv), 1) % N_HEADS
    same_head = row_head == col_head
    scores = []
    for i in range(nb):
        s = lax.dot_general(q_ref[i].astype(BF16), k_ref[i].astype(BF16), _NT, preferred_element_type=F32) * scale
        scores.append(jnp.where(same_head, s, -jnp.inf))
    p = _softmax_rows(jnp.concatenate(scores, axis=0)).astype(BF16)
    for i in range(nb):
        o_ref[i] = jnp.dot(p[i * rows:(i + 1) * rows, :], v_ref[i].astype(BF16), preferred_element_type=F32)


def _attn_sample(q, k_all, v_all, layer, nb=8):
    n_seq, rows, hd = q.shape
    n_kv = k_all.shape[2]
    return pl.pallas_call(
        _attn_sample_kernel,
        grid=(n_seq // nb,),
        in_specs=[
            pl.BlockSpec((nb, rows, hd), lambda i: (i, 0, 0)),
            pl.BlockSpec((None, nb, n_kv, hd), lambda i: (layer, i, 0, 0)),
            pl.BlockSpec((None, nb, n_kv, hd), lambda i: (layer, i, 0, 0)),
        ],
        out_specs=pl.BlockSpec((nb, rows, hd), lambda i: (i, 0, 0)),
        out_shape=jax.ShapeDtypeStruct(q.shape, F32),
        compiler_params=_params(1),
        name="attn_sample",
    )(q, k_all, v_all)


def _ffn_kernel(x1_ref, o_ref, wo_ref, g_ref, wg_ref, wu_ref, fg_ref, fu_ref, wd_ref, pg_ref, pu_ref, fn_ref,
                out_ref, tg_ref, tu_ref, hn_sc, eg_sc, eu_sc, cg_sc, cu_sc, *, shift, tiles_per_seq, chunk,
                apply_final_norm):
    i = pl.program_id(0)
    j = pl.program_id(1)
    tm = out_ref.shape[0]
    pad = pg_ref.shape[0]

    @pl.when(j == 0)
    def _():
        for r in range(tm // chunk):
            rs = slice(r * chunk, (r + 1) * chunk)
            x2 = x1_ref[rs, :] + jnp.dot(o_ref[rs, :].astype(BF16), wo_ref[...], preferred_element_type=F32)
            out_ref[rs, :] = x2
            hn_sc[rs, :] = _rms(x2, g_ref[...]).astype(BF16)

    first = (i % tiles_per_seq) == 0

    @pl.when(first)
    def _():
        eg_sc[0:pad, :] = pg_ref[...]
        eu_sc[0:pad, :] = pu_ref[...]

    @pl.when(jnp.logical_not(first))
    def _():
        eg_sc[0:pad, :] = cg_sc[j]
        eu_sc[0:pad, :] = cu_sc[j]

    def conv(e_sc, f_ref, r0):
        return (f_ref[0:1, :] * e_sc[pad - 2 * shift + r0:pad - 2 * shift + r0 + chunk, :]
                + f_ref[1:2, :] * e_sc[pad - shift + r0:pad - shift + r0 + chunk, :]
                + f_ref[2:3, :] * e_sc[pad + r0:pad + r0 + chunk, :])

    def up(r0):
        hn = hn_sc[r0:r0 + chunk, :]
        eg_sc[pad + r0:pad + r0 + chunk, :] = jnp.dot(hn, wg_ref[...], preferred_element_type=F32)
        eu_sc[pad + r0:pad + r0 + chunk, :] = jnp.dot(hn, wu_ref[...], preferred_element_type=F32)

    n_chunks = tm // chunk
    up(0)
    for r in range(n_chunks):
        r0 = r * chunk
        if r + 1 < n_chunks:
            up(r0 + chunk)
        gate = conv(eg_sc, fg_ref, r0)
        act = (gate * _sigmoid(gate) * conv(eu_sc, fu_ref, r0)).astype(BF16)
        out_ref[r0:r0 + chunk, :] += jnp.dot(act, wd_ref[...], preferred_element_type=F32)

    tail_g = eg_sc[tm:tm + pad, :]
    tail_u = eu_sc[tm:tm + pad, :]
    cg_sc[j] = tail_g
    cu_sc[j] = tail_u
    tg_ref[...] = tail_g
    tu_ref[...] = tail_u

    if apply_final_norm:
        @pl.when(j == pl.num_programs(1) - 1)
        def _():
            for r in range(n_chunks):
                rs = slice(r * chunk, (r + 1) * chunk)
                out_ref[rs, :] = _rms(out_ref[rs, :], fn_ref[...])


def _ffn(x1, o, w_o_all, g, w_up_all, f_conv, w_down_all, prefix_all, final_g, layer, pre_layer, tm, shift,
         tiles_per_seq, apply_final_norm, chunk=256):
    m, d = x1.shape
    dfp = w_down_all.shape[1]
    tn = FF_TILE
    nj = dfp // tn
    pad = prefix_all.shape[1]
    n_tiles = m // tm
    return pl.pallas_call(
        functools.partial(_ffn_kernel, shift=shift, tiles_per_seq=tiles_per_seq, chunk=min(chunk, tm),
                          apply_final_norm=apply_final_norm),
        grid=(n_tiles, nj),
        in_specs=[
            pl.BlockSpec((tm, d), lambda i, j: (i, 0), pipeline_mode=pl.Buffered(1)),
            pl.BlockSpec((tm, o.shape[1]), lambda i, j: (i, 0)),
            _layer_spec(w_o_all.shape[1:], layer),
            _const_spec((1, d)),
            pl.BlockSpec((None, d, tn), lambda i, j: (layer, 0, j)),
            pl.BlockSpec((None, d, tn), lambda i, j: (layer, 0, nj + j)),
            pl.BlockSpec((3, tn), lambda i, j: (0, j)),
            pl.BlockSpec((3, tn), lambda i, j: (0, nj + j)),
            pl.BlockSpec((None, tn, d), lambda i, j: (layer, j, 0)),
            pl.BlockSpec((None, pad, tn), lambda i, j: (pre_layer, 0, j)),
            pl.BlockSpec((None, pad, tn), lambda i, j: (pre_layer, 0, nj + j)),
            _const_spec((1, d)),
        ],
        out_specs=[
            pl.BlockSpec((tm, d), lambda i, j: (i, 0)),
            pl.BlockSpec((pad, tn), lambda i, j: (i, j)),
            pl.BlockSpec((pad, tn), lambda i, j: (i, j)),
        ],
        out_shape=[
            jax.ShapeDtypeStruct((m, d), F32),
            jax.ShapeDtypeStruct((n_tiles * pad, dfp), F32),
            jax.ShapeDtypeStruct((n_tiles * pad, dfp), F32),
        ],
        scratch_shapes=[
            pltpu.VMEM((tm, d), BF16),
            pltpu.VMEM((pad + tm, tn), F32),
            pltpu.VMEM((pad + tm, tn), F32),
            pltpu.VMEM((nj, pad, tn), F32),
            pltpu.VMEM((nj, pad, tn), F32),
        ],
        compiler_params=_params(2),
        name="ffn",
    )(x1, o, w_o_all, g, w_up_all, w_up_all, f_conv, f_conv, w_down_all, prefix_all, prefix_all, final_g)


def _cast_up_kernel(w_ref, o_ref):
    d_ff = w_ref.shape[1] // 2
    dfp = o_ref.shape[1] // 2
    rows = w_ref.shape[0]
    for half in range(2):
        o_ref[:, half * dfp:half * dfp + d_ff] = w_ref[:, half * d_ff:(half + 1) * d_ff].astype(BF16)
        o_ref[:, half * dfp + d_ff:(half + 1) * dfp] = jnp.zeros((rows, dfp - d_ff), BF16)


def _cast_up(w_up, dfp, tr=128):
    depth, d, two_dff = w_up.shape
    return pl.pallas_call(
        _cast_up_kernel,
        grid=(depth, d // tr),
        in_specs=[pl.BlockSpec((None, tr, two_dff), lambda l, i: (l, i, 0))],
        out_specs=pl.BlockSpec((None, tr, 2 * dfp), lambda l, i: (l, i, 0)),
        out_shape=jax.ShapeDtypeStruct((depth, d, 2 * dfp), BF16),
        compiler_params=_params(2),
        name="cast_up",
    )(w_up)


def _cast_down_kernel(w_ref, rem_ref, o_ref):
    i = pl.program_id(1)
    last = pl.num_programs(1) - 1
    rem = rem_ref.shape[0]

    @pl.when(i < last)
    def _():
        o_ref[...] = w_ref[...].astype(BF16)

    @pl.when(i == last)
    def _():
        o_ref[0:rem, :] = rem_ref[...].astype(BF16)
        o_ref[rem:, :] = jnp.zeros((o_ref.shape[0] - rem, o_ref.shape[1]), BF16)


def _cast_down(w_down, dfp, tr=256):
    depth, d_ff, d = w_down.shape
    n_full = d_ff // tr
    rem = d_ff - n_full * tr
    assert 0 < rem < tr and d_ff % rem == 0 and dfp == (n_full + 1) * tr
    return pl.pallas_call(
        _cast_down_kernel,
        grid=(depth, n_full + 1),
        in_specs=[
            pl.BlockSpec((None, tr, d), lambda l, i: (l, jnp.minimum(i, n_full - 1), 0)),
            pl.BlockSpec((None, rem, d), lambda l, i: (l, d_ff // rem - 1, 0)),
        ],
        out_specs=pl.BlockSpec((None, tr, d), lambda l, i: (l, i, 0)),
        out_shape=jax.ShapeDtypeStruct((depth, dfp, d), BF16),
        compiler_params=_params(2),
        name="cast_down",
    )(w_down, w_down)


def _pad_cols(a, width):
    return jnp.pad(a, [(0, 0)] * (a.ndim - 1) + [(0, width - a.shape[-1])])


def _layer_params(l, ts, d_ff, dfp, norm_mix, w_s, b_s, w_pool, pool_scale, b_gate, mlstm_norm, w_dconv, norm_cross,
                  norm_ffn, w_fconv):
    ws_l, bs_l, fc = w_s[l], b_s[l], w_fconv[l]
    return {
        "norm_mix": norm_mix[l][None, :],
        "w_s": ws_l,
        "b_s_col": bs_l[:, :, None],
        "ws_tab": jnp.repeat(jnp.transpose(ws_l[:, :ts, :ts], (1, 2, 0)).reshape(ts * ts, N_HEADS), HEAD_DIM, axis=1),
        "bs_tab": jnp.repeat(jnp.transpose(bs_l[:, :ts], (1, 0)), HEAD_DIM, axis=1),
        "w_pool": w_pool[l],
        "pool_scale": pool_scale[l][None, :],
        "mlstm_norm": mlstm_norm[l][None, :],
        "bg_row": _pad_cols(b_gate[l][None, :], LANES),
        "bg_col": b_gate[l][:, None],
        "w_dconv": w_dconv[l],
        "norm_cross": norm_cross[l][None, :],
        "norm_ffn": norm_ffn[l][None, :],
        "f_conv": jnp.concatenate([_pad_cols(fc[:, :d_ff], dfp), _pad_cols(fc[:, d_ff:], dfp)], axis=1),
    }


def kernel(x_prompt, x_sample, mem_prompt, state_pool, state_conv, state_mlstm_c, state_mlstm_n, state_mlstm_m,
           state_ffn_conv, cache_mem_k, cache_mem_v, norm_mix, w_in, w_s, b_s, w_pool, pool_scale, b_gate,
           mlstm_norm, w_dconv, w_out, mem_norm, w_kv, norm_cross, w_q, w_o, norm_ffn, w_up, w_fconv, w_down,
           final_norm):
    depth = w_in.shape[0]
    nbp, seq, d = x_prompt.shape
    nbs, ts, _ = x_sample.shape
    gw = GROUP_WIDTH
    mem_w = w_q.shape[2]
    n_main = N_SLICES * gw
    d_ff = w_down.shape[1]
    dfp = -(-d_ff // FF_TILE) * FF_TILE

    w_in_b = w_in.astype(BF16)
    w_gate_b = _pad_cols(w_in[:, :, n_main:], LANES).astype(BF16)
    w_out_b = w_out.astype(BF16)
    w_q_b = w_q.astype(BF16)
    w_o_b = w_o.astype(BF16)
    w_kv_b = w_kv.astype(BF16)
    w_up_b = _cast_up(w_up, dfp)
    w_down_b = _cast_down(w_down, dfp)

    xp = x_prompt.reshape(nbp * seq, d)
    xs = jnp.transpose(x_sample, (1, 0, 2)).reshape(ts * nbs, d)
    mem2 = mem_prompt.reshape(nbp * N_MEM, d)
    kc_all = cache_mem_k.reshape(depth, nbs, N_MEM * N_HEADS, HEAD_DIM)
    vc_all = cache_mem_v.reshape(depth, nbs, N_MEM * N_HEADS, HEAD_DIM)
    n_all = state_mlstm_n.reshape(depth, nbs, gw)
    m_all = _pad_cols(state_mlstm_m, LANES)
    pool_pre_all = jnp.transpose(state_pool, (0, 2, 1, 3))
    conv_pre_all = jnp.transpose(state_conv, (0, 2, 1, 3))
    ffn_pre_t = jnp.transpose(state_ffn_conv, (0, 2, 1, 3))
    ffn_pre_all = jnp.concatenate([_pad_cols(ffn_pre_t[..., :d_ff], dfp), _pad_cols(ffn_pre_t[..., d_ff:], dfp)],
                                  axis=-1).reshape(depth, 2 * nbs, 2 * dfp)
    pad_p = SUBLANES
    zeros_pre = jnp.zeros((1, pad_p, 2 * dfp), F32)
    final_g = final_norm[None, :]

    tm_p = min(1024, seq)
    tm_s = ts * nbs
    outs = {k: [] for k in ("p_pool", "p_conv", "p_c", "p_n", "p_m", "p_tg", "p_tu", "p_kv",
                            "s_v", "s_bx", "s_conv", "s_c", "s_n", "s_m", "s_tg", "s_tu")}
    for l in range(depth):
        last = l == depth - 1
        lw = _layer_params(l, ts, d_ff, dfp, norm_mix, w_s, b_s, w_pool, pool_scale, b_gate, mlstm_norm, w_dconv,
                           norm_cross, norm_ffn, w_fconv)

        kv = _norm_mm(mem2, mem_norm[l][None, :], w_kv_b, l, tm=mem2.shape[0], tn=512)
        outs["p_kv"].append(kv)
        mk = kv[:, :mem_w].reshape(nbp, N_MEM, mem_w)
        mv = kv[:, mem_w:].reshape(nbp, N_MEM, mem_w)

        z, gates = _proj_in(xp, lw["norm_mix"], w_in_b, w_gate_b, l, n_main, tm=tm_p)
        gates_t = jnp.transpose(gates[:, :SUBLANES])
        y, conv_st, c1, n1, m1 = _mix_prompt(z, gates, gates_t, lw, nbp, seq)
        outs["p_pool"].append(z.reshape(nbp, seq, n_main)[:, seq - POOL_STATE:, 2 * gw:3 * gw])
        outs["p_conv"].append(conv_st)
        outs["p_c"].append(c1)
        outs["p_n"].append(n1[:, :N_HEADS, :])
        outs["p_m"].append(m1[:, :N_HEADS, 0])
        x1, q = _out_q(xp, y, w_out_b, lw["norm_cross"], w_q_b, l)
        o = _attn_prompt(q, mk, mv, nbp, seq)
        xp, tail_g, tail_u = _ffn(x1, o, w_o_b, lw["norm_ffn"], w_up_b, lw["f_conv"], w_down_b, zeros_pre, final_g,
                                  l, 0, tm=tm_p, shift=1, tiles_per_seq=seq // tm_p, apply_final_norm=last)
        outs["p_tg"].append(tail_g)
        outs["p_tu"].append(tail_u)

        z, gates = _proj_in(xs, lw["norm_mix"], w_in_b, w_gate_b, l, n_main, tm=tm_s)
        z3 = z.reshape(ts, nbs, n_main)
        y3, v_rows, conv_st, c1, n1, m1 = _mix_sample(
            z3, gates.reshape(ts, nbs, LANES), lw, pool_pre_all[l], conv_pre_all[l], state_mlstm_c, n_all, m_all, l)
        outs["s_v"].append(v_rows)
        outs["s_bx"].append(z3[:, :, 2 * gw:3 * gw])
        outs["s_conv"].append(conv_st)
        outs["s_c"].append(c1)
        outs["s_n"].append(n1)
        outs["s_m"].append(m1)
        x1, q = _out_q(xs, y3.reshape(ts * nbs, 4 * gw), w_out_b, lw["norm_cross"], w_q_b, l)
        q_seq = jnp.transpose(q.reshape(ts, nbs, N_HEADS, HEAD_DIM), (1, 0, 2, 3)).reshape(nbs, ts * N_HEADS, HEAD_DIM)
        o_seq = _attn_sample(q_seq, kc_all, vc_all, l)
        o = jnp.transpose(o_seq.reshape(nbs, ts, N_HEADS, HEAD_DIM), (1, 0, 2, 3)).reshape(ts * nbs, mem_w)
        xs, tail_g, tail_u = _ffn(x1, o, w_o_b, lw["norm_ffn"], w_up_b, lw["f_conv"], w_down_b, ffn_pre_all, final_g,
                                  l, l, tm=tm_s, shift=nbs, tiles_per_seq=1, apply_final_norm=last)
        outs["s_tg"].append(tail_g)
        outs["s_tu"].append(tail_u)

    y_prompt = xp.reshape(nbp, seq, d)
    y_sample = jnp.transpose(xs.reshape(ts, nbs, d), (1, 0, 2))

    st = {k: jnp.stack(v) for k, v in outs.items()}
    tps = seq // tm_p

    def prompt_tail(t):
        return t.reshape(depth, nbp, tps, pad_p, dfp)[:, :, -1, pad_p - 2:, :d_ff]

    p_ffn = jnp.concatenate([prompt_tail(st["p_tg"]), prompt_tail(st["p_tu"])], axis=-1)
    p_kv = st["p_kv"].reshape(depth, nbp, N_MEM, 2, N_HEADS, HEAD_DIM)

    def sample_tail(t):
        return jnp.transpose(t.reshape(depth, 2, nbs, dfp)[..., :d_ff], (0, 2, 1, 3))

    s_ffn = jnp.concatenate([sample_tail(st["s_tg"]), sample_tail(st["s_tu"])], axis=-1)
    s_pool = jnp.concatenate([state_pool, jnp.transpose(st["s_bx"], (0, 2, 1, 3))], axis=2)[:, :, -POOL_STATE:]
    return (y_prompt, y_sample,
            st["p_pool"], st["p_conv"], st["p_c"], st["p_n"], st["p_m"], p_ffn, p_kv[:, :, :, 0], p_kv[:, :, :, 1],
            jnp.transpose(st["s_v"], (0, 2, 1, 3)), s_pool, jnp.transpose(st["s_conv"], (0, 2, 1, 3)),
            st["s_c"], st["s_n"].reshape(depth, nbs, N_HEADS, HEAD_DIM), st["s_m"][:, :, :N_HEADS], s_ffn)
```

```python
import functools

import jax
import jax.numpy as jnp
from jax import lax
from jax.experimental import pallas as pl
from jax.experimental.pallas import tpu as pltpu

F32 = jnp.float32
BF16 = jnp.bfloat16
EPS = 1e-6

LANES = 128
SUBLANES = 8
GROUP_WIDTH = 512
HEAD_DIM = 128
N_HEADS = GROUP_WIDTH // HEAD_DIM
N_SLICES = 10
POOL_WINDOWS = (2, 4, 8, 16)
POOL_STATE = max(POOL_WINDOWS) - 1
N_MEM = 256
FF_TILE = 512
VMEM_LIMIT = 60 * 1024 * 1024
HIGHEST = lax.Precision.HIGHEST

_NT = (((1,), (1,)), ((), ()))


def _params(n_axes, vmem=VMEM_LIMIT):
    return pltpu.CompilerParams(dimension_semantics=("arbitrary",) * n_axes, vmem_limit_bytes=vmem)


def _rms(x, g):
    return x * lax.rsqrt(jnp.mean(x * x, axis=-1, keepdims=True) + EPS) * g


def _gelu(x):
    return 0.5 * x * (1.0 + jnp.tanh(0.7978845608028654 * (x + 0.044715 * (x * x * x))))


def _sigmoid(x):
    return 1.0 / (1.0 + jnp.exp(-x))


def _log_sigmoid(x):
    return jnp.minimum(x, 0.0) - jnp.log(1.0 + jnp.exp(-jnp.abs(x)))


def _lane_col(x, lane):
    idx = lax.broadcasted_iota(jnp.int32, x.shape, 1)
    return jnp.sum(jnp.where(idx == lane, x, 0.0), axis=-1, keepdims=True)


def _const_spec(shape):
    nd = len(shape)
    return pl.BlockSpec(shape, lambda *_: (0,) * nd, pipeline_mode=pl.Buffered(1))


def _proj_in_kernel(x_ref, g_ref, w_ref, wg_ref, z_ref, gate_ref, hn_sc):
    @pl.when(pl.program_id(1) == 0)
    def _():
        hn_sc[...] = _rms(x_ref[...], g_ref[...]).astype(BF16)
        gate_ref[...] = jnp.dot(hn_sc[...], wg_ref[...], preferred_element_type=F32)

    z_ref[...] = jnp.dot(hn_sc[...], w_ref[...].astype(BF16), preferred_element_type=F32)


def _proj_in(x, g, w_all, wg_all, layer, n, tm, tn=1024):
    m, d = x.shape
    return pl.pallas_call(
        _proj_in_kernel,
        grid=(m // tm, n // tn),
        in_specs=[
            pl.BlockSpec((tm, d), lambda i, j: (i, 0)),
            _const_spec((1, d)),
            pl.BlockSpec((None, d, tn), lambda i, j: (layer, 0, j)),
            pl.BlockSpec((None, d, LANES), lambda i, j: (layer, 0, 0), pipeline_mode=pl.Buffered(1)),
        ],
        out_specs=[
            pl.BlockSpec((tm, tn), lambda i, j: (i, j)),
            pl.BlockSpec((tm, LANES), lambda i, j: (i, 0)),
        ],
        out_shape=[jax.ShapeDtypeStruct((m, n), F32), jax.ShapeDtypeStruct((m, LANES), F32)],
        scratch_shapes=[pltpu.VMEM((tm, d), BF16)],
        compiler_params=_params(2),
        name="proj_in",
    )(x, g, w_all, wg_all)


def _norm_mm_kernel(x_ref, g_ref, w_ref, o_ref, hn_sc):
    @pl.when(pl.program_id(1) == 0)
    def _():
        hn_sc[...] = _rms(x_ref[...], g_ref[...]).astype(BF16)

    o_ref[...] = jnp.dot(hn_sc[...], w_ref[...], preferred_element_type=F32)


def _norm_mm(x, g, w_all, layer, tm, tn):
    m, d = x.shape
    n = w_all.shape[2]
    return pl.pallas_call(
        _norm_mm_kernel,
        grid=(m // tm, n // tn),
        in_specs=[
            pl.BlockSpec((tm, d), lambda i, j: (i, 0)),
            _const_spec((1, d)),
            pl.BlockSpec((None, d, tn), lambda i, j: (layer, 0, j)),
        ],
        out_specs=pl.BlockSpec((tm, tn), lambda i, j: (i, j)),
        out_shape=jax.ShapeDtypeStruct((m, n), F32),
        scratch_shapes=[pltpu.VMEM((tm, d), BF16)],
        compiler_params=_params(2),
        name="norm_mm",
    )(x, g, w_all)


def _mix_out_kernel(au_ref, av_ref, bx_ref, cq_ref, ck_ref, cv_ref, co_ref, db_ref, dc_ref, dx_ref,
                    gc_ref, gr_ref, ws_ref, bs_ref, wp_ref, ps_ref, mn_ref, bgr_ref, bgc_ref, wd_ref,
                    x_ref, wo_ref, nc_ref, wq_ref,
                    x1_ref, q_ref, cs_ref, c_ref, n_ref, m_ref,
                    c_sc, n_sc, m_sc, xx_sc, xd_sc, y_sc, *, tiles_per_seq):
    g = pl.program_id(0)
    n_steps = pl.num_programs(0)
    t = jnp.minimum(g, n_steps - 2) % tiles_per_seq
    rows = x1_ref.shape[0]
    y_ref = y_sc.at[g % 2]
    y_prev = y_sc.at[(g + 1) % 2]

    @pl.when(g == 0)
    def _():
        y_sc[1] = jnp.zeros(y_sc.shape[1:], y_sc.dtype)
    gw = GROUP_WIDTH
    pool_pad = 2 * SUBLANES
    conv_pad = SUBLANES

    @pl.when(t == 0)
    def _():
        c_sc[...] = jnp.zeros_like(c_sc)
        n_sc[...] = jnp.zeros_like(n_sc)
        m_sc[...] = jnp.zeros_like(m_sc)
        xx_sc[0:pool_pad, :] = jnp.zeros((pool_pad, gw), F32)
        xd_sc[0:conv_pad, :] = jnp.zeros((conv_pad, gw), F32)

    col_chunk = x_ref.shape[1] // 4

    def out_proj(c):
        cs = slice(c * col_chunk, (c + 1) * col_chunk)
        x1_ref[:, cs] = x_ref[:, cs] + jnp.dot(y_prev[...], wo_ref[:, cs], preferred_element_type=F32)

    def q_proj():
        hn = _rms(x1_ref[...], nc_ref[...]).astype(BF16)
        q_ref[...] = jnp.dot(hn, wq_ref[...], preferred_element_type=F32)

    def mixer_a():
        ri = lax.broadcasted_iota(jnp.int32, (HEAD_DIM, HEAD_DIM), 0)
        ci = lax.broadcasted_iota(jnp.int32, (HEAD_DIM, HEAD_DIM), 1)
        for h in range(N_HEADS):
            hs = slice(h * HEAD_DIM, (h + 1) * HEAD_DIM)
            w_tril = jnp.where(ri >= ci, ws_ref[h], 0.0).astype(BF16)
            for c in range(rows // HEAD_DIM):
                rs = slice(c * HEAD_DIM, (c + 1) * HEAD_DIM)
                v = _gelu(av_ref[rs, hs]).astype(BF16)
                mixed = jnp.dot(w_tril, v, preferred_element_type=F32) + bs_ref[h]
                y_ref[rs, hs] = (_gelu(au_ref[rs, hs]) * mixed).astype(y_ref.dtype)

    def mixer_b():
        xx_sc[pool_pad:pool_pad + rows, :] = bx_ref[...]
        avail = (t * rows + 1 + lax.broadcasted_iota(jnp.int32, (rows, 1), 0)).astype(F32)
        for g, win in enumerate(POOL_WINDOWS):
            gs = slice(g * HEAD_DIM, (g + 1) * HEAD_DIM)
            acc = bx_ref[:, gs]
            for k in range(1, win):
                acc = acc + xx_sc[pool_pad - k:pool_pad - k + rows, gs]
            pooled = acc / jnp.minimum(avail, float(win)) - bx_ref[:, gs]
            yb = jnp.dot(pooled.astype(BF16), wp_ref[g].astype(BF16), preferred_element_type=F32) * ps_ref[:, gs]
            y_ref[:, gw + g * HEAD_DIM:gw + (g + 1) * HEAD_DIM] = yb.astype(y_ref.dtype)
        xx_sc[0:pool_pad, :] = xx_sc[rows:rows + pool_pad, :]

    def mixer_d():
        for g in range(gw // LANES):
            gs = slice(g * LANES, (g + 1) * LANES)
            dcx = dc_ref[:, gs] * dx_ref[:, gs]
            xd_sc[conv_pad:conv_pad + rows, gs] = dcx
            conv = (wd_ref[0:1, gs] * xd_sc[conv_pad - 2:conv_pad - 2 + rows, gs]
                    + wd_ref[1:2, gs] * xd_sc[conv_pad - 1:conv_pad - 1 + rows, gs]
                    + wd_ref[2:3, gs] * dcx)
            y_ref[:, 3 * gw + g * LANES:3 * gw + (g + 1) * LANES] = (db_ref[:, gs] * conv).astype(y_ref.dtype)
        cs_ref[...] = xd_sc[conv_pad + rows - 2:conv_pad + rows, :]
        xd_sc[0:conv_pad, :] = xd_sc[rows:rows + conv_pad, :]

    gc = gc_ref[...] + bgr_ref[...]
    gr = gr_ref[...] + bgc_ref[...]
    rr = lax.broadcasted_iota(jnp.int32, (rows, rows), 0)
    cc = lax.broadcasted_iota(jnp.int32, (rows, rows), 1)
    causal = rr >= cc
    b_cols = jnp.dot(causal.astype(F32), _log_sigmoid(gc), precision=HIGHEST, preferred_element_type=F32)
    b_rows = jnp.dot(_log_sigmoid(gr), (rr <= cc).astype(F32), precision=HIGHEST, preferred_element_type=F32)
    scale = HEAD_DIM ** -0.5

    def mixer_c(h):
        hs = slice(h * HEAD_DIM, (h + 1) * HEAD_DIM)
        bc = _lane_col(b_cols, N_HEADS + h)
        ic = _lane_col(gc, h)
        br = b_rows[N_HEADS + h:N_HEADS + h + 1, :]
        ir = gr[h:h + 1, :]
        m_prev = m_sc[h:h + 1, 0:1]
        log_d = jnp.where(causal, bc - br + ir, -jnp.inf)
        log_inter = bc + m_prev
        m_t = jnp.maximum(log_inter, jnp.max(log_d, axis=-1, keepdims=True))
        dmat = jnp.exp(log_d - m_t)
        inter = jnp.exp(log_inter - m_t)
        qf = cq_ref[:, hs]
        kf = ck_ref[:, hs] * scale
        vf = cv_ref[:, hs]
        q = qf.astype(BF16)
        k = kf.astype(BF16)
        s = lax.dot_general(q, k, _NT, preferred_element_type=F32) * dmat
        c_prev = c_sc[h]
        n_prev = n_sc[h:h + 1, :]
        num = (jnp.dot(s.astype(BF16), vf.astype(BF16), preferred_element_type=F32)
               + inter * lax.dot_general(q, c_prev.astype(BF16), _NT, preferred_element_type=F32))
        den = jnp.sum(s, axis=-1, keepdims=True) + inter * jnp.sum(qf * n_prev, axis=-1, keepdims=True)
        hout = num / jnp.maximum(jnp.abs(den), jnp.exp(-m_t))
        hout = hout * lax.rsqrt(jnp.mean(hout * hout, axis=-1, keepdims=True) + EPS)
        yc = _sigmoid(co_ref[:, hs]) * hout * mn_ref[:, hs]
        y_ref[:, 2 * gw + h * HEAD_DIM:2 * gw + (h + 1) * HEAD_DIM] = yc.astype(y_ref.dtype)
        m_new = m_t[rows - 1:rows, :]
        b_last = bc[rows - 1:rows, :]
        decay = jnp.exp(b_last + m_prev - m_new)
        wts = jnp.exp(b_last - bc + ic - m_new)
        wv_t = jnp.transpose(wts * vf).astype(BF16)
        c_sc[h] = decay * c_prev + jnp.dot(wv_t, k, preferred_element_type=F32)
        n_sc[h:h + 1, :] = decay * n_prev + jnp.sum(wts * kf, axis=0, keepdims=True)
        m_sc[h:h + 1, :] = jnp.broadcast_to(m_new, (1, LANES))

    mixer_c(0)
    out_proj(0)
    mixer_a()
    out_proj(1)
    mixer_c(1)
    out_proj(2)
    mixer_b()
    out_proj(3)
    mixer_c(2)
    q_proj()
    mixer_d()
    mixer_c(3)

    @pl.when(jnp.logical_and(t == tiles_per_seq - 1, g < n_steps - 1))
    def _():
        c_ref[...] = c_sc[...]
        n_ref[...] = n_sc[...]
        m_ref[...] = m_sc[...]


def _mix_out(z, gates, gates_t, lw, x, w_out_all, w_q_all, layer, n_seq, seq, rows=256):
    gw = GROUP_WIDTH
    d = x.shape[1]
    nq = w_q_all.shape[2]
    tps = seq // rows
    n_tiles = n_seq * tps
    last = n_tiles - 1

    def cur(g):
        return jnp.minimum(g, last)

    def prev(g):
        return jnp.maximum(g - 1, 0)

    zspecs = [pl.BlockSpec((rows, gw), functools.partial(lambda g, j: (cur(g), j), j=j)) for j in range(N_SLICES)]
    in_specs = zspecs + [
        pl.BlockSpec((rows, LANES), lambda g: (cur(g), 0)),
        pl.BlockSpec((SUBLANES, rows), lambda g: (0, cur(g))),
        _const_spec((N_HEADS, HEAD_DIM, HEAD_DIM)),
        _const_spec((N_HEADS, HEAD_DIM, 1)),
        _const_spec((N_HEADS, HEAD_DIM, HEAD_DIM)),
        _const_spec((1, gw)),
        _const_spec((1, gw)),
        _const_spec((1, LANES)),
        _const_spec((SUBLANES, 1)),
        _const_spec((3, gw)),
        pl.BlockSpec((rows, d), lambda g: (prev(g), 0)),
        _layer_spec(w_out_all.shape[1:], layer),
        _const_spec((1, d)),
        _layer_spec(w_q_all.shape[1:], layer),
    ]
    out_shape = [
        jax.ShapeDtypeStruct((n_seq * seq, d), F32),
        jax.ShapeDtypeStruct((n_seq * seq, nq), F32),
        jax.ShapeDtypeStruct((n_seq, 2, gw), F32),
        jax.ShapeDtypeStruct((n_seq, N_HEADS, HEAD_DIM, HEAD_DIM), F32),
        jax.ShapeDtypeStruct((n_seq, SUBLANES, HEAD_DIM), F32),
        jax.ShapeDtypeStruct((n_seq, SUBLANES, LANES), F32),
    ]
    out_specs = [
        pl.BlockSpec((rows, d), lambda g: (prev(g), 0)),
        pl.BlockSpec((rows, nq), lambda g: (prev(g), 0)),
        pl.BlockSpec((None, 2, gw), lambda g: (cur(g) // tps, 0, 0)),
        pl.BlockSpec((None, N_HEADS, HEAD_DIM, HEAD_DIM), lambda g: (cur(g) // tps, 0, 0, 0)),
        pl.BlockSpec((None, SUBLANES, HEAD_DIM), lambda g: (cur(g) // tps, 0, 0)),
        pl.BlockSpec((None, SUBLANES, LANES), lambda g: (cur(g) // tps, 0, 0)),
    ]
    scratch = [
        pltpu.VMEM((N_HEADS, HEAD_DIM, HEAD_DIM), F32),
        pltpu.VMEM((SUBLANES, HEAD_DIM), F32),
        pltpu.VMEM((SUBLANES, LANES), F32),
        pltpu.VMEM((2 * SUBLANES + rows, gw), F32),
        pltpu.VMEM((SUBLANES + rows, gw), F32),
        pltpu.VMEM((2, rows, 4 * gw), BF16),
    ]
    return pl.pallas_call(
        functools.partial(_mix_out_kernel, tiles_per_seq=tps),
        grid=(n_tiles + 1,),
        in_specs=in_specs,
        out_specs=out_specs,
        out_shape=out_shape,
        scratch_shapes=scratch,
        compiler_params=_params(1),
        name="mix_out",
    )(*([z] * N_SLICES), gates, gates_t, lw["w_s"], lw["b_s_col"], lw["w_pool"], lw["pool_scale"],
      lw["mlstm_norm"], lw["bg_row"], lw["bg_col"], lw["w_dconv"], x, w_out_all, lw["norm_cross"], w_q_all)


def _mix_sample_kernel(au_ref, av_ref, bx_ref, cq_ref, ck_ref, cv_ref, co_ref, db_ref, dc_ref, dx_ref,
                       g_ref, wst_ref, bst_ref, wp_ref, ps_ref, mn_ref, bgr_ref, wd_ref,
                       pp_ref, cp_ref, c0_ref, n0_ref, m0_ref,
                       y_ref, vr_ref, cs_ref, c1_ref, n1_ref, m1_ref):
    ts = au_ref.shape[0]
    nb = au_ref.shape[1]
    gw = GROUP_WIDTH

    v = [_gelu(av_ref[t]) for t in range(ts)]
    for t in range(ts):
        vr_ref[t] = v[t]
        mixed = bst_ref[t:t + 1, :]
        for s in range(t + 1):
            mixed = mixed + wst_ref[t * ts + s:t * ts + s + 1, :] * v[s]
        y_ref[t, :, 0:gw] = _gelu(au_ref[t]) * mixed

    bx = [bx_ref[t] for t in range(ts)]
    hist = [pp_ref[r] for r in range(POOL_STATE)] + bx
    pooled = []
    for t in range(ts):
        parts = []
        for g, win in enumerate(POOL_WINDOWS):
            gs = slice(g * HEAD_DIM, (g + 1) * HEAD_DIM)
            acc = hist[POOL_STATE + t][:, gs]
            for k in range(1, win):
                acc = acc + hist[POOL_STATE + t - k][:, gs]
            parts.append(acc / float(win) - bx[t][:, gs])
        pooled.append(parts)
    for g in range(len(POOL_WINDOWS)):
        gs = slice(g * HEAD_DIM, (g + 1) * HEAD_DIM)
        pg = jnp.concatenate([pooled[t][g] for t in range(ts)], axis=0).astype(BF16)
        yb = jnp.dot(pg, wp_ref[g].astype(BF16), preferred_element_type=F32) * ps_ref[:, gs]
        for t in range(ts):
            y_ref[t, :, gw + g * HEAD_DIM:gw + (g + 1) * HEAD_DIM] = yb[t * nb:(t + 1) * nb, :]

    dcx = [dc_ref[t] * dx_ref[t] for t in range(ts)]
    xx = [cp_ref[0], cp_ref[1]] + dcx
    for t in range(ts):
        conv = wd_ref[0:1, :] * xx[t] + wd_ref[1:2, :] * xx[t + 1] + wd_ref[2:3, :] * xx[t + 2]
        y_ref[t, :, 3 * gw:4 * gw] = db_ref[t] * conv
    cs_ref[0] = xx[ts]
    cs_ref[1] = xx[ts + 1]

    gate = [g_ref[t] + bgr_ref[...] for t in range(ts)]
    lf = [pltpu.roll(_log_sigmoid(gate[t]), LANES - N_HEADS, 1) for t in range(ts)]
    m0 = m0_ref[...]
    b = []
    for t in range(ts):
        b.append(lf[t] if t == 0 else b[t - 1] + lf[t])
    log_d = [[b[t] - b[s] + gate[s] for s in range(t + 1)] for t in range(ts)]
    m_t = []
    for t in range(ts):
        mt = b[t] + m0
        for s in range(t + 1):
            mt = jnp.maximum(mt, log_d[t][s])
        m_t.append(mt)
    dmat = [[jnp.exp(log_d[t][s] - m_t[t]) for s in range(t + 1)] for t in range(ts)]
    inter = [jnp.exp(b[t] + m0 - m_t[t]) for t in range(ts)]
    floor = [jnp.exp(-m_t[t]) for t in range(ts)]
    m_new = m_t[ts - 1]
    decay = jnp.exp(b[ts - 1] + m0 - m_new)
    wts = [jnp.exp(b[ts - 1] - b[s] + gate[s] - m_new) for s in range(ts)]
    m1_ref[...] = m_new

    scale = HEAD_DIM ** -0.5
    row_seq = lax.broadcasted_iota(jnp.int32, (ts * nb, HEAD_DIM), 0) % nb
    for h in range(N_HEADS):
        hs = slice(h * HEAD_DIM, (h + 1) * HEAD_DIM)
        q = [cq_ref[t][:, hs] for t in range(ts)]
        k = [ck_ref[t][:, hs] * scale for t in range(ts)]
        vv = [cv_ref[t][:, hs] for t in range(ts)]
        n_prev = n0_ref[:, hs]
        q_blk = jnp.concatenate(q, axis=0).astype(BF16)
        k_blk = jnp.concatenate(k, axis=0).astype(BF16)
        dec_col = _lane_col(decay, h)
        wt_col = [_lane_col(wts[s], h) for s in range(ts)]
        wv_blk = jnp.concatenate([wt_col[s] * vv[s] for s in range(ts)], axis=0)
        qc = jnp.zeros((ts * nb, HEAD_DIM), F32)
        for i in range(nb):
            c_prev = c0_ref[i, h]
            r = lax.dot_general(q_blk, c_prev.astype(BF16), _NT, preferred_element_type=F32)
            qc = jnp.where(row_seq == i, r, qc)
            wv_i = jnp.where(row_seq == i, wv_blk, 0.0).astype(BF16)
            upd = lax.dot_general(wv_i, k_blk, (((0,), (0,)), ((), ())), preferred_element_type=F32)
            c1_ref[i, h] = dec_col[i:i + 1, :] * c_prev + upd
        n_new = dec_col * n_prev
        for s in range(ts):
            n_new = n_new + wt_col[s] * k[s]
        n1_ref[:, hs] = n_new
        for t in range(ts):
            int_col = _lane_col(inter[t], h)
            num = int_col * qc[t * nb:(t + 1) * nb, :]
            den = int_col * jnp.sum(q[t] * n_prev, axis=-1, keepdims=True)
            for s in range(t + 1):
                s_ts = jnp.sum(q[t] * k[s], axis=-1, keepdims=True) * _lane_col(dmat[t][s], h)
                num = num + s_ts * vv[s]
                den = den + s_ts
            hout = num / jnp.maximum(jnp.abs(den), _lane_col(floor[t], h))
            hout = hout * lax.rsqrt(jnp.mean(hout * hout, axis=-1, keepdims=True) + EPS)
            y_ref[t, :, 2 * gw + h * HEAD_DIM:2 * gw + (h + 1) * HEAD_DIM] = (
                _sigmoid(co_ref[t][:, hs]) * hout * mn_ref[:, hs])


def _mix_sample(z3, gates3, lw, pool_prefix, conv_prefix, c_all, n_all, m_all, layer, nb=8):
    ts, n_seq, _ = z3.shape
    gw = GROUP_WIDTH
    zspecs = [pl.BlockSpec((ts, nb, gw), functools.partial(lambda i, j: (0, i, j), j=j)) for j in range(N_SLICES)]
    in_specs = zspecs + [
        pl.BlockSpec((ts, nb, LANES), lambda i: (0, i, 0)),
        _const_spec((ts * ts, gw)),
        _const_spec((ts, gw)),
        _const_spec((N_HEADS, HEAD_DIM, HEAD_DIM)),
        _const_spec((1, gw)),
        _const_spec((1, gw)),
        _const_spec((1, LANES)),
        _const_spec((3, gw)),
        pl.BlockSpec((POOL_STATE, nb, gw), lambda i: (0, i, 0)),
        pl.BlockSpec((2, nb, gw), lambda i: (0, i, 0)),
        pl.BlockSpec((None, nb, N_HEADS, HEAD_DIM, HEAD_DIM), lambda i: (layer, i, 0, 0, 0)),
        pl.BlockSpec((None, nb, gw), lambda i: (layer, i, 0)),
        pl.BlockSpec((None, nb, LANES), lambda i: (layer, i, 0)),
    ]
    out_shape = [
        jax.ShapeDtypeStruct((ts, n_seq, 4 * gw), F32),
        jax.ShapeDtypeStruct((ts, n_seq, gw), F32),
        jax.ShapeDtypeStruct((2, n_seq, gw), F32),
        jax.ShapeDtypeStruct((n_seq, N_HEADS, HEAD_DIM, HEAD_DIM), F32),
        jax.ShapeDtypeStruct((n_seq, gw), F32),
        jax.ShapeDtypeStruct((n_seq, LANES), F32),
    ]
    out_specs = [
        pl.BlockSpec((ts, nb, 4 * gw), lambda i: (0, i, 0)),
        pl.BlockSpec((ts, nb, gw), lambda i: (0, i, 0)),
        pl.BlockSpec((2, nb, gw), lambda i: (0, i, 0)),
        pl.BlockSpec((nb, N_HEADS, HEAD_DIM, HEAD_DIM), lambda i: (i, 0, 0, 0)),
        pl.BlockSpec((nb, gw), lambda i: (i, 0)),
        pl.BlockSpec((nb, LANES), lambda i: (i, 0)),
    ]
    return pl.pallas_call(
        _mix_sample_kernel,
        grid=(n_seq // nb,),
        in_specs=in_specs,
        out_specs=out_specs,
        out_shape=out_shape,
        compiler_params=_params(1),
        name="mix_sample",
    )(*([z3] * N_SLICES), gates3, lw["ws_tab"], lw["bs_tab"], lw["w_pool"], lw["pool_scale"],
      lw["mlstm_norm"], lw["bg_row"], lw["w_dconv"], pool_prefix, conv_prefix, c_all, n_all, m_all)


def _out_q_kernel(x_ref, y_ref, wo_ref, g_ref, wq_ref, x1_ref, q_ref):
    x1 = x_ref[...] + jnp.dot(y_ref[...].astype(BF16), wo_ref[...], preferred_element_type=F32)
    x1_ref[...] = x1
    q_ref[...] = jnp.dot(_rms(x1, g_ref[...]).astype(BF16), wq_ref[...], preferred_element_type=F32)


def _layer_spec(shape, layer):
    nd = len(shape)
    return pl.BlockSpec((None,) + tuple(shape), lambda *_: (layer,) + (0,) * nd, pipeline_mode=pl.Buffered(1))


def _out_q(x, y, w_out_all, g, w_q_all, layer, tm=512):
    m, d = x.shape
    tm = min(tm, m)
    nq = w_q_all.shape[2]
    return pl.pallas_call(
        _out_q_kernel,
        grid=(m // tm,),
        in_specs=[
            pl.BlockSpec((tm, d), lambda i: (i, 0)),
            pl.BlockSpec((tm, y.shape[1]), lambda i: (i, 0)),
            _layer_spec(w_out_all.shape[1:], layer),
            _const_spec((1, d)),
            _layer_spec(w_q_all.shape[1:], layer),
        ],
        out_specs=[pl.BlockSpec((tm, d), lambda i: (i, 0)), pl.BlockSpec((tm, nq), lambda i: (i, 0))],
        out_shape=[jax.ShapeDtypeStruct((m, d), F32), jax.ShapeDtypeStruct((m, nq), F32)],
        compiler_params=_params(1),
        name="out_q",
    )(x, y, w_out_all, g, w_q_all)


def _softmax_rows(s):
    e = jnp.exp(s - jnp.max(s, axis=-1, keepdims=True))
    return e / jnp.sum(e, axis=-1, keepdims=True)


def _attn_prompt_kernel(q_ref, k_ref, v_ref, o_ref):
    scale = HEAD_DIM ** -0.5
    for h in range(N_HEADS):
        hs = slice(h * HEAD_DIM, (h + 1) * HEAD_DIM)
        q = q_ref[:, hs].astype(BF16)
        s = lax.dot_general(q, k_ref[:, hs].astype(BF16), _NT, preferred_element_type=F32) * scale
        p = _softmax_rows(s).astype(BF16)
        o_ref[:, hs] = jnp.dot(p, v_ref[:, hs].astype(BF16), preferred_element_type=F32)


def _attn_prompt(q, k, v, n_seq, seq, tq=512):
    tps = seq // tq
    w = q.shape[1]
    return pl.pallas_call(
        _attn_prompt_kernel,
        grid=(n_seq, tps),
        in_specs=[
            pl.BlockSpec((tq, w), lambda n, t: (n * tps + t, 0)),
            pl.BlockSpec((None, N_MEM, w), lambda n, t: (n, 0, 0)),
            pl.BlockSpec((None, N_MEM, w), lambda n, t: (n, 0, 0)),
        ],
        out_specs=pl.BlockSpec((tq, w), lambda n, t: (n * tps + t, 0)),
        out_shape=jax.ShapeDtypeStruct(q.shape, F32),
        compiler_params=_params(2),
        name="attn_prompt",
    )(q, k, v)


def _attn_sample_kernel(q_ref, k_ref, v_ref, o_ref):
    nb, rows, _ = q_ref.shape
    n_kv = k_ref.shape[1]
    scale = HEAD_DIM ** -0.5
    row_head = lax.broadcasted_iota(jnp.int32, (rows, n_kv), 0) % N_HEADS
    col_head = lax.broadcasted_iota(jnp.int32, (rows, n_kv), 1) % N_HEADS
    same_head = row_head == col_head
    scores = []
    for i in range(nb):
        s = lax.dot_general(q_ref[i].astype(BF16), k_ref[i].astype(BF16), _NT, preferred_element_type=F32) * scale
        scores.append(jnp.where(same_head, s, -jnp.inf))
    p = _softmax_rows(jnp.concatenate(scores, axis=0)).astype(BF16)
    for i in range(nb):
        o_ref[i] = jnp.dot(p[i * rows:(i + 1) * rows, :], v_ref[i].astype(BF16), preferred_element_type=F32)


def _attn_sample(q, k_all, v_all, layer, nb=8):
    n_seq, rows, hd = q.shape
    n_kv = k_all.shape[2]
    return pl.pallas_call(
        _attn_sample_kernel,
        grid=(n_seq // nb,),
        in_specs=[
            pl.BlockSpec((nb, rows, hd), lambda i: (i, 0, 0)),
            pl.BlockSpec((None, nb, n_kv, hd), lambda i: (layer, i, 0, 0)),
            pl.BlockSpec((None, nb, n_kv, hd), lambda i: (layer, i, 0, 0)),
        ],
        out_specs=pl.BlockSpec((nb, rows, hd), lambda i: (i, 0, 0)),
        out_shape=jax.ShapeDtypeStruct(q.shape, F32),
        compiler_params=_params(1),
        name="attn_sample",
    )(q, k_all, v_all)


def _ffn_kernel(x1_ref, o_ref, wo_ref, g_ref, wg_ref, wu_ref, fg_ref, fu_ref, wd_ref, pg_ref, pu_ref, fn_ref,
                out_ref, tg_ref, tu_ref, hn_sc, eg_sc, eu_sc, cg_sc, cu_sc, *, shift, tiles_per_seq, chunk,
                apply_final_norm):
    i = pl.program_id(0)
    j = pl.program_id(1)
    tm = out_ref.shape[0]
    pad = pg_ref.shape[0]

    @pl.when(j == 0)
    def _():
        for r in range(tm // chunk):
            rs = slice(r * chunk, (r + 1) * chunk)
            x2 = x1_ref[rs, :] + jnp.dot(o_ref[rs, :].astype(BF16), wo_ref[...], preferred_element_type=F32)
            out_ref[rs, :] = x2
            hn_sc[rs, :] = _rms(x2, g_ref[...]).astype(BF16)

    first = (i % tiles_per_seq) == 0

    @pl.when(first)
    def _():
        eg_sc[0:pad, :] = pg_ref[...]
        eu_sc[0:pad, :] = pu_ref[...]

    @pl.when(jnp.logical_not(first))
    def _():
        eg_sc[0:pad, :] = cg_sc[j]
        eu_sc[0:pad, :] = cu_sc[j]

    def conv(e_sc, f_ref, r0, r1):
        return (f_ref[0:1, :] * e_sc[pad - 2 * shift + r0:pad - 2 * shift + r1, :]
                + f_ref[1:2, :] * e_sc[pad - shift + r0:pad - shift + r1, :]
                + f_ref[2:3, :] * e_sc[pad + r0:pad + r1, :])

    def up(r0, r1):
        hn = hn_sc[r0:r1, :]
        eg_sc[pad + r0:pad + r1, :] = jnp.dot(hn, wg_ref[...], preferred_element_type=F32)
        eu_sc[pad + r0:pad + r1, :] = jnp.dot(hn, wu_ref[...], preferred_element_type=F32)

    n_chunks = tm // chunk
    spans = [(r * chunk, (r + 1) * chunk) for r in range(n_chunks)]
    up(*spans[0])
    for idx, (r0, r1) in enumerate(spans):
        if idx + 1 < len(spans):
            up(*spans[idx + 1])
        gate = conv(eg_sc, fg_ref, r0, r1)
        act = (gate * _sigmoid(gate) * conv(eu_sc, fu_ref, r0, r1)).astype(BF16)
        out_ref[r0:r1, :] += jnp.dot(act, wd_ref[...], preferred_element_type=F32)

    tail_g = eg_sc[tm:tm + pad, :]
    tail_u = eu_sc[tm:tm + pad, :]
    cg_sc[j] = tail_g
    cu_sc[j] = tail_u
    tg_ref[...] = tail_g
    tu_ref[...] = tail_u

    if apply_final_norm:
        @pl.when(j == pl.num_programs(1) - 1)
        def _():
            for r in range(n_chunks):
                rs = slice(r * chunk, (r + 1) * chunk)
                out_ref[rs, :] = _rms(out_ref[rs, :], fn_ref[...])


def _ffn(x1, o, w_o_all, g, w_up_all, f_conv, w_down_all, prefix_all, final_g, layer, pre_layer, tm, shift,
         tiles_per_seq, apply_final_norm, chunk=256):
    m, d = x1.shape
    dfp = w_down_all.shape[1]
    tn = FF_TILE
    nj = dfp // tn
    pad = prefix_all.shape[1]
    n_tiles = m // tm
    return pl.pallas_call(
        functools.partial(_ffn_kernel, shift=shift, tiles_per_seq=tiles_per_seq, chunk=min(chunk, tm),
                          apply_final_norm=apply_final_norm),
        grid=(n_tiles, nj),
        in_specs=[
            pl.BlockSpec((tm, d), lambda i, j: (i, 0), pipeline_mode=pl.Buffered(1)),
            pl.BlockSpec((tm, o.shape[1]), lambda i, j: (i, 0)),
            _layer_spec(w_o_all.shape[1:], layer),
            _const_spec((1, d)),
            pl.BlockSpec((None, d, tn), lambda i, j: (layer, 0, j)),
            pl.BlockSpec((None, d, tn), lambda i, j: (layer, 0, nj + j)),
            pl.BlockSpec((3, tn), lambda i, j: (0, j)),
            pl.BlockSpec((3, tn), lambda i, j: (0, nj + j)),
            pl.BlockSpec((None, tn, d), lambda i, j: (layer, j, 0)),
            pl.BlockSpec((None, pad, tn), lambda i, j: (pre_layer, 0, j)),
            pl.BlockSpec((None, pad, tn), lambda i, j: (pre_layer, 0, nj + j)),
            _const_spec((1, d)),
        ],
        out_specs=[
            pl.BlockSpec((tm, d), lambda i, j: (i, 0)),
            pl.BlockSpec((pad, tn), lambda i, j: (i, j)),
            pl.BlockSpec((pad, tn), lambda i, j: (i, j)),
        ],
        out_shape=[
            jax.ShapeDtypeStruct((m, d), F32),
            jax.ShapeDtypeStruct((n_tiles * pad, dfp), F32),
            jax.ShapeDtypeStruct((n_tiles * pad, dfp), F32),
        ],
        scratch_shapes=[
            pltpu.VMEM((tm, d), BF16),
            pltpu.VMEM((pad + tm, tn), F32),
            pltpu.VMEM((pad + tm, tn), F32),
            pltpu.VMEM((nj, pad, tn), F32),
            pltpu.VMEM((nj, pad, tn), F32),
        ],
        compiler_params=_params(2),
        name="ffn",
    )(x1, o, w_o_all, g, w_up_all, w_up_all, f_conv, f_conv, w_down_all, prefix_all, prefix_all, final_g)


def _cast_up_kernel(w_ref, o_ref):
    d_ff = w_ref.shape[1] // 2
    dfp = o_ref.shape[1] // 2
    rows = w_ref.shape[0]
    for half in range(2):
        o_ref[:, half * dfp:half * dfp + d_ff] = w_ref[:, half * d_ff:(half + 1) * d_ff].astype(BF16)
        o_ref[:, half * dfp + d_ff:(half + 1) * dfp] = jnp.zeros((rows, dfp - d_ff), BF16)


def _cast_up(w_up, dfp, tr=128):
    depth, d, two_dff = w_up.shape
    return pl.pallas_call(
        _cast_up_kernel,
        grid=(depth, d // tr),
        in_specs=[pl.BlockSpec((None, tr, two_dff), lambda l, i: (l, i, 0))],
        out_specs=pl.BlockSpec((None, tr, 2 * dfp), lambda l, i: (l, i, 0)),
        out_shape=jax.ShapeDtypeStruct((depth, d, 2 * dfp), BF16),
        compiler_params=_params(2),
        name="cast_up",
    )(w_up)


def _cast_kernel(w_ref, o_ref):
    o_ref[...] = w_ref[...].astype(BF16)


def _cast(w, tr=256):
    depth, rows, cols = w.shape
    return pl.pallas_call(
        _cast_kernel,
        grid=(depth, rows // tr),
        in_specs=[pl.BlockSpec((None, tr, cols), lambda l, i: (l, i, 0))],
        out_specs=pl.BlockSpec((None, tr, cols), lambda l, i: (l, i, 0)),
        out_shape=jax.ShapeDtypeStruct(w.shape, BF16),
        compiler_params=_params(2),
        name="cast",
    )(w)


def _cast_down_kernel(w_ref, rem_ref, o_ref):
    i = pl.program_id(1)
    last = pl.num_programs(1) - 1
    rem = rem_ref.shape[0]

    @pl.when(i < last)
    def _():
        o_ref[...] = w_ref[...].astype(BF16)

    @pl.when(i == last)
    def _():
        o_ref[0:rem, :] = rem_ref[...].astype(BF16)
        o_ref[rem:, :] = jnp.zeros((o_ref.shape[0] - rem, o_ref.shape[1]), BF16)


def _cast_down(w_down, dfp, tr=256):
    depth, d_ff, d = w_down.shape
    n_full = d_ff // tr
    rem = d_ff - n_full * tr
    assert 0 < rem < tr and d_ff % rem == 0 and dfp == (n_full + 1) * tr
    return pl.pallas_call(
        _cast_down_kernel,
        grid=(depth, n_full + 1),
        in_specs=[
            pl.BlockSpec((None, tr, d), lambda l, i: (l, jnp.minimum(i, n_full - 1), 0)),
            pl.BlockSpec((None, rem, d), lambda l, i: (l, d_ff // rem - 1, 0)),
        ],
        out_specs=pl.BlockSpec((None, tr, d), lambda l, i: (l, i, 0)),
        out_shape=jax.ShapeDtypeStruct((depth, dfp, d), BF16),
        compiler_params=_params(2),
        name="cast_down",
    )(w_down, w_down)


def _pad_cols(a, width):
    return jnp.pad(a, [(0, 0)] * (a.ndim - 1) + [(0, width - a.shape[-1])])


def _layer_params(l, ts, d_ff, dfp, norm_mix, w_s, b_s, w_pool, pool_scale, b_gate, mlstm_norm, w_dconv, norm_cross,
                  norm_ffn, w_fconv):
    ws_l, bs_l, fc = w_s[l], b_s[l], w_fconv[l]
    return {
        "norm_mix": norm_mix[l][None, :],
        "w_s": ws_l,
        "b_s_col": bs_l[:, :, None],
        "ws_tab": jnp.repeat(jnp.transpose(ws_l[:, :ts, :ts], (1, 2, 0)).reshape(ts * ts, N_HEADS), HEAD_DIM, axis=1),
        "bs_tab": jnp.repeat(jnp.transpose(bs_l[:, :ts], (1, 0)), HEAD_DIM, axis=1),
        "w_pool": w_pool[l],
        "pool_scale": pool_scale[l][None, :],
        "mlstm_norm": mlstm_norm[l][None, :],
        "bg_row": _pad_cols(b_gate[l][None, :], LANES),
        "bg_col": b_gate[l][:, None],
        "w_dconv": w_dconv[l],
        "norm_cross": norm_cross[l][None, :],
        "norm_ffn": norm_ffn[l][None, :],
        "f_conv": jnp.concatenate([_pad_cols(fc[:, :d_ff], dfp), _pad_cols(fc[:, d_ff:], dfp)], axis=1),
    }


def kernel(x_prompt, x_sample, mem_prompt, state_pool, state_conv, state_mlstm_c, state_mlstm_n, state_mlstm_m,
           state_ffn_conv, cache_mem_k, cache_mem_v, norm_mix, w_in, w_s, b_s, w_pool, pool_scale, b_gate,
           mlstm_norm, w_dconv, w_out, mem_norm, w_kv, norm_cross, w_q, w_o, norm_ffn, w_up, w_fconv, w_down,
           final_norm):
    depth = w_in.shape[0]
    nbp, seq, d = x_prompt.shape
    nbs, ts, _ = x_sample.shape
    gw = GROUP_WIDTH
    mem_w = w_q.shape[2]
    n_main = N_SLICES * gw
    d_ff = w_down.shape[1]
    dfp = -(-d_ff // FF_TILE) * FF_TILE

    w_gate_b = _pad_cols(w_in[:, :, n_main:], LANES).astype(BF16)
    w_out_b = _cast(w_out)
    w_q_b = _cast(w_q)
    w_o_b = _cast(w_o)
    w_kv_b = _cast(w_kv)
    w_up_b = _cast_up(w_up, dfp)
    w_down_b = _cast_down(w_down, dfp)

    xp = x_prompt.reshape(nbp * seq, d)
    xs = jnp.transpose(x_sample, (1, 0, 2)).reshape(ts * nbs, d)
    mem2 = mem_prompt.reshape(nbp * N_MEM, d)
    kc_all = cache_mem_k.reshape(depth, nbs, N_MEM * N_HEADS, HEAD_DIM)
    vc_all = cache_mem_v.reshape(depth, nbs, N_MEM * N_HEADS, HEAD_DIM)
    n_all = state_mlstm_n.reshape(depth, nbs, gw)
    m_all = _pad_cols(state_mlstm_m, LANES)
    pool_pre_all = jnp.transpose(state_pool, (0, 2, 1, 3))
    conv_pre_all = jnp.transpose(state_conv, (0, 2, 1, 3))
    ffn_pre_t = jnp.transpose(state_ffn_conv, (0, 2, 1, 3))
    ffn_pre_all = jnp.concatenate([_pad_cols(ffn_pre_t[..., :d_ff], dfp), _pad_cols(ffn_pre_t[..., d_ff:], dfp)],
                                  axis=-1).reshape(depth, 2 * nbs, 2 * dfp)
    pad_p = SUBLANES
    zeros_pre = jnp.zeros((1, pad_p, 2 * dfp), F32)
    final_g = final_norm[None, :]

    tm_p = min(1024, seq)
    tm_s = ts * nbs
    outs = {k: [] for k in ("p_pool", "p_conv", "p_c", "p_n", "p_m", "p_tg", "p_tu", "p_kv",
                            "s_v", "s_bx", "s_conv", "s_c", "s_n", "s_m", "s_tg", "s_tu")}
    for l in range(depth):
        last = l == depth - 1
        lw = _layer_params(l, ts, d_ff, dfp, norm_mix, w_s, b_s, w_pool, pool_scale, b_gate, mlstm_norm, w_dconv,
                           norm_cross, norm_ffn, w_fconv)

        kv = _norm_mm(mem2, mem_norm[l][None, :], w_kv_b, l, tm=mem2.shape[0], tn=512)
        outs["p_kv"].append(kv)
        mk = kv[:, :mem_w].reshape(nbp, N_MEM, mem_w)
        mv = kv[:, mem_w:].reshape(nbp, N_MEM, mem_w)

        z, gates = _proj_in(xp, lw["norm_mix"], w_in, w_gate_b, l, n_main, tm=tm_p)
        gates_t = jnp.transpose(gates[:, :SUBLANES])
        x1, q, conv_st, c1, n1, m1 = _mix_out(z, gates, gates_t, lw, xp, w_out_b, w_q_b, l, nbp, seq)
        outs["p_pool"].append(z.reshape(nbp, seq, n_main)[:, seq - POOL_STATE:, 2 * gw:3 * gw])
        outs["p_conv"].append(conv_st)
        outs["p_c"].append(c1)
        outs["p_n"].append(n1[:, :N_HEADS, :])
        outs["p_m"].append(m1[:, :N_HEADS, 0])
        o = _attn_prompt(q, mk, mv, nbp, seq)
        xp, tail_g, tail_u = _ffn(x1, o, w_o_b, lw["norm_ffn"], w_up_b, lw["f_conv"], w_down_b, zeros_pre, final_g,
                                  l, 0, tm=tm_p, shift=1, tiles_per_seq=seq // tm_p, apply_final_norm=last)
        outs["p_tg"].append(tail_g)
        outs["p_tu"].append(tail_u)

        z, gates = _proj_in(xs, lw["norm_mix"], w_in, w_gate_b, l, n_main, tm=tm_s)
        z3 = z.reshape(ts, nbs, n_main)
        y3, v_rows, conv_st, c1, n1, m1 = _mix_sample(
            z3, gates.reshape(ts, nbs, LANES), lw, pool_pre_all[l], conv_pre_all[l], state_mlstm_c, n_all, m_all, l)
        outs["s_v"].append(v_rows)
        outs["s_bx"].append(z3[:, :, 2 * gw:3 * gw])
        outs["s_conv"].append(conv_st)
        outs["s_c"].append(c1)
        outs["s_n"].append(n1)
        outs["s_m"].append(m1)
        x1, q = _out_q(xs, y3.reshape(ts * nbs, 4 * gw), w_out_b, lw["norm_cross"], w_q_b, l)
        q_seq = jnp.transpose(q.reshape(ts, nbs, N_HEADS, HEAD_DIM), (1, 0, 2, 3)).reshape(nbs, ts * N_HEADS, HEAD_DIM)
        o_seq = _attn_sample(q_seq, kc_all, vc_all, l)
        o = jnp.transpose(o_seq.reshape(nbs, ts, N_HEADS, HEAD_DIM), (1, 0, 2, 3)).reshape(ts * nbs, mem_w)
        xs, tail_g, tail_u = _ffn(x1, o, w_o_b, lw["norm_ffn"], w_up_b, lw["f_conv"], w_down_b, ffn_pre_all, final_g,
                                  l, l, tm=tm_s, shift=nbs, tiles_per_seq=1, apply_final_norm=last)
        outs["s_tg"].append(tail_g)
        outs["s_tu"].append(tail_u)

    y_prompt = xp.reshape(nbp, seq, d)
    y_sample = jnp.transpose(xs.reshape(ts, nbs, d), (1, 0, 2))

    st = {k: jnp.stack(v) for k, v in outs.items()}
    tps = seq // tm_p

    def prompt_tail(t):
        return t.reshape(depth, nbp, tps, pad_p, dfp)[:, :, -1, pad_p - 2:, :d_ff]

    p_ffn = jnp.concatenate([prompt_tail(st["p_tg"]), prompt_tail(st["p_tu"])], axis=-1)
    p_kv = st["p_kv"].reshape(depth, nbp, N_MEM, 2, N_HEADS, HEAD_DIM)

    def sample_tail(t):
        return jnp.transpose(t.reshape(depth, 2, nbs, dfp)[..., :d_ff], (0, 2, 1, 3))

    s_ffn = jnp.concatenate([sample_tail(st["s_tg"]), sample_tail(st["s_tu"])], axis=-1)
    s_pool = jnp.concatenate([state_pool, jnp.transpose(st["s_bx"], (0, 2, 1, 3))], axis=2)[:, :, -POOL_STATE:]
    return (y_prompt, y_sample,
            st["p_pool"], st["p_conv"], st["p_c"], st["p_n"], st["p_m"], p_ffn, p_kv[:, :, :, 0], p_kv[:, :, :, 1],
            jnp.transpose(st["s_v"], (0, 2, 1, 3)), s_pool, jnp.transpose(st["s_conv"], (0, 2, 1, 3)),
            st["s_c"], st["s_n"].reshape(depth, nbs, N_HEADS, HEAD_DIM), st["s_m"][:, :, :N_HEADS], s_ffn)
```

```python
import functools

import jax
import jax.numpy as jnp
from jax import lax
from jax.experimental import pallas as pl
from jax.experimental.pallas import tpu as pltpu

F32 = jnp.float32
BF16 = jnp.bfloat16
EPS = 1e-6

LANES = 128
SUBLANES = 8
GROUP_WIDTH = 512
HEAD_DIM = 128
N_HEADS = GROUP_WIDTH // HEAD_DIM
N_SLICES = 10
POOL_WINDOWS = (2, 4, 8, 16)
POOL_STATE = max(POOL_WINDOWS) - 1
N_MEM = 256
FF_TILE = 512
VMEM_LIMIT = 60 * 1024 * 1024
HIGHEST = lax.Precision.HIGHEST

_NT = (((1,), (1,)), ((), ()))


def _params(n_axes, vmem=VMEM_LIMIT):
    return pltpu.CompilerParams(dimension_semantics=("arbitrary",) * n_axes, vmem_limit_bytes=vmem)


def _rms(x, g):
    return x * lax.rsqrt(jnp.mean(x * x, axis=-1, keepdims=True) + EPS) * g


def _gelu(x):
    return 0.5 * x * (1.0 + jnp.tanh(0.7978845608028654 * (x + 0.044715 * (x * x * x))))


def _sigmoid(x):
    return 1.0 / (1.0 + jnp.exp(-x))


def _log_sigmoid(x):
    return jnp.minimum(x, 0.0) - jnp.log(1.0 + jnp.exp(-jnp.abs(x)))


def _lane_col(x, lane):
    idx = lax.broadcasted_iota(jnp.int32, x.shape, 1)
    return jnp.sum(jnp.where(idx == lane, x, 0.0), axis=-1, keepdims=True)


def _const_spec(shape):
    nd = len(shape)
    return pl.BlockSpec(shape, lambda *_: (0,) * nd, pipeline_mode=pl.Buffered(1))


def _proj_in_kernel(x_ref, g_ref, w_ref, wg_ref, z_ref, gate_ref, hn_sc):
    @pl.when(pl.program_id(1) == 0)
    def _():
        hn_sc[...] = _rms(x_ref[...], g_ref[...]).astype(BF16)
        gate_ref[...] = jnp.dot(hn_sc[...], wg_ref[...], preferred_element_type=F32)

    z_ref[...] = jnp.dot(hn_sc[...], w_ref[...], preferred_element_type=F32)


def _proj_in(x, g, w_all, wg_all, layer, n, tm, tn=1024):
    m, d = x.shape
    return pl.pallas_call(
        _proj_in_kernel,
        grid=(m // tm, n // tn),
        in_specs=[
            pl.BlockSpec((tm, d), lambda i, j: (i, 0)),
            _const_spec((1, d)),
            pl.BlockSpec((None, d, tn), lambda i, j: (layer, 0, j)),
            pl.BlockSpec((None, d, LANES), lambda i, j: (layer, 0, 0), pipeline_mode=pl.Buffered(1)),
        ],
        out_specs=[
            pl.BlockSpec((tm, tn), lambda i, j: (i, j)),
            pl.BlockSpec((tm, LANES), lambda i, j: (i, 0)),
        ],
        out_shape=[jax.ShapeDtypeStruct((m, n), F32), jax.ShapeDtypeStruct((m, LANES), F32)],
        scratch_shapes=[pltpu.VMEM((tm, d), BF16)],
        compiler_params=_params(2),
        name="proj_in",
    )(x, g, w_all, wg_all)


def _norm_mm_kernel(x_ref, g_ref, w_ref, o_ref, hn_sc):
    @pl.when(pl.program_id(1) == 0)
    def _():
        hn_sc[...] = _rms(x_ref[...], g_ref[...]).astype(BF16)

    o_ref[...] = jnp.dot(hn_sc[...], w_ref[...], preferred_element_type=F32)


def _norm_mm(x, g, w_all, layer, tm, tn):
    m, d = x.shape
    n = w_all.shape[2]
    return pl.pallas_call(
        _norm_mm_kernel,
        grid=(m // tm, n // tn),
        in_specs=[
            pl.BlockSpec((tm, d), lambda i, j: (i, 0)),
            _const_spec((1, d)),
            pl.BlockSpec((None, d, tn), lambda i, j: (layer, 0, j)),
        ],
        out_specs=pl.BlockSpec((tm, tn), lambda i, j: (i, j)),
        out_shape=jax.ShapeDtypeStruct((m, n), F32),
        scratch_shapes=[pltpu.VMEM((tm, d), BF16)],
        compiler_params=_params(2),
        name="norm_mm",
    )(x, g, w_all)


def _mix_out_kernel(au_ref, av_ref, bx_ref, cq_ref, ck_ref, cv_ref, co_ref, db_ref, dc_ref, dx_ref,
                    gc_ref, gr_ref, ws_ref, bs_ref, wp_ref, ps_ref, mn_ref, bgr_ref, bgc_ref, wd_ref,
                    x_ref, wo_ref, nc_ref, wq_ref,
                    x1_ref, q_ref, cs_ref, c_ref, n_ref, m_ref,
                    c_sc, n_sc, m_sc, xx_sc, xd_sc, y_sc, *, tiles_per_seq):
    g = pl.program_id(0)
    n_steps = pl.num_programs(0)
    t = jnp.minimum(g, n_steps - 2) % tiles_per_seq
    rows = x1_ref.shape[0]
    y_ref = y_sc.at[g % 2]
    y_prev = y_sc.at[(g + 1) % 2]

    @pl.when(g == 0)
    def _():
        y_sc[1] = jnp.zeros(y_sc.shape[1:], y_sc.dtype)
    gw = GROUP_WIDTH
    pool_pad = 2 * SUBLANES
    conv_pad = SUBLANES

    @pl.when(t == 0)
    def _():
        c_sc[...] = jnp.zeros_like(c_sc)
        n_sc[...] = jnp.zeros_like(n_sc)
        m_sc[...] = jnp.zeros_like(m_sc)
        xx_sc[0:pool_pad, :] = jnp.zeros((pool_pad, gw), F32)
        xd_sc[0:conv_pad, :] = jnp.zeros((conv_pad, gw), F32)

    col_chunk = x_ref.shape[1] // 4

    def out_proj(c):
        cs = slice(c * col_chunk, (c + 1) * col_chunk)
        x1_ref[:, cs] = x_ref[:, cs] + jnp.dot(y_prev[...], wo_ref[:, cs], preferred_element_type=F32)

    def q_proj():
        hn = _rms(x1_ref[...], nc_ref[...]).astype(BF16)
        q_ref[...] = jnp.dot(hn, wq_ref[...], preferred_element_type=F32)

    def mixer_a():
        ri = lax.broadcasted_iota(jnp.int32, (HEAD_DIM, HEAD_DIM), 0)
        ci = lax.broadcasted_iota(jnp.int32, (HEAD_DIM, HEAD_DIM), 1)
        for h in range(N_HEADS):
            hs = slice(h * HEAD_DIM, (h + 1) * HEAD_DIM)
            w_tril = jnp.where(ri >= ci, ws_ref[h], 0.0).astype(BF16)
            for c in range(rows // HEAD_DIM):
                rs = slice(c * HEAD_DIM, (c + 1) * HEAD_DIM)
                v = _gelu(av_ref[rs, hs]).astype(BF16)
                mixed = jnp.dot(w_tril, v, preferred_element_type=F32) + bs_ref[h]
                y_ref[rs, hs] = (_gelu(au_ref[rs, hs]) * mixed).astype(y_ref.dtype)

    def mixer_b():
        xx_sc[pool_pad:pool_pad + rows, :] = bx_ref[...]
        avail = (t * rows + 1 + lax.broadcasted_iota(jnp.int32, (rows, 1), 0)).astype(F32)
        for g, win in enumerate(POOL_WINDOWS):
            gs = slice(g * HEAD_DIM, (g + 1) * HEAD_DIM)
            acc = bx_ref[:, gs]
            for k in range(1, win):
                acc = acc + xx_sc[pool_pad - k:pool_pad - k + rows, gs]
            pooled = acc / jnp.minimum(avail, float(win)) - bx_ref[:, gs]
            yb = jnp.dot(pooled.astype(BF16), wp_ref[g].astype(BF16), preferred_element_type=F32) * ps_ref[:, gs]
            y_ref[:, gw + g * HEAD_DIM:gw + (g + 1) * HEAD_DIM] = yb.astype(y_ref.dtype)
        xx_sc[0:pool_pad, :] = xx_sc[rows:rows + pool_pad, :]

    def mixer_d():
        for g in range(gw // LANES):
            gs = slice(g * LANES, (g + 1) * LANES)
            dcx = dc_ref[:, gs] * dx_ref[:, gs]
            xd_sc[conv_pad:conv_pad + rows, gs] = dcx
            conv = (wd_ref[0:1, gs] * xd_sc[conv_pad - 2:conv_pad - 2 + rows, gs]
                    + wd_ref[1:2, gs] * xd_sc[conv_pad - 1:conv_pad - 1 + rows, gs]
                    + wd_ref[2:3, gs] * dcx)
            y_ref[:, 3 * gw + g * LANES:3 * gw + (g + 1) * LANES] = (db_ref[:, gs] * conv).astype(y_ref.dtype)
        cs_ref[...] = xd_sc[conv_pad + rows - 2:conv_pad + rows, :]
        xd_sc[0:conv_pad, :] = xd_sc[rows:rows + conv_pad, :]

    gc = gc_ref[...] + bgr_ref[...]
    gr = gr_ref[...] + bgc_ref[...]
    rr = lax.broadcasted_iota(jnp.int32, (rows, rows), 0)
    cc = lax.broadcasted_iota(jnp.int32, (rows, rows), 1)
    causal = rr >= cc
    b_cols = jnp.dot(causal.astype(F32), _log_sigmoid(gc), precision=HIGHEST, preferred_element_type=F32)
    b_rows = jnp.dot(_log_sigmoid(gr), (rr <= cc).astype(F32), precision=HIGHEST, preferred_element_type=F32)
    scale = HEAD_DIM ** -0.5

    def mixer_c(h):
        hs = slice(h * HEAD_DIM, (h + 1) * HEAD_DIM)
        bc = _lane_col(b_cols, N_HEADS + h)
        ic = _lane_col(gc, h)
        br = b_rows[N_HEADS + h:N_HEADS + h + 1, :]
        ir = gr[h:h + 1, :]
        m_prev = m_sc[h:h + 1, 0:1]
        log_d = jnp.where(causal, bc - br + ir, -jnp.inf)
        log_inter = bc + m_prev
        m_t = jnp.maximum(log_inter, jnp.max(log_d, axis=-1, keepdims=True))
        dmat = jnp.exp(log_d - m_t)
        inter = jnp.exp(log_inter - m_t)
        qf = cq_ref[:, hs]
        kf = ck_ref[:, hs] * scale
        vf = cv_ref[:, hs]
        q = qf.astype(BF16)
        k = kf.astype(BF16)
        s = lax.dot_general(q, k, _NT, preferred_element_type=F32) * dmat
        c_prev = c_sc[h]
        n_prev = n_sc[h:h + 1, :]
        num = (jnp.dot(s.astype(BF16), vf.astype(BF16), preferred_element_type=F32)
               + inter * lax.dot_general(q, c_prev.astype(BF16), _NT, preferred_element_type=F32))
        den = jnp.sum(s, axis=-1, keepdims=True) + inter * jnp.sum(qf * n_prev, axis=-1, keepdims=True)
        hout = num / jnp.maximum(jnp.abs(den), jnp.exp(-m_t))
        hout = hout * lax.rsqrt(jnp.mean(hout * hout, axis=-1, keepdims=True) + EPS)
        yc = _sigmoid(co_ref[:, hs]) * hout * mn_ref[:, hs]
        y_ref[:, 2 * gw + h * HEAD_DIM:2 * gw + (h + 1) * HEAD_DIM] = yc.astype(y_ref.dtype)
        m_new = m_t[rows - 1:rows, :]
        b_last = bc[rows - 1:rows, :]
        decay = jnp.exp(b_last + m_prev - m_new)
        wts = jnp.exp(b_last - bc + ic - m_new)
        wv_t = jnp.transpose(wts * vf).astype(BF16)
        c_sc[h] = decay * c_prev + jnp.dot(wv_t, k, preferred_element_type=F32)
        n_sc[h:h + 1, :] = decay * n_prev + jnp.sum(wts * kf, axis=0, keepdims=True)
        m_sc[h:h + 1, :] = jnp.broadcast_to(m_new, (1, LANES))

    mixer_c(0)
    out_proj(0)
    mixer_a()
    out_proj(1)
    mixer_c(1)
    out_proj(2)
    mixer_b()
    out_proj(3)
    mixer_c(2)
    q_proj()
    mixer_d()
    mixer_c(3)

    @pl.when(jnp.logical_and(t == tiles_per_seq - 1, g < n_steps - 1))
    def _():
        c_ref[...] = c_sc[...]
        n_ref[...] = n_sc[...]
        m_ref[...] = m_sc[...]


def _mix_out(z, gates, gates_t, lw, x, w_out_all, w_q_all, layer, n_seq, seq, rows=256):
    gw = GROUP_WIDTH
    d = x.shape[1]
    nq = w_q_all.shape[2]
    tps = seq // rows
    n_tiles = n_seq * tps
    last = n_tiles - 1

    def cur(g):
        return jnp.minimum(g, last)

    def prev(g):
        return jnp.maximum(g - 1, 0)

    zspecs = [pl.BlockSpec((rows, gw), functools.partial(lambda g, j: (cur(g), j), j=j)) for j in range(N_SLICES)]
    in_specs = zspecs + [
        pl.BlockSpec((rows, LANES), lambda g: (cur(g), 0)),
        pl.BlockSpec((SUBLANES, rows), lambda g: (0, cur(g))),
        _const_spec((N_HEADS, HEAD_DIM, HEAD_DIM)),
        _const_spec((N_HEADS, HEAD_DIM, 1)),
        _const_spec((N_HEADS, HEAD_DIM, HEAD_DIM)),
        _const_spec((1, gw)),
        _const_spec((1, gw)),
        _const_spec((1, LANES)),
        _const_spec((SUBLANES, 1)),
        _const_spec((3, gw)),
        pl.BlockSpec((rows, d), lambda g: (prev(g), 0)),
        _layer_spec(w_out_all.shape[1:], layer),
        _const_spec((1, d)),
        _layer_spec(w_q_all.shape[1:], layer),
    ]
    out_shape = [
        jax.ShapeDtypeStruct((n_seq * seq, d), F32),
        jax.ShapeDtypeStruct((n_seq * seq, nq), F32),
        jax.ShapeDtypeStruct((n_seq, 2, gw), F32),
        jax.ShapeDtypeStruct((n_seq, N_HEADS, HEAD_DIM, HEAD_DIM), F32),
        jax.ShapeDtypeStruct((n_seq, SUBLANES, HEAD_DIM), F32),
        jax.ShapeDtypeStruct((n_seq, SUBLANES, LANES), F32),
    ]
    out_specs = [
        pl.BlockSpec((rows, d), lambda g: (prev(g), 0)),
        pl.BlockSpec((rows, nq), lambda g: (prev(g), 0)),
        pl.BlockSpec((None, 2, gw), lambda g: (cur(g) // tps, 0, 0)),
        pl.BlockSpec((None, N_HEADS, HEAD_DIM, HEAD_DIM), lambda g: (cur(g) // tps, 0, 0, 0)),
        pl.BlockSpec((None, SUBLANES, HEAD_DIM), lambda g: (cur(g) // tps, 0, 0)),
        pl.BlockSpec((None, SUBLANES, LANES), lambda g: (cur(g) // tps, 0, 0)),
    ]
    scratch = [
        pltpu.VMEM((N_HEADS, HEAD_DIM, HEAD_DIM), F32),
        pltpu.VMEM((SUBLANES, HEAD_DIM), F32),
        pltpu.VMEM((SUBLANES, LANES), F32),
        pltpu.VMEM((2 * SUBLANES + rows, gw), F32),
        pltpu.VMEM((SUBLANES + rows, gw), F32),
        pltpu.VMEM((2, rows, 4 * gw), BF16),
    ]
    return pl.pallas_call(
        functools.partial(_mix_out_kernel, tiles_per_seq=tps),
        grid=(n_tiles + 1,),
        in_specs=in_specs,
        out_specs=out_specs,
        out_shape=out_shape,
        scratch_shapes=scratch,
        compiler_params=_params(1),
        name="mix_out",
    )(*([z] * N_SLICES), gates, gates_t, lw["w_s"], lw["b_s_col"], lw["w_pool"], lw["pool_scale"],
      lw["mlstm_norm"], lw["bg_row"], lw["bg_col"], lw["w_dconv"], x, w_out_all, lw["norm_cross"], w_q_all)


def _mix_sample_kernel(au_ref, av_ref, bx_ref, cq_ref, ck_ref, cv_ref, co_ref, db_ref, dc_ref, dx_ref,
                       g_ref, wst_ref, bst_ref, wp_ref, ps_ref, mn_ref, bgr_ref, wd_ref,
                       pp_ref, cp_ref, c0_ref, n0_ref, m0_ref, c_buf_ref,
                       y_ref, vr_ref, cs_ref, c1_ref, n1_ref, m1_ref):
    ts = au_ref.shape[0]
    nb = au_ref.shape[1]
    gw = GROUP_WIDTH

    v = [_gelu(av_ref[t]) for t in range(ts)]
    for t in range(ts):
        vr_ref[t] = v[t]
        mixed = bst_ref[t:t + 1, :]
        for s in range(t + 1):
            mixed = mixed + wst_ref[t * ts + s:t * ts + s + 1, :] * v[s]
        y_ref[t, :, 0:gw] = _gelu(au_ref[t]) * mixed

    bx = [bx_ref[t] for t in range(ts)]
    hist = [pp_ref[r] for r in range(POOL_STATE)] + bx
    pooled = []
    for t in range(ts):
        parts = []
        for g, win in enumerate(POOL_WINDOWS):
            gs = slice(g * HEAD_DIM, (g + 1) * HEAD_DIM)
            acc = hist[POOL_STATE + t][:, gs]
            for k in range(1, win):
                acc = acc + hist[POOL_STATE + t - k][:, gs]
            parts.append(acc / float(win) - bx[t][:, gs])
        pooled.append(parts)
    for g in range(len(POOL_WINDOWS)):
        gs = slice(g * HEAD_DIM, (g + 1) * HEAD_DIM)
        pg = jnp.concatenate([pooled[t][g] for t in range(ts)], axis=0).astype(BF16)
        yb = jnp.dot(pg, wp_ref[g].astype(BF16), preferred_element_type=F32) * ps_ref[:, gs]
        for t in range(ts):
            y_ref[t, :, gw + g * HEAD_DIM:gw + (g + 1) * HEAD_DIM] = yb[t * nb:(t + 1) * nb, :]

    dcx = [dc_ref[t] * dx_ref[t] for t in range(ts)]
    xx = [cp_ref[0], cp_ref[1]] + dcx
    for t in range(ts):
        conv = wd_ref[0:1, :] * xx[t] + wd_ref[1:2, :] * xx[t + 1] + wd_ref[2:3, :] * xx[t + 2]
        y_ref[t, :, 3 * gw:4 * gw] = db_ref[t] * conv
    cs_ref[0] = xx[ts]
    cs_ref[1] = xx[ts + 1]

    gate = [g_ref[t] + bgr_ref[...] for t in range(ts)]
    lf = [pltpu.roll(_log_sigmoid(gate[t]), LANES - N_HEADS, 1) for t in range(ts)]
    m0 = m0_ref[...]
    b = []
    for t in range(ts):
        b.append(lf[t] if t == 0 else b[t - 1] + lf[t])
    log_d = [[b[t] - b[s] + gate[s] for s in range(t + 1)] for t in range(ts)]
    m_t = []
    for t in range(ts):
        mt = b[t] + m0
        for s in range(t + 1):
            mt = jnp.maximum(mt, log_d[t][s])
        m_t.append(mt)
    dmat = [[jnp.exp(log_d[t][s] - m_t[t]) for s in range(t + 1)] for t in range(ts)]
    inter = [jnp.exp(b[t] + m0 - m_t[t]) for t in range(ts)]
    floor = [jnp.exp(-m_t[t]) for t in range(ts)]
    m_new = m_t[ts - 1]
    decay = jnp.exp(b[ts - 1] + m0 - m_new)
    wts = [jnp.exp(b[ts - 1] - b[s] + gate[s] - m_new) for s in range(ts)]
    m1_ref[...] = m_new

    scale = HEAD_DIM ** -0.5
    row_seq = lax.broadcasted_iota(jnp.int32, (ts * nb, HEAD_DIM), 0) % nb
    for h in range(N_HEADS):
        hs = slice(h * HEAD_DIM, (h + 1) * HEAD_DIM)
        q = [cq_ref[t][:, hs] for t in range(ts)]
        k = [ck_ref[t][:, hs] * scale for t in range(ts)]
        vv = [cv_ref[t][:, hs] for t in range(ts)]
        n_prev = n0_ref[:, hs]
        q_blk = jnp.concatenate(q, axis=0).astype(BF16)
        k_blk = jnp.concatenate(k, axis=0).astype(BF16)
        dec_col = _lane_col(decay, h)
        wt_col = [_lane_col(wts[s], h) for s in range(ts)]
        wv_blk = jnp.concatenate([wt_col[s] * vv[s] for s in range(ts)], axis=0)
        qc = jnp.zeros((ts * nb, HEAD_DIM), F32)
        for i in range(nb):
            c_prev = c0_ref[i, h]
            r = lax.dot_general(q_blk, c_prev.astype(BF16), _NT, preferred_element_type=F32)
            qc = jnp.where(row_seq == i, r, qc)
            wv_i = jnp.where(row_seq == i, wv_blk, 0.0).astype(BF16)
            upd = lax.dot_general(wv_i, k_blk, (((0,), (0,)), ((), ())), preferred_element_type=F32)
            c1_ref[i, h] = dec_col[i:i + 1, :] * c_prev + upd
        n_new = dec_col * n_prev
        for s in range(ts):
            n_new = n_new + wt_col[s] * k[s]
        n1_ref[:, hs] = n_new
        for t in range(ts):
            int_col = _lane_col(inter[t], h)
            num = int_col * qc[t * nb:(t + 1) * nb, :]
            den = int_col * jnp.sum(q[t] * n_prev, axis=-1, keepdims=True)
            for s in range(t + 1):
                s_ts = jnp.sum(q[t] * k[s], axis=-1, keepdims=True) * _lane_col(dmat[t][s], h)
                num = num + s_ts * vv[s]
                den = den + s_ts
            hout = num / jnp.maximum(jnp.abs(den), _lane_col(floor[t], h))
            hout = hout * lax.rsqrt(jnp.mean(hout * hout, axis=-1, keepdims=True) + EPS)
            y_ref[t, :, 2 * gw + h * HEAD_DIM:2 * gw + (h + 1) * HEAD_DIM] = (
                _sigmoid(co_ref[t][:, hs]) * hout * mn_ref[:, hs])


def _mix_sample(z3, gates3, lw, pool_prefix, conv_prefix, c_all, n_all, m_all, c_new_all, layer, nb=8):
    ts, n_seq, _ = z3.shape
    n_in = N_SLICES + 13
    gw = GROUP_WIDTH
    zspecs = [pl.BlockSpec((ts, nb, gw), functools.partial(lambda i, j: (0, i, j), j=j)) for j in range(N_SLICES)]
    in_specs = zspecs + [
        pl.BlockSpec((ts, nb, LANES), lambda i: (0, i, 0)),
        _const_spec((ts * ts, gw)),
        _const_spec((ts, gw)),
        _const_spec((N_HEADS, HEAD_DIM, HEAD_DIM)),
        _const_spec((1, gw)),
        _const_spec((1, gw)),
        _const_spec((1, LANES)),
        _const_spec((3, gw)),
        pl.BlockSpec((POOL_STATE, nb, gw), lambda i: (0, i, 0)),
        pl.BlockSpec((2, nb, gw), lambda i: (0, i, 0)),
        pl.BlockSpec((None, nb, N_HEADS, HEAD_DIM, HEAD_DIM), lambda i: (layer, i, 0, 0, 0)),
        pl.BlockSpec((None, nb, gw), lambda i: (layer, i, 0)),
        pl.BlockSpec((None, nb, LANES), lambda i: (layer, i, 0)),
        pl.BlockSpec(memory_space=pl.ANY),
    ]
    assert len(in_specs) == n_in + 1
    out_shape = [
        jax.ShapeDtypeStruct((ts, n_seq, 4 * gw), F32),
        jax.ShapeDtypeStruct((ts, n_seq, gw), F32),
        jax.ShapeDtypeStruct((2, n_seq, gw), F32),
        jax.ShapeDtypeStruct(c_new_all.shape, F32),
        jax.ShapeDtypeStruct((n_seq, gw), F32),
        jax.ShapeDtypeStruct((n_seq, LANES), F32),
    ]
    out_specs = [
        pl.BlockSpec((ts, nb, 4 * gw), lambda i: (0, i, 0)),
        pl.BlockSpec((ts, nb, gw), lambda i: (0, i, 0)),
        pl.BlockSpec((2, nb, gw), lambda i: (0, i, 0)),
        pl.BlockSpec((None, nb, N_HEADS, HEAD_DIM, HEAD_DIM), lambda i: (layer, i, 0, 0, 0)),
        pl.BlockSpec((nb, gw), lambda i: (i, 0)),
        pl.BlockSpec((nb, LANES), lambda i: (i, 0)),
    ]
    return pl.pallas_call(
        _mix_sample_kernel,
        grid=(n_seq // nb,),
        in_specs=in_specs,
        out_specs=out_specs,
        out_shape=out_shape,
        input_output_aliases={n_in: 3},
        compiler_params=_params(1),
        name="mix_sample",
    )(*([z3] * N_SLICES), gates3, lw["ws_tab"], lw["bs_tab"], lw["w_pool"], lw["pool_scale"],
      lw["mlstm_norm"], lw["bg_row"], lw["w_dconv"], pool_prefix, conv_prefix, c_all, n_all, m_all, c_new_all)


def _out_q_kernel(x_ref, y_ref, wo_ref, g_ref, wq_ref, x1_ref, q_ref):
    x1 = x_ref[...] + jnp.dot(y_ref[...].astype(BF16), wo_ref[...], preferred_element_type=F32)
    x1_ref[...] = x1
    q_ref[...] = jnp.dot(_rms(x1, g_ref[...]).astype(BF16), wq_ref[...], preferred_element_type=F32)


def _layer_spec(shape, layer):
    nd = len(shape)
    return pl.BlockSpec((None,) + tuple(shape), lambda *_: (layer,) + (0,) * nd, pipeline_mode=pl.Buffered(1))


def _out_q(x, y, w_out_all, g, w_q_all, layer, tm=512):
    m, d = x.shape
    tm = min(tm, m)
    nq = w_q_all.shape[2]
    return pl.pallas_call(
        _out_q_kernel,
        grid=(m // tm,),
        in_specs=[
            pl.BlockSpec((tm, d), lambda i: (i, 0)),
            pl.BlockSpec((tm, y.shape[1]), lambda i: (i, 0)),
            _layer_spec(w_out_all.shape[1:], layer),
            _const_spec((1, d)),
            _layer_spec(w_q_all.shape[1:], layer),
        ],
        out_specs=[pl.BlockSpec((tm, d), lambda i: (i, 0)), pl.BlockSpec((tm, nq), lambda i: (i, 0))],
        out_shape=[jax.ShapeDtypeStruct((m, d), F32), jax.ShapeDtypeStruct((m, nq), F32)],
        compiler_params=_params(1),
        name="out_q",
    )(x, y, w_out_all, g, w_q_all)


def _softmax_rows(s):
    e = jnp.exp(s - jnp.max(s, axis=-1, keepdims=True))
    return e / jnp.sum(e, axis=-1, keepdims=True)


def _attn_prompt_kernel(q_ref, k_ref, v_ref, o_ref):
    scale = HEAD_DIM ** -0.5
    for h in range(N_HEADS):
        hs = slice(h * HEAD_DIM, (h + 1) * HEAD_DIM)
        q = q_ref[:, hs].astype(BF16)
        s = lax.dot_general(q, k_ref[:, hs].astype(BF16), _NT, preferred_element_type=F32) * scale
        p = _softmax_rows(s).astype(BF16)
        o_ref[:, hs] = jnp.dot(p, v_ref[:, hs].astype(BF16), preferred_element_type=F32)


def _attn_prompt(q, k, v, n_seq, seq, tq=512):
    tps = seq // tq
    w = q.shape[1]
    return pl.pallas_call(
        _attn_prompt_kernel,
        grid=(n_seq, tps),
        in_specs=[
            pl.BlockSpec((tq, w), lambda n, t: (n * tps + t, 0)),
            pl.BlockSpec((None, N_MEM, w), lambda n, t: (n, 0, 0)),
            pl.BlockSpec((None, N_MEM, w), lambda n, t: (n, 0, 0)),
        ],
        out_specs=pl.BlockSpec((tq, w), lambda n, t: (n * tps + t, 0)),
        out_shape=jax.ShapeDtypeStruct(q.shape, F32),
        compiler_params=_params(2),
        name="attn_prompt",
    )(q, k, v)


def _attn_sample_kernel(q_ref, k_ref, v_ref, o_ref):
    nb, rows, _ = q_ref.shape
    n_kv = k_ref.shape[1]
    scale = HEAD_DIM ** -0.5
    row_head = lax.broadcasted_iota(jnp.int32, (rows, n_kv), 0) % N_HEADS
    col_head = lax.broadcasted_iota(jnp.int32, (rows, n_kv), 1) % N_HEADS
    same_head = row_head == col_head
    scores = []
    for i in range(nb):
        s = lax.dot_general(q_ref[i].astype(BF16), k_ref[i].astype(BF16), _NT, preferred_element_type=F32) * scale
        scores.append(jnp.where(same_head, s, -jnp.inf))
    p = _softmax_rows(jnp.concatenate(scores, axis=0)).astype(BF16)
    for i in range(nb):
        o_ref[i] = jnp.dot(p[i * rows:(i + 1) * rows, :], v_ref[i].astype(BF16), preferred_element_type=F32)


def _attn_sample(q, k_all, v_all, layer, nb=8):
    n_seq, rows, hd = q.shape
    n_kv = k_all.shape[2]
    return pl.pallas_call(
        _attn_sample_kernel,
        grid=(n_seq // nb,),
        in_specs=[
            pl.BlockSpec((nb, rows, hd), lambda i: (i, 0, 0)),
            pl.BlockSpec((None, nb, n_kv, hd), lambda i: (layer, i, 0, 0)),
            pl.BlockSpec((None, nb, n_kv, hd), lambda i: (layer, i, 0, 0)),
        ],
        out_specs=pl.BlockSpec((nb, rows, hd), lambda i: (i, 0, 0)),
        out_shape=jax.ShapeDtypeStruct(q.shape, F32),
        compiler_params=_params(1),
        name="attn_sample",
    )(q, k_all, v_all)


def _ffn_kernel(x1_ref, o_ref, wo_ref, g_ref, wg_ref, wu_ref, fg_ref, fu_ref, wd_ref, pg_ref, pu_ref, fn_ref,
                out_ref, tg_ref, tu_ref, hn_sc, eg_sc, eu_sc, cg_sc, cu_sc, *, shift, tiles_per_seq, chunk,
                apply_final_norm):
    i = pl.program_id(0)
    j = pl.program_id(1)
    tm = out_ref.shape[0]
    pad = pg_ref.shape[0]

    @pl.when(j == 0)
    def _():
        for r in range(tm // chunk):
            rs = slice(r * chunk, (r + 1) * chunk)
            x2 = x1_ref[rs, :] + jnp.dot(o_ref[rs, :].astype(BF16), wo_ref[...], preferred_element_type=F32)
            out_ref[rs, :] = x2
            hn_sc[rs, :] = _rms(x2, g_ref[...]).astype(BF16)

    first = (i % tiles_per_seq) == 0

    @pl.when(first)
    def _():
        eg_sc[0:pad, :] = pg_ref[...]
        eu_sc[0:pad, :] = pu_ref[...]

    @pl.when(jnp.logical_not(first))
    def _():
        eg_sc[0:pad, :] = cg_sc[j]
        eu_sc[0:pad, :] = cu_sc[j]

    def conv(e_sc, f_ref, r0, r1):
        return (f_ref[0:1, :] * e_sc[pad - 2 * shift + r0:pad - 2 * shift + r1, :]
                + f_ref[1:2, :] * e_sc[pad - shift + r0:pad - shift + r1, :]
                + f_ref[2:3, :] * e_sc[pad + r0:pad + r1, :])

    def up(r0, r1):
        hn = hn_sc[r0:r1, :]
        eg_sc[pad + r0:pad + r1, :] = jnp.dot(hn, wg_ref[...], preferred_element_type=F32)
        eu_sc[pad + r0:pad + r1, :] = jnp.dot(hn, wu_ref[...], preferred_element_type=F32)

    n_chunks = tm // chunk
    spans = [(r * chunk, (r + 1) * chunk) for r in range(n_chunks)]
    ahead = 2
    for span in spans[:ahead]:
        up(*span)
    for idx, (r0, r1) in enumerate(spans):
        if idx + ahead < len(spans):
            up(*spans[idx + ahead])
        gate = conv(eg_sc, fg_ref, r0, r1)
        act = (gate * _sigmoid(gate) * conv(eu_sc, fu_ref, r0, r1)).astype(BF16)
        out_ref[r0:r1, :] += jnp.dot(act, wd_ref[...], preferred_element_type=F32)

    tail_g = eg_sc[tm:tm + pad, :]
    tail_u = eu_sc[tm:tm + pad, :]
    cg_sc[j] = tail_g
    cu_sc[j] = tail_u
    tg_ref[...] = tail_g
    tu_ref[...] = tail_u

    if apply_final_norm:
        @pl.when(j == pl.num_programs(1) - 1)
        def _():
            for r in range(n_chunks):
                rs = slice(r * chunk, (r + 1) * chunk)
                out_ref[rs, :] = _rms(out_ref[rs, :], fn_ref[...])


def _ffn(x1, o, w_o_all, g, w_up_all, f_conv, w_down_all, prefix_all, final_g, layer, pre_layer, tm, shift,
         tiles_per_seq, apply_final_norm, chunk=256):
    m, d = x1.shape
    dfp = w_down_all.shape[1]
    tn = FF_TILE
    nj = dfp // tn
    pad = prefix_all.shape[1]
    n_tiles = m // tm
    return pl.pallas_call(
        functools.partial(_ffn_kernel, shift=shift, tiles_per_seq=tiles_per_seq, chunk=min(chunk, tm),
                          apply_final_norm=apply_final_norm),
        grid=(n_tiles, nj),
        in_specs=[
            pl.BlockSpec((tm, d), lambda i, j: (i, 0), pipeline_mode=pl.Buffered(1)),
            pl.BlockSpec((tm, o.shape[1]), lambda i, j: (i, 0)),
            _layer_spec(w_o_all.shape[1:], layer),
            _const_spec((1, d)),
            pl.BlockSpec((None, d, tn), lambda i, j: (layer, 0, j)),
            pl.BlockSpec((None, d, tn), lambda i, j: (layer, 0, nj + j)),
            pl.BlockSpec((3, tn), lambda i, j: (0, j)),
            pl.BlockSpec((3, tn), lambda i, j: (0, nj + j)),
            pl.BlockSpec((None, tn, d), lambda i, j: (layer, j, 0)),
            pl.BlockSpec((None, pad, tn), lambda i, j: (pre_layer, 0, j)),
            pl.BlockSpec((None, pad, tn), lambda i, j: (pre_layer, 0, nj + j)),
            _const_spec((1, d)),
        ],
        out_specs=[
            pl.BlockSpec((tm, d), lambda i, j: (i, 0)),
            pl.BlockSpec((pad, tn), lambda i, j: (i, j)),
            pl.BlockSpec((pad, tn), lambda i, j: (i, j)),
        ],
        out_shape=[
            jax.ShapeDtypeStruct((m, d), F32),
            jax.ShapeDtypeStruct((n_tiles * pad, dfp), F32),
            jax.ShapeDtypeStruct((n_tiles * pad, dfp), F32),
        ],
        scratch_shapes=[
            pltpu.VMEM((tm, d), BF16),
            pltpu.VMEM((pad + tm, tn), F32),
            pltpu.VMEM((pad + tm, tn), F32),
            pltpu.VMEM((nj, pad, tn), F32),
            pltpu.VMEM((nj, pad, tn), F32),
        ],
        compiler_params=_params(2),
        name="ffn",
    )(x1, o, w_o_all, g, w_up_all, w_up_all, f_conv, f_conv, w_down_all, prefix_all, prefix_all, final_g)


def _cast_up_kernel(w_ref, o_ref):
    d_ff = w_ref.shape[1] // 2
    dfp = o_ref.shape[1] // 2
    rows = w_ref.shape[0]
    for half in range(2):
        o_ref[:, half * dfp:half * dfp + d_ff] = w_ref[:, half * d_ff:(half + 1) * d_ff].astype(BF16)
        o_ref[:, half * dfp + d_ff:(half + 1) * dfp] = jnp.zeros((rows, dfp - d_ff), BF16)


def _cast_up(w_up, dfp, tr=128):
    depth, d, two_dff = w_up.shape
    return pl.pallas_call(
        _cast_up_kernel,
        grid=(depth, d // tr),
        in_specs=[pl.BlockSpec((None, tr, two_dff), lambda l, i: (l, i, 0))],
        out_specs=pl.BlockSpec((None, tr, 2 * dfp), lambda l, i: (l, i, 0)),
        out_shape=jax.ShapeDtypeStruct((depth, d, 2 * dfp), BF16),
        compiler_params=_params(2),
        name="cast_up",
    )(w_up)


def _cast_down_kernel(w_ref, rem_ref, o_ref):
    i = pl.program_id(1)
    last = pl.num_programs(1) - 1
    rem = rem_ref.shape[0]

    @pl.when(i < last)
    def _():
        o_ref[...] = w_ref[...].astype(BF16)

    @pl.when(i == last)
    def _():
        o_ref[0:rem, :] = rem_ref[...].astype(BF16)
        o_ref[rem:, :] = jnp.zeros((o_ref.shape[0] - rem, o_ref.shape[1]), BF16)


def _cast_down(w_down, dfp, tr=256):
    depth, d_ff, d = w_down.shape
    n_full = d_ff // tr
    rem = d_ff - n_full * tr
    assert 0 < rem < tr and d_ff % rem == 0 and dfp == (n_full + 1) * tr
    return pl.pallas_call(
        _cast_down_kernel,
        grid=(depth, n_full + 1),
        in_specs=[
            pl.BlockSpec((None, tr, d), lambda l, i: (l, jnp.minimum(i, n_full - 1), 0)),
            pl.BlockSpec((None, rem, d), lambda l, i: (l, d_ff // rem - 1, 0)),
        ],
        out_specs=pl.BlockSpec((None, tr, d), lambda l, i: (l, i, 0)),
        out_shape=jax.ShapeDtypeStruct((depth, dfp, d), BF16),
        compiler_params=_params(2),
        name="cast_down",
    )(w_down, w_down)


def _pad_cols(a, width):
    return jnp.pad(a, [(0, 0)] * (a.ndim - 1) + [(0, width - a.shape[-1])])


def _layer_params(l, ts, d_ff, dfp, norm_mix, w_s, b_s, w_pool, pool_scale, b_gate, mlstm_norm, w_dconv, norm_cross,
                  norm_ffn, w_fconv):
    ws_l, bs_l, fc = w_s[l], b_s[l], w_fconv[l]
    return {
        "norm_mix": norm_mix[l][None, :],
        "w_s": ws_l,
        "b_s_col": bs_l[:, :, None],
        "ws_tab": jnp.repeat(jnp.transpose(ws_l[:, :ts, :ts], (1, 2, 0)).reshape(ts * ts, N_HEADS), HEAD_DIM, axis=1),
        "bs_tab": jnp.repeat(jnp.transpose(bs_l[:, :ts], (1, 0)), HEAD_DIM, axis=1),
        "w_pool": w_pool[l],
        "pool_scale": pool_scale[l][None, :],
        "mlstm_norm": mlstm_norm[l][None, :],
        "bg_row": _pad_cols(b_gate[l][None, :], LANES),
        "bg_col": b_gate[l][:, None],
        "w_dconv": w_dconv[l],
        "norm_cross": norm_cross[l][None, :],
        "norm_ffn": norm_ffn[l][None, :],
        "f_conv": jnp.concatenate([_pad_cols(fc[:, :d_ff], dfp), _pad_cols(fc[:, d_ff:], dfp)], axis=1),
    }


def kernel(x_prompt, x_sample, mem_prompt, state_pool, state_conv, state_mlstm_c, state_mlstm_n, state_mlstm_m,
           state_ffn_conv, cache_mem_k, cache_mem_v, norm_mix, w_in, w_s, b_s, w_pool, pool_scale, b_gate,
           mlstm_norm, w_dconv, w_out, mem_norm, w_kv, norm_cross, w_q, w_o, norm_ffn, w_up, w_fconv, w_down,
           final_norm):
    depth = w_in.shape[0]
    nbp, seq, d = x_prompt.shape
    nbs, ts, _ = x_sample.shape
    gw = GROUP_WIDTH
    mem_w = w_q.shape[2]
    n_main = N_SLICES * gw
    d_ff = w_down.shape[1]
    dfp = -(-d_ff // FF_TILE) * FF_TILE

    w_gate_b = _pad_cols(w_in[:, :, n_main:], LANES).astype(BF16)
    w_in_b = w_in.astype(BF16)
    w_out_b = w_out.astype(BF16)
    w_q_b = w_q.astype(BF16)
    w_o_b = w_o.astype(BF16)
    w_kv_b = w_kv.astype(BF16)
    w_up_b = _cast_up(w_up, dfp)
    w_down_b = _cast_down(w_down, dfp)

    xp = x_prompt.reshape(nbp * seq, d)
    xs = jnp.transpose(x_sample, (1, 0, 2)).reshape(ts * nbs, d)
    mem2 = mem_prompt.reshape(nbp * N_MEM, d)
    kc_all = cache_mem_k.reshape(depth, nbs, N_MEM * N_HEADS, HEAD_DIM)
    vc_all = cache_mem_v.reshape(depth, nbs, N_MEM * N_HEADS, HEAD_DIM)
    n_all = state_mlstm_n.reshape(depth, nbs, gw)
    m_all = _pad_cols(state_mlstm_m, LANES)
    pool_pre_all = jnp.transpose(state_pool, (0, 2, 1, 3))
    conv_pre_all = jnp.transpose(state_conv, (0, 2, 1, 3))
    ffn_pre_t = jnp.transpose(state_ffn_conv, (0, 2, 1, 3))
    ffn_pre_all = jnp.concatenate([_pad_cols(ffn_pre_t[..., :d_ff], dfp), _pad_cols(ffn_pre_t[..., d_ff:], dfp)],
                                  axis=-1).reshape(depth, 2 * nbs, 2 * dfp)
    pad_p = SUBLANES
    zeros_pre = jnp.zeros((1, pad_p, 2 * dfp), F32)
    final_g = final_norm[None, :]
    s_c = jnp.zeros(state_mlstm_c.shape, F32)

    tm_p = min(1024, seq)
    tm_s = ts * nbs
    outs = {k: [] for k in ("p_pool", "p_conv", "p_c", "p_n", "p_m", "p_tg", "p_tu", "p_kv",
                            "s_v", "s_bx", "s_conv", "s_n", "s_m", "s_tg", "s_tu")}
    for l in range(depth):
        last = l == depth - 1
        lw = _layer_params(l, ts, d_ff, dfp, norm_mix, w_s, b_s, w_pool, pool_scale, b_gate, mlstm_norm, w_dconv,
                           norm_cross, norm_ffn, w_fconv)

        kv = _norm_mm(mem2, mem_norm[l][None, :], w_kv_b, l, tm=mem2.shape[0], tn=512)
        outs["p_kv"].append(kv)
        mk = kv[:, :mem_w].reshape(nbp, N_MEM, mem_w)
        mv = kv[:, mem_w:].reshape(nbp, N_MEM, mem_w)

        z, gates = _proj_in(xp, lw["norm_mix"], w_in_b, w_gate_b, l, n_main, tm=tm_p)
        gates_t = jnp.transpose(gates[:, :SUBLANES])
        x1, q, conv_st, c1, n1, m1 = _mix_out(z, gates, gates_t, lw, xp, w_out_b, w_q_b, l, nbp, seq)
        outs["p_pool"].append(z.reshape(nbp, seq, n_main)[:, seq - POOL_STATE:, 2 * gw:3 * gw])
        outs["p_conv"].append(conv_st)
        outs["p_c"].append(c1)
        outs["p_n"].append(n1[:, :N_HEADS, :])
        outs["p_m"].append(m1[:, :N_HEADS, 0])
        o = _attn_prompt(q, mk, mv, nbp, seq)
        xp, tail_g, tail_u = _ffn(x1, o, w_o_b, lw["norm_ffn"], w_up_b, lw["f_conv"], w_down_b, zeros_pre, final_g,
                                  l, 0, tm=tm_p, shift=1, tiles_per_seq=seq // tm_p, apply_final_norm=last,
                                  chunk=512)
        outs["p_tg"].append(tail_g)
        outs["p_tu"].append(tail_u)

        z, gates = _proj_in(xs, lw["norm_mix"], w_in_b, w_gate_b, l, n_main, tm=tm_s)
        z3 = z.reshape(ts, nbs, n_main)
        y3, v_rows, conv_st, s_c, n1, m1 = _mix_sample(
            z3, gates.reshape(ts, nbs, LANES), lw, pool_pre_all[l], conv_pre_all[l], state_mlstm_c, n_all, m_all,
            s_c, l)
        outs["s_v"].append(v_rows)
        outs["s_bx"].append(z3[:, :, 2 * gw:3 * gw])
        outs["s_conv"].append(conv_st)
        outs["s_n"].append(n1)
        outs["s_m"].append(m1)
        x1, q = _out_q(xs, y3.reshape(ts * nbs, 4 * gw), w_out_b, lw["norm_cross"], w_q_b, l)
        q_seq = jnp.transpose(q.reshape(ts, nbs, N_HEADS, HEAD_DIM), (1, 0, 2, 3)).reshape(nbs, ts * N_HEADS, HEAD_DIM)
        o_seq = _attn_sample(q_seq, kc_all, vc_all, l)
        o = jnp.transpose(o_seq.reshape(nbs, ts, N_HEADS, HEAD_DIM), (1, 0, 2, 3)).reshape(ts * nbs, mem_w)
        xs, tail_g, tail_u = _ffn(x1, o, w_o_b, lw["norm_ffn"], w_up_b, lw["f_conv"], w_down_b, ffn_pre_all, final_g,
                                  l, l, tm=tm_s, shift=nbs, tiles_per_seq=1, apply_final_norm=last)
        outs["s_tg"].append(tail_g)
        outs["s_tu"].append(tail_u)

    y_prompt = xp.reshape(nbp, seq, d)
    y_sample = jnp.transpose(xs.reshape(ts, nbs, d), (1, 0, 2))

    st = {k: jnp.stack(v) for k, v in outs.items()}
    tps = seq // tm_p

    def prompt_tail(t):
        return t.reshape(depth, nbp, tps, pad_p, dfp)[:, :, -1, pad_p - 2:, :d_ff]

    p_ffn = jnp.concatenate([prompt_tail(st["p_tg"]), prompt_tail(st["p_tu"])], axis=-1)
    p_kv = st["p_kv"].reshape(depth, nbp, N_MEM, 2, N_HEADS, HEAD_DIM)

    def sample_tail(t):
        return jnp.transpose(t.reshape(depth, 2, nbs, dfp)[..., :d_ff], (0, 2, 1, 3))

    s_ffn = jnp.concatenate([sample_tail(st["s_tg"]), sample_tail(st["s_tu"])], axis=-1)
    s_pool = jnp.concatenate([state_pool, jnp.transpose(st["s_bx"], (0, 2, 1, 3))], axis=2)[:, :, -POOL_STATE:]
    return (y_prompt, y_sample,
            st["p_pool"], st["p_conv"], st["p_c"], st["p_n"], st["p_m"], p_ffn, p_kv[:, :, :, 0], p_kv[:, :, :, 1],
            jnp.transpose(st["s_v"], (0, 2, 1, 3)), s_pool, jnp.transpose(st["s_conv"], (0, 2, 1, 3)),
            s_c, st["s_n"].reshape(depth, nbs, N_HEADS, HEAD_DIM), st["s_m"][:, :, :N_HEADS], s_ffn)
```

```python
import functools

import jax
import jax.numpy as jnp
from jax import lax
from jax.experimental import pallas as pl
from jax.experimental.pallas import tpu as pltpu

F32 = jnp.float32
BF16 = jnp.bfloat16
EPS = 1e-6

LANES = 128
SUBLANES = 8
GROUP_WIDTH = 512
HEAD_DIM = 128
N_HEADS = GROUP_WIDTH // HEAD_DIM
N_SLICES = 10
POOL_WINDOWS = (2, 4, 8, 16)
POOL_STATE = max(POOL_WINDOWS) - 1
N_MEM = 256
FF_TILE = 512
VMEM_LIMIT = 60 * 1024 * 1024
HIGHEST = lax.Precision.HIGHEST

_NT = (((1,), (1,)), ((), ()))


def _params(n_axes, vmem=VMEM_LIMIT):
    return pltpu.CompilerParams(dimension_semantics=("arbitrary",) * n_axes, vmem_limit_bytes=vmem)


def _rms(x, g):
    return x * lax.rsqrt(jnp.mean(x * x, axis=-1, keepdims=True) + EPS) * g


def _gelu(x):
    return 0.5 * x * (1.0 + jnp.tanh(0.7978845608028654 * (x + 0.044715 * (x * x * x))))


def _sigmoid(x):
    return 1.0 / (1.0 + jnp.exp(-x))


def _log_sigmoid(x):
    return jnp.minimum(x, 0.0) - jnp.log(1.0 + jnp.exp(-jnp.abs(x)))


def _lane_col(x, lane):
    idx = lax.broadcasted_iota(jnp.int32, x.shape, 1)
    return jnp.sum(jnp.where(idx == lane, x, 0.0), axis=-1, keepdims=True)


def _const_spec(shape):
    nd = len(shape)
    return pl.BlockSpec(shape, lambda *_: (0,) * nd, pipeline_mode=pl.Buffered(1))


def _proj_in_kernel(x_ref, g_ref, w_ref, wg_ref, z_ref, gate_ref, hn_sc):
    @pl.when(pl.program_id(1) == 0)
    def _():
        hn_sc[...] = _rms(x_ref[...], g_ref[...]).astype(BF16)
        gate_ref[...] = jnp.dot(hn_sc[...], wg_ref[...], preferred_element_type=F32)

    z_ref[...] = jnp.dot(hn_sc[...], w_ref[...], preferred_element_type=F32)


def _proj_in(x, g, w_all, wg_all, layer, n, tm, tn=1024):
    m, d = x.shape
    return pl.pallas_call(
        _proj_in_kernel,
        grid=(m // tm, n // tn),
        in_specs=[
            pl.BlockSpec((tm, d), lambda i, j: (i, 0)),
            _const_spec((1, d)),
            pl.BlockSpec((None, d, tn), lambda i, j: (layer, 0, j)),
            pl.BlockSpec((None, d, LANES), lambda i, j: (layer, 0, 0), pipeline_mode=pl.Buffered(1)),
        ],
        out_specs=[
            pl.BlockSpec((tm, tn), lambda i, j: (i, j)),
            pl.BlockSpec((tm, LANES), lambda i, j: (i, 0)),
        ],
        out_shape=[jax.ShapeDtypeStruct((m, n), F32), jax.ShapeDtypeStruct((m, LANES), F32)],
        scratch_shapes=[pltpu.VMEM((tm, d), BF16)],
        compiler_params=_params(2),
        name="proj_in",
    )(x, g, w_all, wg_all)


def _norm_mm_kernel(x_ref, g_ref, w_ref, o_ref, hn_sc):
    @pl.when(pl.program_id(1) == 0)
    def _():
        hn_sc[...] = _rms(x_ref[...], g_ref[...]).astype(BF16)

    o_ref[...] = jnp.dot(hn_sc[...], w_ref[...], preferred_element_type=F32)


def _norm_mm(x, g, w_all, layer, tm, tn):
    m, d = x.shape
    n = w_all.shape[2]
    return pl.pallas_call(
        _norm_mm_kernel,
        grid=(m // tm, n // tn),
        in_specs=[
            pl.BlockSpec((tm, d), lambda i, j: (i, 0)),
            _const_spec((1, d)),
            pl.BlockSpec((None, d, tn), lambda i, j: (layer, 0, j)),
        ],
        out_specs=pl.BlockSpec((tm, tn), lambda i, j: (i, j)),
        out_shape=jax.ShapeDtypeStruct((m, n), F32),
        scratch_shapes=[pltpu.VMEM((tm, d), BF16)],
        compiler_params=_params(2),
        name="norm_mm",
    )(x, g, w_all)


def _mix_out_kernel(au_ref, av_ref, bx_ref, cq_ref, ck_ref, cv_ref, co_ref, db_ref, dc_ref, dx_ref,
                    gc_ref, gr_ref, ws_ref, bs_ref, wp_ref, ps_ref, mn_ref, bgr_ref, bgc_ref, wd_ref,
                    x_ref, wo_ref, nc_ref, wq_ref,
                    x1_ref, q_ref, cs_ref, c_ref, n_ref, m_ref,
                    c_sc, n_sc, m_sc, xx_sc, xd_sc, y_sc, *, tiles_per_seq):
    g = pl.program_id(0)
    n_steps = pl.num_programs(0)
    t = jnp.minimum(g, n_steps - 2) % tiles_per_seq
    rows = x1_ref.shape[0]
    y_ref = y_sc.at[g % 2]
    y_prev = y_sc.at[(g + 1) % 2]

    @pl.when(g == 0)
    def _():
        y_sc[1] = jnp.zeros(y_sc.shape[1:], y_sc.dtype)
    gw = GROUP_WIDTH
    pool_pad = 2 * SUBLANES
    conv_pad = SUBLANES

    @pl.when(t == 0)
    def _():
        c_sc[...] = jnp.zeros_like(c_sc)
        n_sc[...] = jnp.zeros_like(n_sc)
        m_sc[...] = jnp.zeros_like(m_sc)
        xx_sc[0:pool_pad, :] = jnp.zeros((pool_pad, gw), F32)
        xd_sc[0:conv_pad, :] = jnp.zeros((conv_pad, gw), F32)

    col_chunk = x_ref.shape[1] // 4

    def out_proj(c):
        cs = slice(c * col_chunk, (c + 1) * col_chunk)
        x1_ref[:, cs] = x_ref[:, cs] + jnp.dot(y_prev[...], wo_ref[:, cs], preferred_element_type=F32)

    def q_proj():
        hn = _rms(x1_ref[...], nc_ref[...]).astype(BF16)
        q_ref[...] = jnp.dot(hn, wq_ref[...], preferred_element_type=F32)

    def mixer_a():
        ri = lax.broadcasted_iota(jnp.int32, (HEAD_DIM, HEAD_DIM), 0)
        ci = lax.broadcasted_iota(jnp.int32, (HEAD_DIM, HEAD_DIM), 1)
        for h in range(N_HEADS):
            hs = slice(h * HEAD_DIM, (h + 1) * HEAD_DIM)
            w_tril = jnp.where(ri >= ci, ws_ref[h], 0.0).astype(BF16)
            for c in range(rows // HEAD_DIM):
                rs = slice(c * HEAD_DIM, (c + 1) * HEAD_DIM)
                v = _gelu(av_ref[rs, hs]).astype(BF16)
                mixed = jnp.dot(w_tril, v, preferred_element_type=F32) + bs_ref[h]
                y_ref[rs, hs] = (_gelu(au_ref[rs, hs]) * mixed).astype(y_ref.dtype)

    def mixer_b():
        xx_sc[pool_pad:pool_pad + rows, :] = bx_ref[...]
        avail = (t * rows + 1 + lax.broadcasted_iota(jnp.int32, (rows, 1), 0)).astype(F32)
        for g, win in enumerate(POOL_WINDOWS):
            gs = slice(g * HEAD_DIM, (g + 1) * HEAD_DIM)
            acc = bx_ref[:, gs]
            for k in range(1, win):
                acc = acc + xx_sc[pool_pad - k:pool_pad - k + rows, gs]
            pooled = acc / jnp.minimum(avail, float(win)) - bx_ref[:, gs]
            yb = jnp.dot(pooled.astype(BF16), wp_ref[g].astype(BF16), preferred_element_type=F32) * ps_ref[:, gs]
            y_ref[:, gw + g * HEAD_DIM:gw + (g + 1) * HEAD_DIM] = yb.astype(y_ref.dtype)
        xx_sc[0:pool_pad, :] = xx_sc[rows:rows + pool_pad, :]

    def mixer_d():
        for g in range(gw // LANES):
            gs = slice(g * LANES, (g + 1) * LANES)
            dcx = dc_ref[:, gs] * dx_ref[:, gs]
            xd_sc[conv_pad:conv_pad + rows, gs] = dcx
            conv = (wd_ref[0:1, gs] * xd_sc[conv_pad - 2:conv_pad - 2 + rows, gs]
                    + wd_ref[1:2, gs] * xd_sc[conv_pad - 1:conv_pad - 1 + rows, gs]
                    + wd_ref[2:3, gs] * dcx)
            y_ref[:, 3 * gw + g * LANES:3 * gw + (g + 1) * LANES] = (db_ref[:, gs] * conv).astype(y_ref.dtype)
        cs_ref[...] = xd_sc[conv_pad + rows - 2:conv_pad + rows, :]
        xd_sc[0:conv_pad, :] = xd_sc[rows:rows + conv_pad, :]

    gc = gc_ref[...] + bgr_ref[...]
    gr = gr_ref[...] + bgc_ref[...]
    rr = lax.broadcasted_iota(jnp.int32, (rows, rows), 0)
    cc = lax.broadcasted_iota(jnp.int32, (rows, rows), 1)
    causal = rr >= cc
    b_cols = jnp.dot(causal.astype(F32), _log_sigmoid(gc), precision=HIGHEST, preferred_element_type=F32)
    b_rows = jnp.dot(_log_sigmoid(gr), (rr <= cc).astype(F32), precision=HIGHEST, preferred_element_type=F32)
    scale = HEAD_DIM ** -0.5

    def mixer_c(h):
        hs = slice(h * HEAD_DIM, (h + 1) * HEAD_DIM)
        bc = _lane_col(b_cols, N_HEADS + h)
        ic = _lane_col(gc, h)
        br = b_rows[N_HEADS + h:N_HEADS + h + 1, :]
        ir = gr[h:h + 1, :]
        m_prev = m_sc[h:h + 1, 0:1]
        log_d = jnp.where(causal, bc - br + ir, -jnp.inf)
        log_inter = bc + m_prev
        m_t = jnp.maximum(log_inter, jnp.max(log_d, axis=-1, keepdims=True))
        dmat = jnp.exp(log_d - m_t)
        inter = jnp.exp(log_inter - m_t)
        qf = cq_ref[:, hs]
        kf = ck_ref[:, hs] * scale
        vf = cv_ref[:, hs]
        q = qf.astype(BF16)
        k = kf.astype(BF16)
        s = lax.dot_general(q, k, _NT, preferred_element_type=F32) * dmat
        c_prev = c_sc[h]
        n_prev = n_sc[h:h + 1, :]
        num = (jnp.dot(s.astype(BF16), vf.astype(BF16), preferred_element_type=F32)
               + inter * lax.dot_general(q, c_prev.astype(BF16), _NT, preferred_element_type=F32))
        den = jnp.sum(s, axis=-1, keepdims=True) + inter * jnp.sum(qf * n_prev, axis=-1, keepdims=True)
        hout = num / jnp.maximum(jnp.abs(den), jnp.exp(-m_t))
        hout = hout * lax.rsqrt(jnp.mean(hout * hout, axis=-1, keepdims=True) + EPS)
        yc = _sigmoid(co_ref[:, hs]) * hout * mn_ref[:, hs]
        y_ref[:, 2 * gw + h * HEAD_DIM:2 * gw + (h + 1) * HEAD_DIM] = yc.astype(y_ref.dtype)
        m_new = m_t[rows - 1:rows, :]
        b_last = bc[rows - 1:rows, :]
        decay = jnp.exp(b_last + m_prev - m_new)
        wts = jnp.exp(b_last - bc + ic - m_new)
        wv_t = jnp.transpose(wts * vf).astype(BF16)
        c_sc[h] = decay * c_prev + jnp.dot(wv_t, k, preferred_element_type=F32)
        n_sc[h:h + 1, :] = decay * n_prev + jnp.sum(wts * kf, axis=0, keepdims=True)
        m_sc[h:h + 1, :] = jnp.broadcast_to(m_new, (1, LANES))

    mixer_c(0)
    out_proj(0)
    mixer_a()
    out_proj(1)
    mixer_c(1)
    out_proj(2)
    mixer_b()
    out_proj(3)
    mixer_c(2)
    q_proj()
    mixer_d()
    mixer_c(3)

    @pl.when(jnp.logical_and(t == tiles_per_seq - 1, g < n_steps - 1))
    def _():
        c_ref[...] = c_sc[...]
        n_ref[...] = n_sc[...]
        m_ref[...] = m_sc[...]


def _mix_out(z, gates, gates_t, lw, x, w_out_all, w_q_all, layer, n_seq, seq, rows=256):
    gw = GROUP_WIDTH
    d = x.shape[1]
    nq = w_q_all.shape[2]
    tps = seq // rows
    n_tiles = n_seq * tps
    last = n_tiles - 1

    def cur(g):
        return jnp.minimum(g, last)

    def prev(g):
        return jnp.maximum(g - 1, 0)

    zspecs = [pl.BlockSpec((rows, gw), functools.partial(lambda g, j: (cur(g), j), j=j)) for j in range(N_SLICES)]
    in_specs = zspecs + [
        pl.BlockSpec((rows, LANES), lambda g: (cur(g), 0)),
        pl.BlockSpec((SUBLANES, rows), lambda g: (0, cur(g))),
        _const_spec((N_HEADS, HEAD_DIM, HEAD_DIM)),
        _const_spec((N_HEADS, HEAD_DIM, 1)),
        _const_spec((N_HEADS, HEAD_DIM, HEAD_DIM)),
        _const_spec((1, gw)),
        _const_spec((1, gw)),
        _const_spec((1, LANES)),
        _const_spec((SUBLANES, 1)),
        _const_spec((3, gw)),
        pl.BlockSpec((rows, d), lambda g: (prev(g), 0)),
        _layer_spec(w_out_all.shape[1:], layer),
        _const_spec((1, d)),
        _layer_spec(w_q_all.shape[1:], layer),
    ]
    out_shape = [
        jax.ShapeDtypeStruct((n_seq * seq, d), F32),
        jax.ShapeDtypeStruct((n_seq * seq, nq), F32),
        jax.ShapeDtypeStruct((n_seq, 2, gw), F32),
        jax.ShapeDtypeStruct((n_seq, N_HEADS, HEAD_DIM, HEAD_DIM), F32),
        jax.ShapeDtypeStruct((n_seq, SUBLANES, HEAD_DIM), F32),
        jax.ShapeDtypeStruct((n_seq, SUBLANES, LANES), F32),
    ]
    out_specs = [
        pl.BlockSpec((rows, d), lambda g: (prev(g), 0)),
        pl.BlockSpec((rows, nq), lambda g: (prev(g), 0)),
        pl.BlockSpec((None, 2, gw), lambda g: (cur(g) // tps, 0, 0)),
        pl.BlockSpec((None, N_HEADS, HEAD_DIM, HEAD_DIM), lambda g: (cur(g) // tps, 0, 0, 0)),
        pl.BlockSpec((None, SUBLANES, HEAD_DIM), lambda g: (cur(g) // tps, 0, 0)),
        pl.BlockSpec((None, SUBLANES, LANES), lambda g: (cur(g) // tps, 0, 0)),
    ]
    scratch = [
        pltpu.VMEM((N_HEADS, HEAD_DIM, HEAD_DIM), F32),
        pltpu.VMEM((SUBLANES, HEAD_DIM), F32),
        pltpu.VMEM((SUBLANES, LANES), F32),
        pltpu.VMEM((2 * SUBLANES + rows, gw), F32),
        pltpu.VMEM((SUBLANES + rows, gw), F32),
        pltpu.VMEM((2, rows, 4 * gw), BF16),
    ]
    return pl.pallas_call(
        functools.partial(_mix_out_kernel, tiles_per_seq=tps),
        grid=(n_tiles + 1,),
        in_specs=in_specs,
        out_specs=out_specs,
        out_shape=out_shape,
        scratch_shapes=scratch,
        compiler_params=_params(1),
        name="mix_out",
    )(*([z] * N_SLICES), gates, gates_t, lw["w_s"], lw["b_s_col"], lw["w_pool"], lw["pool_scale"],
      lw["mlstm_norm"], lw["bg_row"], lw["bg_col"], lw["w_dconv"], x, w_out_all, lw["norm_cross"], w_q_all)


def _mix_sample_kernel(au_ref, av_ref, bx_ref, cq_ref, ck_ref, cv_ref, co_ref, db_ref, dc_ref, dx_ref,
                       g_ref, wst_ref, bst_ref, wp_ref, ps_ref, mn_ref, bgr_ref, wd_ref,
                       pp_ref, cp_ref, c0_ref, n0_ref, m0_ref, c_buf_ref,
                       y_ref, vr_ref, cs_ref, c1_ref, n1_ref, m1_ref):
    ts = au_ref.shape[0]
    nb = au_ref.shape[1]
    gw = GROUP_WIDTH

    v = [_gelu(av_ref[t]) for t in range(ts)]
    for t in range(ts):
        vr_ref[t] = v[t]
        mixed = bst_ref[t:t + 1, :]
        for s in range(t + 1):
            mixed = mixed + wst_ref[t * ts + s:t * ts + s + 1, :] * v[s]
        y_ref[t, :, 0:gw] = _gelu(au_ref[t]) * mixed

    bx = [bx_ref[t] for t in range(ts)]
    hist = [pp_ref[r] for r in range(POOL_STATE)] + bx
    pooled = []
    for t in range(ts):
        parts = []
        for g, win in enumerate(POOL_WINDOWS):
            gs = slice(g * HEAD_DIM, (g + 1) * HEAD_DIM)
            acc = hist[POOL_STATE + t][:, gs]
            for k in range(1, win):
                acc = acc + hist[POOL_STATE + t - k][:, gs]
            parts.append(acc / float(win) - bx[t][:, gs])
        pooled.append(parts)
    for g in range(len(POOL_WINDOWS)):
        gs = slice(g * HEAD_DIM, (g + 1) * HEAD_DIM)
        pg = jnp.concatenate([pooled[t][g] for t in range(ts)], axis=0).astype(BF16)
        yb = jnp.dot(pg, wp_ref[g].astype(BF16), preferred_element_type=F32) * ps_ref[:, gs]
        for t in range(ts):
            y_ref[t, :, gw + g * HEAD_DIM:gw + (g + 1) * HEAD_DIM] = yb[t * nb:(t + 1) * nb, :]

    dcx = [dc_ref[t] * dx_ref[t] for t in range(ts)]
    xx = [cp_ref[0], cp_ref[1]] + dcx
    for t in range(ts):
        conv = wd_ref[0:1, :] * xx[t] + wd_ref[1:2, :] * xx[t + 1] + wd_ref[2:3, :] * xx[t + 2]
        y_ref[t, :, 3 * gw:4 * gw] = db_ref[t] * conv
    cs_ref[0] = xx[ts]
    cs_ref[1] = xx[ts + 1]

    gate = [g_ref[t] + bgr_ref[...] for t in range(ts)]
    lf = [pltpu.roll(_log_sigmoid(gate[t]), LANES - N_HEADS, 1) for t in range(ts)]
    m0 = m0_ref[...]
    b = []
    for t in range(ts):
        b.append(lf[t] if t == 0 else b[t - 1] + lf[t])
    log_d = [[b[t] - b[s] + gate[s] for s in range(t + 1)] for t in range(ts)]
    m_t = []
    for t in range(ts):
        mt = b[t] + m0
        for s in range(t + 1):
            mt = jnp.maximum(mt, log_d[t][s])
        m_t.append(mt)
    dmat = [[jnp.exp(log_d[t][s] - m_t[t]) for s in range(t + 1)] for t in range(ts)]
    inter = [jnp.exp(b[t] + m0 - m_t[t]) for t in range(ts)]
    floor = [jnp.exp(-m_t[t]) for t in range(ts)]
    m_new = m_t[ts - 1]
    decay = jnp.exp(b[ts - 1] + m0 - m_new)
    wts = [jnp.exp(b[ts - 1] - b[s] + gate[s] - m_new) for s in range(ts)]
    m1_ref[...] = m_new

    scale = HEAD_DIM ** -0.5
    row_seq = lax.broadcasted_iota(jnp.int32, (ts * nb, HEAD_DIM), 0) % nb
    for h in range(N_HEADS):
        hs = slice(h * HEAD_DIM, (h + 1) * HEAD_DIM)
        q = [cq_ref[t][:, hs] for t in range(ts)]
        k = [ck_ref[t][:, hs] * scale for t in range(ts)]
        vv = [cv_ref[t][:, hs] for t in range(ts)]
        n_prev = n0_ref[:, hs]
        q_blk = jnp.concatenate(q, axis=0).astype(BF16)
        k_blk = jnp.concatenate(k, axis=0).astype(BF16)
        dec_col = _lane_col(decay, h)
        wt_col = [_lane_col(wts[s], h) for s in range(ts)]
        wv_blk = jnp.concatenate([wt_col[s] * vv[s] for s in range(ts)], axis=0)
        qc = jnp.zeros((ts * nb, HEAD_DIM), F32)
        for i in range(nb):
            c_prev = c0_ref[i, h]
            r = lax.dot_general(q_blk, c_prev.astype(BF16), _NT, preferred_element_type=F32)
            qc = jnp.where(row_seq == i, r, qc)
            wv_i = jnp.where(row_seq == i, wv_blk, 0.0).astype(BF16)
            upd = lax.dot_general(wv_i, k_blk, (((0,), (0,)), ((), ())), preferred_element_type=F32)
            c1_ref[i, h] = dec_col[i:i + 1, :] * c_prev + upd
        n_new = dec_col * n_prev
        for s in range(ts):
            n_new = n_new + wt_col[s] * k[s]
        n1_ref[:, hs] = n_new
        for t in range(ts):
            int_col = _lane_col(inter[t], h)
            num = int_col * qc[t * nb:(t + 1) * nb, :]
            den = int_col * jnp.sum(q[t] * n_prev, axis=-1, keepdims=True)
            for s in range(t + 1):
                s_ts = jnp.sum(q[t] * k[s], axis=-1, keepdims=True) * _lane_col(dmat[t][s], h)
                num = num + s_ts * vv[s]
                den = den + s_ts
            hout = num / jnp.maximum(jnp.abs(den), _lane_col(floor[t], h))
            hout = hout * lax.rsqrt(jnp.mean(hout * hout, axis=-1, keepdims=True) + EPS)
            y_ref[t, :, 2 * gw + h * HEAD_DIM:2 * gw + (h + 1) * HEAD_DIM] = (
                _sigmoid(co_ref[t][:, hs]) * hout * mn_ref[:, hs])


def _mix_sample(z3, gates3, lw, pool_prefix, conv_prefix, c_all, n_all, m_all, c_new_all, layer, nb=16):
    ts, n_seq, _ = z3.shape
    n_in = N_SLICES + 13
    gw = GROUP_WIDTH
    zspecs = [pl.BlockSpec((ts, nb, gw), functools.partial(lambda i, j: (0, i, j), j=j)) for j in range(N_SLICES)]
    in_specs = zspecs + [
        pl.BlockSpec((ts, nb, LANES), lambda i: (0, i, 0)),
        _const_spec((ts * ts, gw)),
        _const_spec((ts, gw)),
        _const_spec((N_HEADS, HEAD_DIM, HEAD_DIM)),
        _const_spec((1, gw)),
        _const_spec((1, gw)),
        _const_spec((1, LANES)),
        _const_spec((3, gw)),
        pl.BlockSpec((POOL_STATE, nb, gw), lambda i: (0, i, 0)),
        pl.BlockSpec((2, nb, gw), lambda i: (0, i, 0)),
        pl.BlockSpec((None, nb, N_HEADS, HEAD_DIM, HEAD_DIM), lambda i: (layer, i, 0, 0, 0)),
        pl.BlockSpec((None, nb, gw), lambda i: (layer, i, 0)),
        pl.BlockSpec((None, nb, LANES), lambda i: (layer, i, 0)),
        pl.BlockSpec(memory_space=pl.ANY),
    ]
    assert len(in_specs) == n_in + 1
    out_shape = [
        jax.ShapeDtypeStruct((ts, n_seq, 4 * gw), F32),
        jax.ShapeDtypeStruct((ts, n_seq, gw), F32),
        jax.ShapeDtypeStruct((2, n_seq, gw), F32),
        jax.ShapeDtypeStruct(c_new_all.shape, F32),
        jax.ShapeDtypeStruct((n_seq, gw), F32),
        jax.ShapeDtypeStruct((n_seq, LANES), F32),
    ]
    out_specs = [
        pl.BlockSpec((ts, nb, 4 * gw), lambda i: (0, i, 0)),
        pl.BlockSpec((ts, nb, gw), lambda i: (0, i, 0)),
        pl.BlockSpec((2, nb, gw), lambda i: (0, i, 0)),
        pl.BlockSpec((None, nb, N_HEADS, HEAD_DIM, HEAD_DIM), lambda i: (layer, i, 0, 0, 0)),
        pl.BlockSpec((nb, gw), lambda i: (i, 0)),
        pl.BlockSpec((nb, LANES), lambda i: (i, 0)),
    ]
    return pl.pallas_call(
        _mix_sample_kernel,
        grid=(n_seq // nb,),
        in_specs=in_specs,
        out_specs=out_specs,
        out_shape=out_shape,
        input_output_aliases={n_in: 3},
        compiler_params=_params(1),
        name="mix_sample",
    )(*([z3] * N_SLICES), gates3, lw["ws_tab"], lw["bs_tab"], lw["w_pool"], lw["pool_scale"],
      lw["mlstm_norm"], lw["bg_row"], lw["w_dconv"], pool_prefix, conv_prefix, c_all, n_all, m_all, c_new_all)


def _out_q_kernel(x_ref, y_ref, wo_ref, g_ref, wq_ref, x1_ref, q_ref):
    x1 = x_ref[...] + jnp.dot(y_ref[...].astype(BF16), wo_ref[...], preferred_element_type=F32)
    x1_ref[...] = x1
    q_ref[...] = jnp.dot(_rms(x1, g_ref[...]).astype(BF16), wq_ref[...], preferred_element_type=F32)


def _layer_spec(shape, layer):
    nd = len(shape)
    return pl.BlockSpec((None,) + tuple(shape), lambda *_: (layer,) + (0,) * nd, pipeline_mode=pl.Buffered(1))


def _out_q(x, y, w_out_all, g, w_q_all, layer, tm=512):
    m, d = x.shape
    tm = min(tm, m)
    nq = w_q_all.shape[2]
    return pl.pallas_call(
        _out_q_kernel,
        grid=(m // tm,),
        in_specs=[
            pl.BlockSpec((tm, d), lambda i: (i, 0)),
            pl.BlockSpec((tm, y.shape[1]), lambda i: (i, 0)),
            _layer_spec(w_out_all.shape[1:], layer),
            _const_spec((1, d)),
            _layer_spec(w_q_all.shape[1:], layer),
        ],
        out_specs=[pl.BlockSpec((tm, d), lambda i: (i, 0)), pl.BlockSpec((tm, nq), lambda i: (i, 0))],
        out_shape=[jax.ShapeDtypeStruct((m, d), F32), jax.ShapeDtypeStruct((m, nq), F32)],
        compiler_params=_params(1),
        name="out_q",
    )(x, y, w_out_all, g, w_q_all)


def _softmax_rows(s):
    e = jnp.exp(s - jnp.max(s, axis=-1, keepdims=True))
    return e / jnp.sum(e, axis=-1, keepdims=True)


def _attn_prompt_kernel(q_ref, k_ref, v_ref, o_ref):
    scale = HEAD_DIM ** -0.5
    for h in range(N_HEADS):
        hs = slice(h * HEAD_DIM, (h + 1) * HEAD_DIM)
        q = q_ref[:, hs].astype(BF16)
        s = lax.dot_general(q, k_ref[:, hs].astype(BF16), _NT, preferred_element_type=F32) * scale
        p = _softmax_rows(s).astype(BF16)
        o_ref[:, hs] = jnp.dot(p, v_ref[:, hs].astype(BF16), preferred_element_type=F32)


def _attn_prompt(q, k, v, n_seq, seq, tq=512):
    tps = seq // tq
    w = q.shape[1]
    return pl.pallas_call(
        _attn_prompt_kernel,
        grid=(n_seq, tps),
        in_specs=[
            pl.BlockSpec((tq, w), lambda n, t: (n * tps + t, 0)),
            pl.BlockSpec((None, N_MEM, w), lambda n, t: (n, 0, 0)),
            pl.BlockSpec((None, N_MEM, w), lambda n, t: (n, 0, 0)),
        ],
        out_specs=pl.BlockSpec((tq, w), lambda n, t: (n * tps + t, 0)),
        out_shape=jax.ShapeDtypeStruct(q.shape, F32),
        compiler_params=_params(2),
        name="attn_prompt",
    )(q, k, v)


def _attn_sample_kernel(q_ref, k_ref, v_ref, o_ref):
    nb, rows, _ = q_ref.shape
    n_kv = k_ref.shape[1]
    scale = HEAD_DIM ** -0.5
    row_head = lax.broadcasted_iota(jnp.int32, (rows, n_kv), 0) % N_HEADS
    col_head = lax.broadcasted_iota(jnp.int32, (rows, n_kv), 1) % N_HEADS
    same_head = row_head == col_head
    scores = []
    for i in range(nb):
        s = lax.dot_general(q_ref[i].astype(BF16), k_ref[i].astype(BF16), _NT, preferred_element_type=F32) * scale
        scores.append(jnp.where(same_head, s, -jnp.inf))
    p = _softmax_rows(jnp.concatenate(scores, axis=0)).astype(BF16)
    for i in range(nb):
        o_ref[i] = jnp.dot(p[i * rows:(i + 1) * rows, :], v_ref[i].astype(BF16), preferred_element_type=F32)


def _attn_sample(q, k_all, v_all, layer, nb=16):
    n_seq, rows, hd = q.shape
    n_kv = k_all.shape[2]
    return pl.pallas_call(
        _attn_sample_kernel,
        grid=(n_seq // nb,),
        in_specs=[
            pl.BlockSpec((nb, rows, hd), lambda i: (i, 0, 0)),
            pl.BlockSpec((None, nb, n_kv, hd), lambda i: (layer, i, 0, 0)),
            pl.BlockSpec((None, nb, n_kv, hd), lambda i: (layer, i, 0, 0)),
        ],
        out_specs=pl.BlockSpec((nb, rows, hd), lambda i: (i, 0, 0)),
        out_shape=jax.ShapeDtypeStruct(q.shape, F32),
        compiler_params=_params(1),
        name="attn_sample",
    )(q, k_all, v_all)


def _ffn_kernel(x1_ref, o_ref, wo_ref, g_ref, wg_ref, wu_ref, fg_ref, fu_ref, wd_ref, pg_ref, pu_ref, fn_ref,
                out_ref, tg_ref, tu_ref, hn_sc, eg_sc, eu_sc, cg_sc, cu_sc, *, shift, tiles_per_seq, chunk,
                apply_final_norm):
    i = pl.program_id(0)
    j = pl.program_id(1)
    tm = out_ref.shape[0]
    pad = pg_ref.shape[0]

    @pl.when(j == 0)
    def _():
        for r in range(tm // chunk):
            rs = slice(r * chunk, (r + 1) * chunk)
            x2 = x1_ref[rs, :] + jnp.dot(o_ref[rs, :].astype(BF16), wo_ref[...], preferred_element_type=F32)
            out_ref[rs, :] = x2
            hn_sc[rs, :] = _rms(x2, g_ref[...]).astype(BF16)

    first = (i % tiles_per_seq) == 0

    @pl.when(first)
    def _():
        eg_sc[0:pad, :] = pg_ref[...]
        eu_sc[0:pad, :] = pu_ref[...]

    @pl.when(jnp.logical_not(first))
    def _():
        eg_sc[0:pad, :] = cg_sc[j]
        eu_sc[0:pad, :] = cu_sc[j]

    def conv(e_sc, f_ref, r0, r1):
        return (f_ref[0:1, :] * e_sc[pad - 2 * shift + r0:pad - 2 * shift + r1, :]
                + f_ref[1:2, :] * e_sc[pad - shift + r0:pad - shift + r1, :]
                + f_ref[2:3, :] * e_sc[pad + r0:pad + r1, :])

    def up(r0, r1):
        hn = hn_sc[r0:r1, :]
        eg_sc[pad + r0:pad + r1, :] = jnp.dot(hn, wg_ref[...], preferred_element_type=F32)
        eu_sc[pad + r0:pad + r1, :] = jnp.dot(hn, wu_ref[...], preferred_element_type=F32)

    n_chunks = tm // chunk
    spans = [(r * chunk, (r + 1) * chunk) for r in range(n_chunks)]
    ahead = 2
    for span in spans[:ahead]:
        up(*span)
    for idx, (r0, r1) in enumerate(spans):
        if idx + ahead < len(spans):
            up(*spans[idx + ahead])
        gate = conv(eg_sc, fg_ref, r0, r1)
        act = (gate * _sigmoid(gate) * conv(eu_sc, fu_ref, r0, r1)).astype(BF16)
        out_ref[r0:r1, :] += jnp.dot(act, wd_ref[...], preferred_element_type=F32)

    tail_g = eg_sc[tm:tm + pad, :]
    tail_u = eu_sc[tm:tm + pad, :]
    cg_sc[j] = tail_g
    cu_sc[j] = tail_u
    tg_ref[...] = tail_g
    tu_ref[...] = tail_u

    if apply_final_norm:
        @pl.when(j == pl.num_programs(1) - 1)
        def _():
            for r in range(n_chunks):
                rs = slice(r * chunk, (r + 1) * chunk)
                out_ref[rs, :] = _rms(out_ref[rs, :], fn_ref[...])


def _ffn(x1, o, w_o_all, g, w_up_all, f_conv, w_down_all, prefix_all, final_g, layer, pre_layer, tm, shift,
         tiles_per_seq, apply_final_norm, chunk=256):
    m, d = x1.shape
    dfp = w_down_all.shape[1]
    tn = FF_TILE
    nj = dfp // tn
    pad = prefix_all.shape[1]
    n_tiles = m // tm
    return pl.pallas_call(
        functools.partial(_ffn_kernel, shift=shift, tiles_per_seq=tiles_per_seq, chunk=min(chunk, tm),
                          apply_final_norm=apply_final_norm),
        grid=(n_tiles, nj),
        in_specs=[
            pl.BlockSpec((tm, d), lambda i, j: (i, 0), pipeline_mode=pl.Buffered(1)),
            pl.BlockSpec((tm, o.shape[1]), lambda i, j: (i, 0)),
            _layer_spec(w_o_all.shape[1:], layer),
            _const_spec((1, d)),
            pl.BlockSpec((None, d, tn), lambda i, j: (layer, 0, j)),
            pl.BlockSpec((None, d, tn), lambda i, j: (layer, 0, nj + j)),
            pl.BlockSpec((3, tn), lambda i, j: (0, j)),
            pl.BlockSpec((3, tn), lambda i, j: (0, nj + j)),
            pl.BlockSpec((None, tn, d), lambda i, j: (layer, j, 0)),
            pl.BlockSpec((None, pad, tn), lambda i, j: (pre_layer, 0, j)),
            pl.BlockSpec((None, pad, tn), lambda i, j: (pre_layer, 0, nj + j)),
            _const_spec((1, d)),
        ],
        out_specs=[
            pl.BlockSpec((tm, d), lambda i, j: (i, 0)),
            pl.BlockSpec((pad, tn), lambda i, j: (i, j)),
            pl.BlockSpec((pad, tn), lambda i, j: (i, j)),
        ],
        out_shape=[
            jax.ShapeDtypeStruct((m, d), F32),
            jax.ShapeDtypeStruct((n_tiles * pad, dfp), F32),
            jax.ShapeDtypeStruct((n_tiles * pad, dfp), F32),
        ],
        scratch_shapes=[
            pltpu.VMEM((tm, d), BF16),
            pltpu.VMEM((pad + tm, tn), F32),
            pltpu.VMEM((pad + tm, tn), F32),
            pltpu.VMEM((nj, pad, tn), F32),
            pltpu.VMEM((nj, pad, tn), F32),
        ],
        compiler_params=_params(2),
        name="ffn",
    )(x1, o, w_o_all, g, w_up_all, w_up_all, f_conv, f_conv, w_down_all, prefix_all, prefix_all, final_g)


def _cast_up_kernel(w_ref, o_ref):
    d_ff = w_ref.shape[1] // 2
    dfp = o_ref.shape[1] // 2
    rows = w_ref.shape[0]
    for half in range(2):
        o_ref[:, half * dfp:half * dfp + d_ff] = w_ref[:, half * d_ff:(half + 1) * d_ff].astype(BF16)
        o_ref[:, half * dfp + d_ff:(half + 1) * dfp] = jnp.zeros((rows, dfp - d_ff), BF16)


def _cast_up(w_up, dfp, tr=128):
    depth, d, two_dff = w_up.shape
    return pl.pallas_call(
        _cast_up_kernel,
        grid=(depth, d // tr),
        in_specs=[pl.BlockSpec((None, tr, two_dff), lambda l, i: (l, i, 0))],
        out_specs=pl.BlockSpec((None, tr, 2 * dfp), lambda l, i: (l, i, 0)),
        out_shape=jax.ShapeDtypeStruct((depth, d, 2 * dfp), BF16),
        compiler_params=_params(2),
        name="cast_up",
    )(w_up)


def _cast_down_kernel(w_ref, rem_ref, o_ref):
    i = pl.program_id(1)
    last = pl.num_programs(1) - 1
    rem = rem_ref.shape[0]

    @pl.when(i < last)
    def _():
        o_ref[...] = w_ref[...].astype(BF16)

    @pl.when(i == last)
    def _():
        o_ref[0:rem, :] = rem_ref[...].astype(BF16)
        o_ref[rem:, :] = jnp.zeros((o_ref.shape[0] - rem, o_ref.shape[1]), BF16)


def _cast_down(w_down, dfp, tr=256):
    depth, d_ff, d = w_down.shape
    n_full = d_ff // tr
    rem = d_ff - n_full * tr
    assert 0 < rem < tr and d_ff % rem == 0 and dfp == (n_full + 1) * tr
    return pl.pallas_call(
        _cast_down_kernel,
        grid=(depth, n_full + 1),
        in_specs=[
            pl.BlockSpec((None, tr, d), lambda l, i: (l, jnp.minimum(i, n_full - 1), 0)),
            pl.BlockSpec((None, rem, d), lambda l, i: (l, d_ff // rem - 1, 0)),
        ],
        out_specs=pl.BlockSpec((None, tr, d), lambda l, i: (l, i, 0)),
        out_shape=jax.ShapeDtypeStruct((depth, dfp, d), BF16),
        compiler_params=_params(2),
        name="cast_down",
    )(w_down, w_down)


def _pad_cols(a, width):
    return jnp.pad(a, [(0, 0)] * (a.ndim - 1) + [(0, width - a.shape[-1])])


def _layer_params(l, ts, d_ff, dfp, norm_mix, w_s, b_s, w_pool, pool_scale, b_gate, mlstm_norm, w_dconv, norm_cross,
                  norm_ffn, w_fconv):
    ws_l, bs_l, fc = w_s[l], b_s[l], w_fconv[l]
    return {
        "norm_mix": norm_mix[l][None, :],
        "w_s": ws_l,
        "b_s_col": bs_l[:, :, None],
        "ws_tab": jnp.repeat(jnp.transpose(ws_l[:, :ts, :ts], (1, 2, 0)).reshape(ts * ts, N_HEADS), HEAD_DIM, axis=1),
        "bs_tab": jnp.repeat(jnp.transpose(bs_l[:, :ts], (1, 0)), HEAD_DIM, axis=1),
        "w_pool": w_pool[l],
        "pool_scale": pool_scale[l][None, :],
        "mlstm_norm": mlstm_norm[l][None, :],
        "bg_row": _pad_cols(b_gate[l][None, :], LANES),
        "bg_col": b_gate[l][:, None],
        "w_dconv": w_dconv[l],
        "norm_cross": norm_cross[l][None, :],
        "norm_ffn": norm_ffn[l][None, :],
        "f_conv": jnp.concatenate([_pad_cols(fc[:, :d_ff], dfp), _pad_cols(fc[:, d_ff:], dfp)], axis=1),
    }


def kernel(x_prompt, x_sample, mem_prompt, state_pool, state_conv, state_mlstm_c, state_mlstm_n, state_mlstm_m,
           state_ffn_conv, cache_mem_k, cache_mem_v, norm_mix, w_in, w_s, b_s, w_pool, pool_scale, b_gate,
           mlstm_norm, w_dconv, w_out, mem_norm, w_kv, norm_cross, w_q, w_o, norm_ffn, w_up, w_fconv, w_down,
           final_norm):
    depth = w_in.shape[0]
    nbp, seq, d = x_prompt.shape
    nbs, ts, _ = x_sample.shape
    gw = GROUP_WIDTH
    mem_w = w_q.shape[2]
    n_main = N_SLICES * gw
    d_ff = w_down.shape[1]
    dfp = -(-d_ff // FF_TILE) * FF_TILE

    w_gate_b = _pad_cols(w_in[:, :, n_main:], LANES).astype(BF16)
    w_in_b = w_in.astype(BF16)
    w_out_b = w_out.astype(BF16)
    w_q_b = w_q.astype(BF16)
    w_o_b = w_o.astype(BF16)
    w_kv_b = w_kv.astype(BF16)
    w_up_b = _cast_up(w_up, dfp)
    w_down_b = _cast_down(w_down, dfp)

    xp = x_prompt.reshape(nbp * seq, d)
    xs = jnp.transpose(x_sample, (1, 0, 2)).reshape(ts * nbs, d)
    mem2 = mem_prompt.reshape(nbp * N_MEM, d)
    kc_all = cache_mem_k.reshape(depth, nbs, N_MEM * N_HEADS, HEAD_DIM)
    vc_all = cache_mem_v.reshape(depth, nbs, N_MEM * N_HEADS, HEAD_DIM)
    n_all = state_mlstm_n.reshape(depth, nbs, gw)
    m_all = _pad_cols(state_mlstm_m, LANES)
    pool_pre_all = jnp.transpose(state_pool, (0, 2, 1, 3))
    conv_pre_all = jnp.transpose(state_conv, (0, 2, 1, 3))
    ffn_pre_t = jnp.transpose(state_ffn_conv, (0, 2, 1, 3))
    ffn_pre_all = jnp.concatenate([_pad_cols(ffn_pre_t[..., :d_ff], dfp), _pad_cols(ffn_pre_t[..., d_ff:], dfp)],
                                  axis=-1).reshape(depth, 2 * nbs, 2 * dfp)
    pad_p = SUBLANES
    zeros_pre = jnp.zeros((1, pad_p, 2 * dfp), F32)
    final_g = final_norm[None, :]
    s_c = jnp.zeros(state_mlstm_c.shape, F32)

    tm_p = min(1024, seq)
    tm_s = ts * nbs
    outs = {k: [] for k in ("p_pool", "p_conv", "p_c", "p_n", "p_m", "p_tg", "p_tu", "p_kv",
                            "s_v", "s_bx", "s_conv", "s_n", "s_m", "s_tg", "s_tu")}
    for l in range(depth):
        last = l == depth - 1
        lw = _layer_params(l, ts, d_ff, dfp, norm_mix, w_s, b_s, w_pool, pool_scale, b_gate, mlstm_norm, w_dconv,
                           norm_cross, norm_ffn, w_fconv)

        kv = _norm_mm(mem2, mem_norm[l][None, :], w_kv_b, l, tm=mem2.shape[0], tn=512)
        outs["p_kv"].append(kv)
        mk = kv[:, :mem_w].reshape(nbp, N_MEM, mem_w)
        mv = kv[:, mem_w:].reshape(nbp, N_MEM, mem_w)

        z, gates = _proj_in(xp, lw["norm_mix"], w_in_b, w_gate_b, l, n_main, tm=tm_p)
        gates_t = jnp.transpose(gates[:, :SUBLANES])
        x1, q, conv_st, c1, n1, m1 = _mix_out(z, gates, gates_t, lw, xp, w_out_b, w_q_b, l, nbp, seq)
        outs["p_pool"].append(z.reshape(nbp, seq, n_main)[:, seq - POOL_STATE:, 2 * gw:3 * gw])
        outs["p_conv"].append(conv_st)
        outs["p_c"].append(c1)
        outs["p_n"].append(n1[:, :N_HEADS, :])
        outs["p_m"].append(m1[:, :N_HEADS, 0])
        o = _attn_prompt(q, mk, mv, nbp, seq)
        xp, tail_g, tail_u = _ffn(x1, o, w_o_b, lw["norm_ffn"], w_up_b, lw["f_conv"], w_down_b, zeros_pre, final_g,
                                  l, 0, tm=tm_p, shift=1, tiles_per_seq=seq // tm_p, apply_final_norm=last,
                                  chunk=512)
        outs["p_tg"].append(tail_g)
        outs["p_tu"].append(tail_u)

        z, gates = _proj_in(xs, lw["norm_mix"], w_in_b, w_gate_b, l, n_main, tm=tm_s)
        z3 = z.reshape(ts, nbs, n_main)
        y3, v_rows, conv_st, s_c, n1, m1 = _mix_sample(
            z3, gates.reshape(ts, nbs, LANES), lw, pool_pre_all[l], conv_pre_all[l], state_mlstm_c, n_all, m_all,
            s_c, l)
        outs["s_v"].append(v_rows)
        outs["s_bx"].append(z3[:, :, 2 * gw:3 * gw])
        outs["s_conv"].append(conv_st)
        outs["s_n"].append(n1)
        outs["s_m"].append(m1)
        x1, q = _out_q(xs, y3.reshape(ts * nbs, 4 * gw), w_out_b, lw["norm_cross"], w_q_b, l)
        q_seq = jnp.transpose(q.reshape(ts, nbs, N_HEADS, HEAD_DIM), (1, 0, 2, 3)).reshape(nbs, ts * N_HEADS, HEAD_DIM)
        o_seq = _attn_sample(q_seq, kc_all, vc_all, l)
        o = jnp.transpose(o_seq.reshape(nbs, ts, N_HEADS, HEAD_DIM), (1, 0, 2, 3)).reshape(ts * nbs, mem_w)
        xs, tail_g, tail_u = _ffn(x1, o, w_o_b, lw["norm_ffn"], w_up_b, lw["f_conv"], w_down_b, ffn_pre_all, final_g,
                                  l, l, tm=tm_s, shift=nbs, tiles_per_seq=1, apply_final_norm=last)
        outs["s_tg"].append(tail_g)
        outs["s_tu"].append(tail_u)

    y_prompt = xp.reshape(nbp, seq, d)
    y_sample = jnp.transpose(xs.reshape(ts, nbs, d), (1, 0, 2))

    st = {k: jnp.stack(v) for k, v in outs.items()}
    tps = seq // tm_p

    def prompt_tail(t):
        return t.reshape(depth, nbp, tps, pad_p, dfp)[:, :, -1, pad_p - 2:, :d_ff]

    p_ffn = jnp.concatenate([prompt_tail(st["p_tg"]), prompt_tail(st["p_tu"])], axis=-1)
    p_kv = st["p_kv"].reshape(depth, nbp, N_MEM, 2, N_HEADS, HEAD_DIM)

    def sample_tail(t):
        return jnp.transpose(t.reshape(depth, 2, nbs, dfp)[..., :d_ff], (0, 2, 1, 3))

    s_ffn = jnp.concatenate([sample_tail(st["s_tg"]), sample_tail(st["s_tu"])], axis=-1)
    s_pool = jnp.concatenate([state_pool, jnp.transpose(st["s_bx"], (0, 2, 1, 3))], axis=2)[:, :, -POOL_STATE:]
    return (y_prompt, y_sample,
            st["p_pool"], st["p_conv"], st["p_c"], st["p_n"], st["p_m"], p_ffn, p_kv[:, :, :, 0], p_kv[:, :, :, 1],
            jnp.transpose(st["s_v"], (0, 2, 1, 3)), s_pool, jnp.transpose(st["s_conv"], (0, 2, 1, 3)),
            s_c, st["s_n"].reshape(depth, nbs, N_HEADS, HEAD_DIM), st["s_m"][:, :, :N_HEADS], s_ffn)
```

```python
import functools

import jax
import jax.numpy as jnp
from jax import lax
from jax.experimental import pallas as pl
from jax.experimental.pallas import tpu as pltpu

F32 = jnp.float32
BF16 = jnp.bfloat16
EPS = 1e-6

LANES = 128
SUBLANES = 8
GROUP_WIDTH = 512
HEAD_DIM = 128
N_HEADS = GROUP_WIDTH // HEAD_DIM
N_SLICES = 10
POOL_WINDOWS = (2, 4, 8, 16)
POOL_STATE = max(POOL_WINDOWS) - 1
N_MEM = 256
FF_TILE = 512
VMEM_LIMIT = 60 * 1024 * 1024
HIGHEST = lax.Precision.HIGHEST

_NT = (((1,), (1,)), ((), ()))


def _params(n_axes, vmem=VMEM_LIMIT):
    return pltpu.CompilerParams(dimension_semantics=("arbitrary",) * n_axes, vmem_limit_bytes=vmem)


def _rms(x, g):
    return x * lax.rsqrt(jnp.mean(x * x, axis=-1, keepdims=True) + EPS) * g


def _gelu(x):
    return 0.5 * x * (1.0 + jnp.tanh(0.7978845608028654 * (x + 0.044715 * (x * x * x))))


def _sigmoid(x):
    return 1.0 / (1.0 + jnp.exp(-x))


def _log_sigmoid(x):
    return jnp.minimum(x, 0.0) - jnp.log(1.0 + jnp.exp(-jnp.abs(x)))


def _lane_col(x, lane):
    idx = lax.broadcasted_iota(jnp.int32, x.shape, 1)
    return jnp.sum(jnp.where(idx == lane, x, 0.0), axis=-1, keepdims=True)


def _const_spec(shape):
    nd = len(shape)
    return pl.BlockSpec(shape, lambda *_: (0,) * nd, pipeline_mode=pl.Buffered(1))


def _proj_in_kernel(x_ref, g_ref, w_ref, wg_ref, z_ref, gate_ref, hn_sc):
    @pl.when(pl.program_id(1) == 0)
    def _():
        hn_sc[...] = _rms(x_ref[...], g_ref[...]).astype(BF16)
        gate_ref[...] = jnp.dot(hn_sc[...], wg_ref[...], preferred_element_type=F32)

    z_ref[...] = jnp.dot(hn_sc[...], w_ref[...], preferred_element_type=F32)


def _proj_in(x, g, w_all, wg_all, layer, n, tm, tn=1024):
    m, d = x.shape
    return pl.pallas_call(
        _proj_in_kernel,
        grid=(m // tm, n // tn),
        in_specs=[
            pl.BlockSpec((tm, d), lambda i, j: (i, 0)),
            _const_spec((1, d)),
            pl.BlockSpec((None, d, tn), lambda i, j: (layer, 0, j)),
            pl.BlockSpec((None, d, LANES), lambda i, j: (layer, 0, 0), pipeline_mode=pl.Buffered(1)),
        ],
        out_specs=[
            pl.BlockSpec((tm, tn), lambda i, j: (i, j)),
            pl.BlockSpec((tm, LANES), lambda i, j: (i, 0)),
        ],
        out_shape=[jax.ShapeDtypeStruct((m, n), F32), jax.ShapeDtypeStruct((m, LANES), F32)],
        scratch_shapes=[pltpu.VMEM((tm, d), BF16)],
        compiler_params=_params(2),
        name="proj_in",
    )(x, g, w_all, wg_all)


def _norm_mm_kernel(x_ref, g_ref, w_ref, o_ref, hn_sc):
    @pl.when(pl.program_id(1) == 0)
    def _():
        hn_sc[...] = _rms(x_ref[...], g_ref[...]).astype(BF16)

    o_ref[...] = jnp.dot(hn_sc[...], w_ref[...], preferred_element_type=F32)


def _norm_mm(x, g, w_all, layer, tm, tn):
    m, d = x.shape
    n = w_all.shape[2]
    return pl.pallas_call(
        _norm_mm_kernel,
        grid=(m // tm, n // tn),
        in_specs=[
            pl.BlockSpec((tm, d), lambda i, j: (i, 0)),
            _const_spec((1, d)),
            pl.BlockSpec((None, d, tn), lambda i, j: (layer, 0, j)),
        ],
        out_specs=pl.BlockSpec((tm, tn), lambda i, j: (i, j)),
        out_shape=jax.ShapeDtypeStruct((m, n), F32),
        scratch_shapes=[pltpu.VMEM((tm, d), BF16)],
        compiler_params=_params(2),
        name="norm_mm",
    )(x, g, w_all)


def _mix_out_kernel(au_ref, av_ref, bx_ref, cq_ref, ck_ref, cv_ref, co_ref, db_ref, dc_ref, dx_ref,
                    gc_ref, gr_ref, ws_ref, bs_ref, wp_ref, ps_ref, mn_ref, bgr_ref, bgc_ref, wd_ref,
                    x_ref, wo_ref, nc_ref, wq_ref,
                    x1_ref, q_ref, cs_ref, c_ref, n_ref, m_ref,
                    c_sc, n_sc, m_sc, xx_sc, xd_sc, y_ref, y_prev, *, tiles_per_seq):
    g = pl.program_id(0)
    n_steps = pl.num_programs(0)
    t = jnp.minimum(g, n_steps - 2) % tiles_per_seq
    rows = x1_ref.shape[0]

    @pl.when(g == 0)
    def _():
        y_prev[...] = jnp.zeros(y_prev.shape, y_prev.dtype)
    gw = GROUP_WIDTH
    pool_pad = 2 * SUBLANES
    conv_pad = SUBLANES

    @pl.when(t == 0)
    def _():
        c_sc[...] = jnp.zeros_like(c_sc)
        n_sc[...] = jnp.zeros_like(n_sc)
        m_sc[...] = jnp.zeros_like(m_sc)
        xx_sc[0:pool_pad, :] = jnp.zeros((pool_pad, gw), F32)
        xd_sc[0:conv_pad, :] = jnp.zeros((conv_pad, gw), F32)

    col_chunk = x_ref.shape[1] // 2

    def out_proj(c):
        cs = slice(c * col_chunk, (c + 1) * col_chunk)
        x1_ref[:, cs] = x_ref[:, cs] + jnp.dot(y_prev[...], wo_ref[:, cs], preferred_element_type=F32)

    def q_proj():
        hn = _rms(x1_ref[...], nc_ref[...]).astype(BF16)
        q_ref[...] = jnp.dot(hn, wq_ref[...], preferred_element_type=F32)

    def mixer_a():
        ri = lax.broadcasted_iota(jnp.int32, (HEAD_DIM, HEAD_DIM), 0)
        ci = lax.broadcasted_iota(jnp.int32, (HEAD_DIM, HEAD_DIM), 1)
        for h in range(N_HEADS):
            hs = slice(h * HEAD_DIM, (h + 1) * HEAD_DIM)
            w_tril = jnp.where(ri >= ci, ws_ref[h], 0.0).astype(BF16)
            for c in range(rows // HEAD_DIM):
                rs = slice(c * HEAD_DIM, (c + 1) * HEAD_DIM)
                v = _gelu(av_ref[rs, hs]).astype(BF16)
                mixed = jnp.dot(w_tril, v, preferred_element_type=F32) + bs_ref[h]
                y_ref[rs, hs] = (_gelu(au_ref[rs, hs]) * mixed).astype(y_ref.dtype)

    def mixer_b():
        xx_sc[pool_pad:pool_pad + rows, :] = bx_ref[...]
        avail = (t * rows + 1 + lax.broadcasted_iota(jnp.int32, (rows, 1), 0)).astype(F32)
        for g, win in enumerate(POOL_WINDOWS):
            gs = slice(g * HEAD_DIM, (g + 1) * HEAD_DIM)
            acc = bx_ref[:, gs]
            for k in range(1, win):
                acc = acc + xx_sc[pool_pad - k:pool_pad - k + rows, gs]
            pooled = acc / jnp.minimum(avail, float(win)) - bx_ref[:, gs]
            yb = jnp.dot(pooled.astype(BF16), wp_ref[g].astype(BF16), preferred_element_type=F32) * ps_ref[:, gs]
            y_ref[:, gw + g * HEAD_DIM:gw + (g + 1) * HEAD_DIM] = yb.astype(y_ref.dtype)
        xx_sc[0:pool_pad, :] = xx_sc[rows:rows + pool_pad, :]

    def mixer_d():
        for g in range(gw // LANES):
            gs = slice(g * LANES, (g + 1) * LANES)
            dcx = dc_ref[:, gs] * dx_ref[:, gs]
            xd_sc[conv_pad:conv_pad + rows, gs] = dcx
            conv = (wd_ref[0:1, gs] * xd_sc[conv_pad - 2:conv_pad - 2 + rows, gs]
                    + wd_ref[1:2, gs] * xd_sc[conv_pad - 1:conv_pad - 1 + rows, gs]
                    + wd_ref[2:3, gs] * dcx)
            y_ref[:, 3 * gw + g * LANES:3 * gw + (g + 1) * LANES] = (db_ref[:, gs] * conv).astype(y_ref.dtype)
        cs_ref[...] = xd_sc[conv_pad + rows - 2:conv_pad + rows, :]
        xd_sc[0:conv_pad, :] = xd_sc[rows:rows + conv_pad, :]

    gc = gc_ref[...] + bgr_ref[...]
    gr = gr_ref[...] + bgc_ref[...]
    rr = lax.broadcasted_iota(jnp.int32, (rows, rows), 0)
    cc = lax.broadcasted_iota(jnp.int32, (rows, rows), 1)
    causal = rr >= cc
    b_cols = jnp.dot(causal.astype(F32), _log_sigmoid(gc), precision=HIGHEST, preferred_element_type=F32)
    b_rows = jnp.dot(_log_sigmoid(gr), (rr <= cc).astype(F32), precision=HIGHEST, preferred_element_type=F32)
    scale = HEAD_DIM ** -0.5

    def mixer_c(h):
        hs = slice(h * HEAD_DIM, (h + 1) * HEAD_DIM)
        bc = _lane_col(b_cols, N_HEADS + h)
        ic = _lane_col(gc, h)
        br = b_rows[N_HEADS + h:N_HEADS + h + 1, :]
        ir = gr[h:h + 1, :]
        m_prev = m_sc[h:h + 1, 0:1]
        log_d = jnp.where(causal, bc - br + ir, -jnp.inf)
        log_inter = bc + m_prev
        m_t = jnp.maximum(log_inter, jnp.max(log_d, axis=-1, keepdims=True))
        dmat = jnp.exp(log_d - m_t)
        inter = jnp.exp(log_inter - m_t)
        qf = cq_ref[:, hs]
        kf = ck_ref[:, hs] * scale
        vf = cv_ref[:, hs]
        q = qf.astype(BF16)
        k = kf.astype(BF16)
        s = lax.dot_general(q, k, _NT, preferred_element_type=F32) * dmat
        c_prev = c_sc[h]
        n_prev = n_sc[h:h + 1, :]
        num = (jnp.dot(s.astype(BF16), vf.astype(BF16), preferred_element_type=F32)
               + inter * lax.dot_general(q, c_prev.astype(BF16), _NT, preferred_element_type=F32))
        den = jnp.sum(s, axis=-1, keepdims=True) + inter * jnp.sum(qf * n_prev, axis=-1, keepdims=True)
        hout = num / jnp.maximum(jnp.abs(den), jnp.exp(-m_t))
        hout = hout * lax.rsqrt(jnp.mean(hout * hout, axis=-1, keepdims=True) + EPS)
        yc = _sigmoid(co_ref[:, hs]) * hout * mn_ref[:, hs]
        y_ref[:, 2 * gw + h * HEAD_DIM:2 * gw + (h + 1) * HEAD_DIM] = yc.astype(y_ref.dtype)
        m_new = m_t[rows - 1:rows, :]
        b_last = bc[rows - 1:rows, :]
        decay = jnp.exp(b_last + m_prev - m_new)
        wts = jnp.exp(b_last - bc + ic - m_new)
        wv_t = jnp.transpose(wts * vf).astype(BF16)
        c_sc[h] = decay * c_prev + jnp.dot(wv_t, k, preferred_element_type=F32)
        n_sc[h:h + 1, :] = decay * n_prev + jnp.sum(wts * kf, axis=0, keepdims=True)
        m_sc[h:h + 1, :] = jnp.broadcast_to(m_new, (1, LANES))

    mixer_c(0)
    out_proj(0)
    mixer_a()
    mixer_c(1)
    out_proj(1)
    mixer_b()
    mixer_c(2)
    q_proj()
    mixer_d()
    mixer_c(3)

    y_prev[...] = y_ref[...]

    @pl.when(jnp.logical_and(t == tiles_per_seq - 1, g < n_steps - 1))
    def _():
        c_ref[...] = c_sc[...]
        n_ref[...] = n_sc[...]
        m_ref[...] = m_sc[...]


def _mix_out(z, gates, gates_t, lw, x, w_out_all, w_q_all, layer, n_seq, seq, rows=256):
    gw = GROUP_WIDTH
    d = x.shape[1]
    nq = w_q_all.shape[2]
    tps = seq // rows
    n_tiles = n_seq * tps
    last = n_tiles - 1

    def cur(g):
        return jnp.minimum(g, last)

    def prev(g):
        return jnp.maximum(g - 1, 0)

    zspecs = [pl.BlockSpec((rows, gw), functools.partial(lambda g, j: (cur(g), j), j=j)) for j in range(N_SLICES)]
    in_specs = zspecs + [
        pl.BlockSpec((rows, LANES), lambda g: (cur(g), 0)),
        pl.BlockSpec((SUBLANES, rows), lambda g: (0, cur(g))),
        _const_spec((N_HEADS, HEAD_DIM, HEAD_DIM)),
        _const_spec((N_HEADS, HEAD_DIM, 1)),
        _const_spec((N_HEADS, HEAD_DIM, HEAD_DIM)),
        _const_spec((1, gw)),
        _const_spec((1, gw)),
        _const_spec((1, LANES)),
        _const_spec((SUBLANES, 1)),
        _const_spec((3, gw)),
        pl.BlockSpec((rows, d), lambda g: (prev(g), 0)),
        _layer_spec(w_out_all.shape[1:], layer),
        _const_spec((1, d)),
        _layer_spec(w_q_all.shape[1:], layer),
    ]
    out_shape = [
        jax.ShapeDtypeStruct((n_seq * seq, d), F32),
        jax.ShapeDtypeStruct((n_seq * seq, nq), F32),
        jax.ShapeDtypeStruct((n_seq, 2, gw), F32),
        jax.ShapeDtypeStruct((n_seq, N_HEADS, HEAD_DIM, HEAD_DIM), F32),
        jax.ShapeDtypeStruct((n_seq, SUBLANES, HEAD_DIM), F32),
        jax.ShapeDtypeStruct((n_seq, SUBLANES, LANES), F32),
    ]
    out_specs = [
        pl.BlockSpec((rows, d), lambda g: (prev(g), 0)),
        pl.BlockSpec((rows, nq), lambda g: (prev(g), 0)),
        pl.BlockSpec((None, 2, gw), lambda g: (cur(g) // tps, 0, 0)),
        pl.BlockSpec((None, N_HEADS, HEAD_DIM, HEAD_DIM), lambda g: (cur(g) // tps, 0, 0, 0)),
        pl.BlockSpec((None, SUBLANES, HEAD_DIM), lambda g: (cur(g) // tps, 0, 0)),
        pl.BlockSpec((None, SUBLANES, LANES), lambda g: (cur(g) // tps, 0, 0)),
    ]
    scratch = [
        pltpu.VMEM((N_HEADS, HEAD_DIM, HEAD_DIM), F32),
        pltpu.VMEM((SUBLANES, HEAD_DIM), F32),
        pltpu.VMEM((SUBLANES, LANES), F32),
        pltpu.VMEM((2 * SUBLANES + rows, gw), F32),
        pltpu.VMEM((SUBLANES + rows, gw), F32),
        pltpu.VMEM((rows, 4 * gw), BF16),
        pltpu.VMEM((rows, 4 * gw), BF16),
    ]
    return pl.pallas_call(
        functools.partial(_mix_out_kernel, tiles_per_seq=tps),
        grid=(n_tiles + 1,),
        in_specs=in_specs,
        out_specs=out_specs,
        out_shape=out_shape,
        scratch_shapes=scratch,
        compiler_params=_params(1),
        name="mix_out",
    )(*([z] * N_SLICES), gates, gates_t, lw["w_s"], lw["b_s_col"], lw["w_pool"], lw["pool_scale"],
      lw["mlstm_norm"], lw["bg_row"], lw["bg_col"], lw["w_dconv"], x, w_out_all, lw["norm_cross"], w_q_all)


def _mix_sample_kernel(au_ref, av_ref, bx_ref, cq_ref, ck_ref, cv_ref, co_ref, db_ref, dc_ref, dx_ref,
                       g_ref, wst_ref, bst_ref, wp_ref, ps_ref, mn_ref, bgr_ref, wd_ref,
                       pp_ref, cp_ref, c0_ref, n0_ref, m0_ref, c_buf_ref,
                       y_ref, vr_ref, cs_ref, c1_ref, n1_ref, m1_ref):
    ts = au_ref.shape[0]
    nb = au_ref.shape[1]
    gw = GROUP_WIDTH

    v = [_gelu(av_ref[t]) for t in range(ts)]
    for t in range(ts):
        vr_ref[t] = v[t]
        mixed = bst_ref[t:t + 1, :]
        for s in range(t + 1):
            mixed = mixed + wst_ref[t * ts + s:t * ts + s + 1, :] * v[s]
        y_ref[t, :, 0:gw] = _gelu(au_ref[t]) * mixed

    bx = [bx_ref[t] for t in range(ts)]
    hist = [pp_ref[r] for r in range(POOL_STATE)] + bx
    pooled = []
    for t in range(ts):
        parts = []
        for g, win in enumerate(POOL_WINDOWS):
            gs = slice(g * HEAD_DIM, (g + 1) * HEAD_DIM)
            acc = hist[POOL_STATE + t][:, gs]
            for k in range(1, win):
                acc = acc + hist[POOL_STATE + t - k][:, gs]
            parts.append(acc / float(win) - bx[t][:, gs])
        pooled.append(parts)
    for g in range(len(POOL_WINDOWS)):
        gs = slice(g * HEAD_DIM, (g + 1) * HEAD_DIM)
        pg = jnp.concatenate([pooled[t][g] for t in range(ts)], axis=0).astype(BF16)
        yb = jnp.dot(pg, wp_ref[g].astype(BF16), preferred_element_type=F32) * ps_ref[:, gs]
        for t in range(ts):
            y_ref[t, :, gw + g * HEAD_DIM:gw + (g + 1) * HEAD_DIM] = yb[t * nb:(t + 1) * nb, :]

    dcx = [dc_ref[t] * dx_ref[t] for t in range(ts)]
    xx = [cp_ref[0], cp_ref[1]] + dcx
    for t in range(ts):
        conv = wd_ref[0:1, :] * xx[t] + wd_ref[1:2, :] * xx[t + 1] + wd_ref[2:3, :] * xx[t + 2]
        y_ref[t, :, 3 * gw:4 * gw] = db_ref[t] * conv
    cs_ref[0] = xx[ts]
    cs_ref[1] = xx[ts + 1]

    gate = [g_ref[t] + bgr_ref[...] for t in range(ts)]
    lf = [pltpu.roll(_log_sigmoid(gate[t]), LANES - N_HEADS, 1) for t in range(ts)]
    m0 = m0_ref[...]
    b = []
    for t in range(ts):
        b.append(lf[t] if t == 0 else b[t - 1] + lf[t])
    log_d = [[b[t] - b[s] + gate[s] for s in range(t + 1)] for t in range(ts)]
    m_t = []
    for t in range(ts):
        mt = b[t] + m0
        for s in range(t + 1):
            mt = jnp.maximum(mt, log_d[t][s])
        m_t.append(mt)
    dmat = [[jnp.exp(log_d[t][s] - m_t[t]) for s in range(t + 1)] for t in range(ts)]
    inter = [jnp.exp(b[t] + m0 - m_t[t]) for t in range(ts)]
    floor = [jnp.exp(-m_t[t]) for t in range(ts)]
    m_new = m_t[ts - 1]
    decay = jnp.exp(b[ts - 1] + m0 - m_new)
    wts = [jnp.exp(b[ts - 1] - b[s] + gate[s] - m_new) for s in range(ts)]
    m1_ref[...] = m_new

    scale = HEAD_DIM ** -0.5
    row_seq = lax.broadcasted_iota(jnp.int32, (ts * nb, HEAD_DIM), 0) % nb
    for h in range(N_HEADS):
        hs = slice(h * HEAD_DIM, (h + 1) * HEAD_DIM)
        q = [cq_ref[t][:, hs] for t in range(ts)]
        k = [ck_ref[t][:, hs] * scale for t in range(ts)]
        vv = [cv_ref[t][:, hs] for t in range(ts)]
        n_prev = n0_ref[:, hs]
        q_blk = jnp.concatenate(q, axis=0).astype(BF16)
        k_blk = jnp.concatenate(k, axis=0).astype(BF16)
        dec_col = _lane_col(decay, h)
        wt_col = [_lane_col(wts[s], h) for s in range(ts)]
        wv_blk = jnp.concatenate([wt_col[s] * vv[s] for s in range(ts)], axis=0)
        qc = jnp.zeros((ts * nb, HEAD_DIM), F32)
        for i in range(nb):
            c_prev = c0_ref[i, h]
            r = lax.dot_general(q_blk, c_prev.astype(BF16), _NT, preferred_element_type=F32)
            qc = jnp.where(row_seq == i, r, qc)
            wv_i = jnp.where(row_seq == i, wv_blk, 0.0).astype(BF16)
            upd = lax.dot_general(wv_i, k_blk, (((0,), (0,)), ((), ())), preferred_element_type=F32)
            c1_ref[i, h] = dec_col[i:i + 1, :] * c_prev + upd
        n_new = dec_col * n_prev
        for s in range(ts):
            n_new = n_new + wt_col[s] * k[s]
        n1_ref[:, hs] = n_new
        for t in range(ts):
            int_col = _lane_col(inter[t], h)
            num = int_col * qc[t * nb:(t + 1) * nb, :]
            den = int_col * jnp.sum(q[t] * n_prev, axis=-1, keepdims=True)
            for s in range(t + 1):
                s_ts = jnp.sum(q[t] * k[s], axis=-1, keepdims=True) * _lane_col(dmat[t][s], h)
                num = num + s_ts * vv[s]
                den = den + s_ts
            hout = num / jnp.maximum(jnp.abs(den), _lane_col(floor[t], h))
            hout = hout * lax.rsqrt(jnp.mean(hout * hout, axis=-1, keepdims=True) + EPS)
            y_ref[t, :, 2 * gw + h * HEAD_DIM:2 * gw + (h + 1) * HEAD_DIM] = (
                _sigmoid(co_ref[t][:, hs]) * hout * mn_ref[:, hs])


def _mix_sample(z3, gates3, lw, pool_prefix, conv_prefix, c_all, n_all, m_all, c_new_all, layer, nb=8):
    ts, n_seq, _ = z3.shape
    n_in = N_SLICES + 13
    gw = GROUP_WIDTH
    zspecs = [pl.BlockSpec((ts, nb, gw), functools.partial(lambda i, j: (0, i, j), j=j)) for j in range(N_SLICES)]
    in_specs = zspecs + [
        pl.BlockSpec((ts, nb, LANES), lambda i: (0, i, 0)),
        _const_spec((ts * ts, gw)),
        _const_spec((ts, gw)),
        _const_spec((N_HEADS, HEAD_DIM, HEAD_DIM)),
        _const_spec((1, gw)),
        _const_spec((1, gw)),
        _const_spec((1, LANES)),
        _const_spec((3, gw)),
        pl.BlockSpec((POOL_STATE, nb, gw), lambda i: (0, i, 0)),
        pl.BlockSpec((2, nb, gw), lambda i: (0, i, 0)),
        pl.BlockSpec((None, nb, N_HEADS, HEAD_DIM, HEAD_DIM), lambda i: (layer, i, 0, 0, 0)),
        pl.BlockSpec((None, nb, gw), lambda i: (layer, i, 0)),
        pl.BlockSpec((None, nb, LANES), lambda i: (layer, i, 0)),
        pl.BlockSpec(memory_space=pl.ANY),
    ]
    assert len(in_specs) == n_in + 1
    out_shape = [
        jax.ShapeDtypeStruct((ts, n_seq, 4 * gw), F32),
        jax.ShapeDtypeStruct((ts, n_seq, gw), F32),
        jax.ShapeDtypeStruct((2, n_seq, gw), F32),
        jax.ShapeDtypeStruct(c_new_all.shape, F32),
        jax.ShapeDtypeStruct((n_seq, gw), F32),
        jax.ShapeDtypeStruct((n_seq, LANES), F32),
    ]
    out_specs = [
        pl.BlockSpec((ts, nb, 4 * gw), lambda i: (0, i, 0)),
        pl.BlockSpec((ts, nb, gw), lambda i: (0, i, 0)),
        pl.BlockSpec((2, nb, gw), lambda i: (0, i, 0)),
        pl.BlockSpec((None, nb, N_HEADS, HEAD_DIM, HEAD_DIM), lambda i: (layer, i, 0, 0, 0)),
        pl.BlockSpec((nb, gw), lambda i: (i, 0)),
        pl.BlockSpec((nb, LANES), lambda i: (i, 0)),
    ]
    return pl.pallas_call(
        _mix_sample_kernel,
        grid=(n_seq // nb,),
        in_specs=in_specs,
        out_specs=out_specs,
        out_shape=out_shape,
        input_output_aliases={n_in: 3},
        compiler_params=_params(1),
        name="mix_sample",
    )(*([z3] * N_SLICES), gates3, lw["ws_tab"], lw["bs_tab"], lw["w_pool"], lw["pool_scale"],
      lw["mlstm_norm"], lw["bg_row"], lw["w_dconv"], pool_prefix, conv_prefix, c_all, n_all, m_all, c_new_all)


def _out_q_kernel(x_ref, y_ref, wo_ref, g_ref, wq_ref, x1_ref, q_ref):
    x1 = x_ref[...] + jnp.dot(y_ref[...].astype(BF16), wo_ref[...], preferred_element_type=F32)
    x1_ref[...] = x1
    q_ref[...] = jnp.dot(_rms(x1, g_ref[...]).astype(BF16), wq_ref[...], preferred_element_type=F32)


def _layer_spec(shape, layer):
    nd = len(shape)
    return pl.BlockSpec((None,) + tuple(shape), lambda *_: (layer,) + (0,) * nd, pipeline_mode=pl.Buffered(1))


def _out_q(x, y, w_out_all, g, w_q_all, layer, tm=512):
    m, d = x.shape
    tm = min(tm, m)
    nq = w_q_all.shape[2]
    return pl.pallas_call(
        _out_q_kernel,
        grid=(m // tm,),
        in_specs=[
            pl.BlockSpec((tm, d), lambda i: (i, 0)),
            pl.BlockSpec((tm, y.shape[1]), lambda i: (i, 0)),
            _layer_spec(w_out_all.shape[1:], layer),
            _const_spec((1, d)),
            _layer_spec(w_q_all.shape[1:], layer),
        ],
        out_specs=[pl.BlockSpec((tm, d), lambda i: (i, 0)), pl.BlockSpec((tm, nq), lambda i: (i, 0))],
        out_shape=[jax.ShapeDtypeStruct((m, d), F32), jax.ShapeDtypeStruct((m, nq), F32)],
        compiler_params=_params(1),
        name="out_q",
    )(x, y, w_out_all, g, w_q_all)


def _softmax_rows(s):
    e = jnp.exp(s - jnp.max(s, axis=-1, keepdims=True))
    return e / jnp.sum(e, axis=-1, keepdims=True)


def _attn_prompt_kernel(q_ref, k_ref, v_ref, o_ref):
    scale = HEAD_DIM ** -0.5
    for h in range(N_HEADS):
        hs = slice(h * HEAD_DIM, (h + 1) * HEAD_DIM)
        q = q_ref[:, hs].astype(BF16)
        s = lax.dot_general(q, k_ref[:, hs].astype(BF16), _NT, preferred_element_type=F32) * scale
        p = _softmax_rows(s).astype(BF16)
        o_ref[:, hs] = jnp.dot(p, v_ref[:, hs].astype(BF16), preferred_element_type=F32)


def _attn_prompt(q, k, v, n_seq, seq, tq=512):
    tps = seq // tq
    w = q.shape[1]
    return pl.pallas_call(
        _attn_prompt_kernel,
        grid=(n_seq, tps),
        in_specs=[
            pl.BlockSpec((tq, w), lambda n, t: (n * tps + t, 0)),
            pl.BlockSpec((None, N_MEM, w), lambda n, t: (n, 0, 0)),
            pl.BlockSpec((None, N_MEM, w), lambda n, t: (n, 0, 0)),
        ],
        out_specs=pl.BlockSpec((tq, w), lambda n, t: (n * tps + t, 0)),
        out_shape=jax.ShapeDtypeStruct(q.shape, F32),
        compiler_params=_params(2),
        name="attn_prompt",
    )(q, k, v)


def _attn_sample_kernel(q_ref, k_ref, v_ref, o_ref):
    nb, rows, _ = q_ref.shape
    n_kv = k_ref.shape[1]
    scale = HEAD_DIM ** -0.5
    row_head = lax.broadcasted_iota(jnp.int32, (rows, n_kv), 0) % N_HEADS
    col_head = lax.broadcasted_iota(jnp.int32, (rows, n_kv), 1) % N_HEADS
    same_head = row_head == col_head
    scores = []
    for i in range(nb):
        s = lax.dot_general(q_ref[i].astype(BF16), k_ref[i].astype(BF16), _NT, preferred_element_type=F32) * scale
        scores.append(jnp.where(same_head, s, -jnp.inf))
    p = _softmax_rows(jnp.concatenate(scores, axis=0)).astype(BF16)
    for i in range(nb):
        o_ref[i] = jnp.dot(p[i * rows:(i + 1) * rows, :], v_ref[i].astype(BF16), preferred_element_type=F32)


def _attn_sample(q, k_all, v_all, layer, nb=8):
    n_seq, rows, hd = q.shape
    n_kv = k_all.shape[2]
    return pl.pallas_call(
        _attn_sample_kernel,
        grid=(n_seq // nb,),
        in_specs=[
            pl.BlockSpec((nb, rows, hd), lambda i: (i, 0, 0)),
            pl.BlockSpec((None, nb, n_kv, hd), lambda i: (layer, i, 0, 0)),
            pl.BlockSpec((None, nb, n_kv, hd), lambda i: (layer, i, 0, 0)),
        ],
        out_specs=pl.BlockSpec((nb, rows, hd), lambda i: (i, 0, 0)),
        out_shape=jax.ShapeDtypeStruct(q.shape, F32),
        compiler_params=_params(1),
        name="attn_sample",
    )(q, k_all, v_all)


def _ffn_kernel(x1_ref, o_ref, wo_ref, g_ref, wg_ref, wu_ref, fg_ref, fu_ref, wd_ref, pg_ref, pu_ref, fn_ref,
                out_ref, tg_ref, tu_ref, hn_sc, eg_sc, eu_sc, cg_sc, cu_sc, *, shift, tiles_per_seq, chunk,
                apply_final_norm):
    i = pl.program_id(0)
    j = pl.program_id(1)
    tm = out_ref.shape[0]
    pad = pg_ref.shape[0]

    @pl.when(j == 0)
    def _():
        for r in range(tm // chunk):
            rs = slice(r * chunk, (r + 1) * chunk)
            x2 = x1_ref[rs, :] + jnp.dot(o_ref[rs, :].astype(BF16), wo_ref[...], preferred_element_type=F32)
            out_ref[rs, :] = x2
            hn_sc[rs, :] = _rms(x2, g_ref[...]).astype(BF16)

    first = (i % tiles_per_seq) == 0

    @pl.when(first)
    def _():
        eg_sc[0:pad, :] = pg_ref[...]
        eu_sc[0:pad, :] = pu_ref[...]

    @pl.when(jnp.logical_not(first))
    def _():
        eg_sc[0:pad, :] = cg_sc[j]
        eu_sc[0:pad, :] = cu_sc[j]

    def conv(e_sc, f_ref, r0, r1):
        return (f_ref[0:1, :] * e_sc[pad - 2 * shift + r0:pad - 2 * shift + r1, :]
                + f_ref[1:2, :] * e_sc[pad - shift + r0:pad - shift + r1, :]
                + f_ref[2:3, :] * e_sc[pad + r0:pad + r1, :])

    def up(r0, r1):
        hn = hn_sc[r0:r1, :]
        eg_sc[pad + r0:pad + r1, :] = jnp.dot(hn, wg_ref[...], preferred_element_type=F32)
        eu_sc[pad + r0:pad + r1, :] = jnp.dot(hn, wu_ref[...], preferred_element_type=F32)

    n_chunks = tm // chunk
    spans = [(r * chunk, (r + 1) * chunk) for r in range(n_chunks)]
    ahead = 2
    for span in spans[:ahead]:
        up(*span)
    for idx, (r0, r1) in enumerate(spans):
        if idx + ahead < len(spans):
            up(*spans[idx + ahead])
        gate = conv(eg_sc, fg_ref, r0, r1)
        act = (gate * _sigmoid(gate) * conv(eu_sc, fu_ref, r0, r1)).astype(BF16)
        out_ref[r0:r1, :] += jnp.dot(act, wd_ref[...], preferred_element_type=F32)

    tail_g = eg_sc[tm:tm + pad, :]
    tail_u = eu_sc[tm:tm + pad, :]
    cg_sc[j] = tail_g
    cu_sc[j] = tail_u
    tg_ref[...] = tail_g
    tu_ref[...] = tail_u

    if apply_final_norm:
        @pl.when(j == pl.num_programs(1) - 1)
        def _():
            for r in range(n_chunks):
                rs = slice(r * chunk, (r + 1) * chunk)
                out_ref[rs, :] = _rms(out_ref[rs, :], fn_ref[...])


def _ffn(x1, o, w_o_all, g, w_up_all, f_conv, w_down_all, prefix_all, final_g, layer, pre_layer, tm, shift,
         tiles_per_seq, apply_final_norm, chunk=256):
    m, d = x1.shape
    dfp = w_down_all.shape[1]
    tn = FF_TILE
    nj = dfp // tn
    pad = prefix_all.shape[1]
    n_tiles = m // tm
    return pl.pallas_call(
        functools.partial(_ffn_kernel, shift=shift, tiles_per_seq=tiles_per_seq, chunk=min(chunk, tm),
                          apply_final_norm=apply_final_norm),
        grid=(n_tiles, nj),
        in_specs=[
            pl.BlockSpec((tm, d), lambda i, j: (i, 0), pipeline_mode=pl.Buffered(1)),
            pl.BlockSpec((tm, o.shape[1]), lambda i, j: (i, 0)),
            _layer_spec(w_o_all.shape[1:], layer),
            _const_spec((1, d)),
            pl.BlockSpec((None, d, tn), lambda i, j: (layer, 0, j)),
            pl.BlockSpec((None, d, tn), lambda i, j: (layer, 0, nj + j)),
            pl.BlockSpec((3, tn), lambda i, j: (0, j)),
            pl.BlockSpec((3, tn), lambda i, j: (0, nj + j)),
            pl.BlockSpec((None, tn, d), lambda i, j: (layer, j, 0)),
            pl.BlockSpec((None, pad, tn), lambda i, j: (pre_layer, 0, j)),
            pl.BlockSpec((None, pad, tn), lambda i, j: (pre_layer, 0, nj + j)),
            _const_spec((1, d)),
        ],
        out_specs=[
            pl.BlockSpec((tm, d), lambda i, j: (i, 0)),
            pl.BlockSpec((pad, tn), lambda i, j: (i, j)),
            pl.BlockSpec((pad, tn), lambda i, j: (i, j)),
        ],
        out_shape=[
            jax.ShapeDtypeStruct((m, d), F32),
            jax.ShapeDtypeStruct((n_tiles * pad, dfp), F32),
            jax.ShapeDtypeStruct((n_tiles * pad, dfp), F32),
        ],
        scratch_shapes=[
            pltpu.VMEM((tm, d), BF16),
            pltpu.VMEM((pad + tm, tn), F32),
            pltpu.VMEM((pad + tm, tn), F32),
            pltpu.VMEM((nj, pad, tn), F32),
            pltpu.VMEM((nj, pad, tn), F32),
        ],
        compiler_params=_params(2),
        name="ffn",
    )(x1, o, w_o_all, g, w_up_all, w_up_all, f_conv, f_conv, w_down_all, prefix_all, prefix_all, final_g)


def _cast_up_kernel(w_ref, o_ref):
    d_ff = w_ref.shape[1] // 2
    dfp = o_ref.shape[1] // 2
    rows = w_ref.shape[0]
    for half in range(2):
        o_ref[:, half * dfp:half * dfp + d_ff] = w_ref[:, half * d_ff:(half + 1) * d_ff].astype(BF16)
        o_ref[:, half * dfp + d_ff:(half + 1) * dfp] = jnp.zeros((rows, dfp - d_ff), BF16)


def _cast_up(w_up, dfp, tr=128):
    depth, d, two_dff = w_up.shape
    return pl.pallas_call(
        _cast_up_kernel,
        grid=(depth, d // tr),
        in_specs=[pl.BlockSpec((None, tr, two_dff), lambda l, i: (l, i, 0))],
        out_specs=pl.BlockSpec((None, tr, 2 * dfp), lambda l, i: (l, i, 0)),
        out_shape=jax.ShapeDtypeStruct((depth, d, 2 * dfp), BF16),
        compiler_params=_params(2),
        name="cast_up",
    )(w_up)


def _cast_down_kernel(w_ref, rem_ref, o_ref):
    i = pl.program_id(1)
    last = pl.num_programs(1) - 1
    rem = rem_ref.shape[0]

    @pl.when(i < last)
    def _():
        o_ref[...] = w_ref[...].astype(BF16)

    @pl.when(i == last)
    def _():
        o_ref[0:rem, :] = rem_ref[...].astype(BF16)
        o_ref[rem:, :] = jnp.zeros((o_ref.shape[0] - rem, o_ref.shape[1]), BF16)


def _cast_down(w_down, dfp, tr=256):
    depth, d_ff, d = w_down.shape
    n_full = d_ff // tr
    rem = d_ff - n_full * tr
    assert 0 < rem < tr and d_ff % rem == 0 and dfp == (n_full + 1) * tr
    return pl.pallas_call(
        _cast_down_kernel,
        grid=(depth, n_full + 1),
        in_specs=[
            pl.BlockSpec((None, tr, d), lambda l, i: (l, jnp.minimum(i, n_full - 1), 0)),
            pl.BlockSpec((None, rem, d), lambda l, i: (l, d_ff // rem - 1, 0)),
        ],
        out_specs=pl.BlockSpec((None, tr, d), lambda l, i: (l, i, 0)),
        out_shape=jax.ShapeDtypeStruct((depth, dfp, d), BF16),
        compiler_params=_params(2),
        name="cast_down",
    )(w_down, w_down)


def _pad_cols(a, width):
    return jnp.pad(a, [(0, 0)] * (a.ndim - 1) + [(0, width - a.shape[-1])])


def _layer_params(l, ts, d_ff, dfp, norm_mix, w_s, b_s, w_pool, pool_scale, b_gate, mlstm_norm, w_dconv, norm_cross,
                  norm_ffn, w_fconv):
    ws_l, bs_l, fc = w_s[l], b_s[l], w_fconv[l]
    return {
        "norm_mix": norm_mix[l][None, :],
        "w_s": ws_l,
        "b_s_col": bs_l[:, :, None],
        "ws_tab": jnp.repeat(jnp.transpose(ws_l[:, :ts, :ts], (1, 2, 0)).reshape(ts * ts, N_HEADS), HEAD_DIM, axis=1),
        "bs_tab": jnp.repeat(jnp.transpose(bs_l[:, :ts], (1, 0)), HEAD_DIM, axis=1),
        "w_pool": w_pool[l],
        "pool_scale": pool_scale[l][None, :],
        "mlstm_norm": mlstm_norm[l][None, :],
        "bg_row": _pad_cols(b_gate[l][None, :], LANES),
        "bg_col": b_gate[l][:, None],
        "w_dconv": w_dconv[l],
        "norm_cross": norm_cross[l][None, :],
        "norm_ffn": norm_ffn[l][None, :],
        "f_conv": jnp.concatenate([_pad_cols(fc[:, :d_ff], dfp), _pad_cols(fc[:, d_ff:], dfp)], axis=1),
    }


def kernel(x_prompt, x_sample, mem_prompt, state_pool, state_conv, state_mlstm_c, state_mlstm_n, state_mlstm_m,
           state_ffn_conv, cache_mem_k, cache_mem_v, norm_mix, w_in, w_s, b_s, w_pool, pool_scale, b_gate,
           mlstm_norm, w_dconv, w_out, mem_norm, w_kv, norm_cross, w_q, w_o, norm_ffn, w_up, w_fconv, w_down,
           final_norm):
    depth = w_in.shape[0]
    nbp, seq, d = x_prompt.shape
    nbs, ts, _ = x_sample.shape
    gw = GROUP_WIDTH
    mem_w = w_q.shape[2]
    n_main = N_SLICES * gw
    d_ff = w_down.shape[1]
    dfp = -(-d_ff // FF_TILE) * FF_TILE

    w_gate_b = _pad_cols(w_in[:, :, n_main:], LANES).astype(BF16)
    w_in_b = w_in.astype(BF16)
    w_out_b = w_out.astype(BF16)
    w_q_b = w_q.astype(BF16)
    w_o_b = w_o.astype(BF16)
    w_kv_b = w_kv.astype(BF16)
    w_up_b = _cast_up(w_up, dfp)
    w_down_b = _cast_down(w_down, dfp)

    xp = x_prompt.reshape(nbp * seq, d)
    xs = jnp.transpose(x_sample, (1, 0, 2)).reshape(ts * nbs, d)
    mem2 = mem_prompt.reshape(nbp * N_MEM, d)
    kc_all = cache_mem_k.reshape(depth, nbs, N_MEM * N_HEADS, HEAD_DIM)
    vc_all = cache_mem_v.reshape(depth, nbs, N_MEM * N_HEADS, HEAD_DIM)
    n_all = state_mlstm_n.reshape(depth, nbs, gw)
    m_all = _pad_cols(state_mlstm_m, LANES)
    pool_pre_all = jnp.transpose(state_pool, (0, 2, 1, 3))
    conv_pre_all = jnp.transpose(state_conv, (0, 2, 1, 3))
    ffn_pre_t = jnp.transpose(state_ffn_conv, (0, 2, 1, 3))
    ffn_pre_all = jnp.concatenate([_pad_cols(ffn_pre_t[..., :d_ff], dfp), _pad_cols(ffn_pre_t[..., d_ff:], dfp)],
                                  axis=-1).reshape(depth, 2 * nbs, 2 * dfp)
    pad_p = SUBLANES
    zeros_pre = jnp.zeros((1, pad_p, 2 * dfp), F32)
    final_g = final_norm[None, :]
    s_c = jnp.zeros(state_mlstm_c.shape, F32)

    tm_p = min(1024, seq)
    tm_s = ts * nbs
    outs = {k: [] for k in ("p_pool", "p_conv", "p_c", "p_n", "p_m", "p_tg", "p_tu", "p_kv",
                            "s_v", "s_bx", "s_conv", "s_n", "s_m", "s_tg", "s_tu")}
    for l in range(depth):
        last = l == depth - 1
        lw = _layer_params(l, ts, d_ff, dfp, norm_mix, w_s, b_s, w_pool, pool_scale, b_gate, mlstm_norm, w_dconv,
                           norm_cross, norm_ffn, w_fconv)

        kv = _norm_mm(mem2, mem_norm[l][None, :], w_kv_b, l, tm=mem2.shape[0], tn=512)
        outs["p_kv"].append(kv)
        mk = kv[:, :mem_w].reshape(nbp, N_MEM, mem_w)
        mv = kv[:, mem_w:].reshape(nbp, N_MEM, mem_w)

        z, gates = _proj_in(xp, lw["norm_mix"], w_in_b, w_gate_b, l, n_main, tm=tm_p)
        gates_t = jnp.transpose(gates[:, :SUBLANES])
        x1, q, conv_st, c1, n1, m1 = _mix_out(z, gates, gates_t, lw, xp, w_out_b, w_q_b, l, nbp, seq)
        outs["p_pool"].append(z.reshape(nbp, seq, n_main)[:, seq - POOL_STATE:, 2 * gw:3 * gw])
        outs["p_conv"].append(conv_st)
        outs["p_c"].append(c1)
        outs["p_n"].append(n1[:, :N_HEADS, :])
        outs["p_m"].append(m1[:, :N_HEADS, 0])
        o = _attn_prompt(q, mk, mv, nbp, seq)
        xp, tail_g, tail_u = _ffn(x1, o, w_o_b, lw["norm_ffn"], w_up_b, lw["f_conv"], w_down_b, zeros_pre, final_g,
                                  l, 0, tm=tm_p, shift=1, tiles_per_seq=seq // tm_p, apply_final_norm=last,
                                  chunk=512)
        outs["p_tg"].append(tail_g)
        outs["p_tu"].append(tail_u)

        z, gates = _proj_in(xs, lw["norm_mix"], w_in_b, w_gate_b, l, n_main, tm=tm_s)
        z3 = z.reshape(ts, nbs, n_main)
        y3, v_rows, conv_st, s_c, n1, m1 = _mix_sample(
            z3, gates.reshape(ts, nbs, LANES), lw, pool_pre_all[l], conv_pre_all[l], state_mlstm_c, n_all, m_all,
            s_c, l)
        outs["s_v"].append(v_rows)
        outs["s_bx"].append(z3[:, :, 2 * gw:3 * gw])
        outs["s_conv"].append(conv_st)
        outs["s_n"].append(n1)
        outs["s_m"].append(m1)
        x1, q = _out_q(xs, y3.reshape(ts * nbs, 4 * gw), w_out_b, lw["norm_cross"], w_q_b, l)
        q_seq = jnp.transpose(q.reshape(ts, nbs, N_HEADS, HEAD_DIM), (1, 0, 2, 3)).reshape(nbs, ts * N_HEADS, HEAD_DIM)
        o_seq = _attn_sample(q_seq, kc_all, vc_all, l)
        o = jnp.transpose(o_seq.reshape(nbs, ts, N_HEADS, HEAD_DIM), (1, 0, 2, 3)).reshape(ts * nbs, mem_w)
        xs, tail_g, tail_u = _ffn(x1, o, w_o_b, lw["norm_ffn"], w_up_b, lw["f_conv"], w_down_b, ffn_pre_all, final_g,
                                  l, l, tm=tm_s, shift=nbs, tiles_per_seq=1, apply_final_norm=last)
        outs["s_tg"].append(tail_g)
        outs["s_tu"].append(tail_u)

    y_prompt = xp.reshape(nbp, seq, d)
    y_sample = jnp.transpose(xs.reshape(ts, nbs, d), (1, 0, 2))

    st = {k: jnp.stack(v) for k, v in outs.items()}
    tps = seq // tm_p

    def prompt_tail(t):
        return t.reshape(depth, nbp, tps, pad_p, dfp)[:, :, -1, pad_p - 2:, :d_ff]

    p_ffn = jnp.concatenate([prompt_tail(st["p_tg"]), prompt_tail(st["p_tu"])], axis=-1)
    p_kv = st["p_kv"].reshape(depth, nbp, N_MEM, 2, N_HEADS, HEAD_DIM)

    def sample_tail(t):
        return jnp.transpose(t.reshape(depth, 2, nbs, dfp)[..., :d_ff], (0, 2, 1, 3))

    s_ffn = jnp.concatenate([sample_tail(st["s_tg"]), sample_tail(st["s_tu"])], axis=-1)
    s_pool = jnp.concatenate([state_pool, jnp.transpose(st["s_bx"], (0, 2, 1, 3))], axis=2)[:, :, -POOL_STATE:]
    return (y_prompt, y_sample,
            st["p_pool"], st["p_conv"], st["p_c"], st["p_n"], st["p_m"], p_ffn, p_kv[:, :, :, 0], p_kv[:, :, :, 1],
            jnp.transpose(st["s_v"], (0, 2, 1, 3)), s_pool, jnp.transpose(st["s_conv"], (0, 2, 1, 3)),
            s_c, st["s_n"].reshape(depth, nbs, N_HEADS, HEAD_DIM), st["s_m"][:, :, :N_HEADS], s_ffn)
```

```python
import functools

import jax
import jax.numpy as jnp
from jax import lax
from jax.experimental import pallas as pl
from jax.experimental.pallas import tpu as pltpu

F32 = jnp.float32
BF16 = jnp.bfloat16
EPS = 1e-6

LANES = 128
SUBLANES = 8
GROUP_WIDTH = 512
HEAD_DIM = 128
N_HEADS = GROUP_WIDTH // HEAD_DIM
N_SLICES = 10
POOL_WINDOWS = (2, 4, 8, 16)
POOL_STATE = max(POOL_WINDOWS) - 1
N_MEM = 256
FF_TILE = 512
VMEM_LIMIT = 60 * 1024 * 1024
HIGHEST = lax.Precision.HIGHEST

_NT = (((1,), (1,)), ((), ()))


def _params(n_axes, vmem=VMEM_LIMIT):
    return pltpu.CompilerParams(dimension_semantics=("arbitrary",) * n_axes, vmem_limit_bytes=vmem)


def _rms(x, g):
    return x * lax.rsqrt(jnp.mean(x * x, axis=-1, keepdims=True) + EPS) * g


def _gelu(x):
    return 0.5 * x * (1.0 + jnp.tanh(0.7978845608028654 * (x + 0.044715 * (x * x * x))))


def _sigmoid(x):
    return 1.0 / (1.0 + jnp.exp(-x))


def _log_sigmoid(x):
    return jnp.minimum(x, 0.0) - jnp.log(1.0 + jnp.exp(-jnp.abs(x)))


def _lane_col(x, lane):
    idx = lax.broadcasted_iota(jnp.int32, x.shape, 1)
    return jnp.sum(jnp.where(idx == lane, x, 0.0), axis=-1, keepdims=True)


def _const_spec(shape):
    nd = len(shape)
    return pl.BlockSpec(shape, lambda *_: (0,) * nd, pipeline_mode=pl.Buffered(1))


def _proj_in_kernel(x_ref, g_ref, w_ref, wg_ref, z_ref, gate_ref, hn_sc):
    @pl.when(pl.program_id(1) == 0)
    def _():
        hn_sc[...] = _rms(x_ref[...], g_ref[...]).astype(BF16)
        gate_ref[...] = jnp.dot(hn_sc[...], wg_ref[...], preferred_element_type=F32)

    z_ref[...] = jnp.dot(hn_sc[...], w_ref[...], preferred_element_type=F32)


def _proj_in(x, g, w_all, wg_all, layer, n, tm, tn=1024):
    m, d = x.shape
    return pl.pallas_call(
        _proj_in_kernel,
        grid=(m // tm, n // tn),
        in_specs=[
            pl.BlockSpec((tm, d), lambda i, j: (i, 0)),
            _const_spec((1, d)),
            pl.BlockSpec((None, d, tn), lambda i, j: (layer, 0, j)),
            pl.BlockSpec((None, d, LANES), lambda i, j: (layer, 0, 0), pipeline_mode=pl.Buffered(1)),
        ],
        out_specs=[
            pl.BlockSpec((tm, tn), lambda i, j: (i, j)),
            pl.BlockSpec((tm, LANES), lambda i, j: (i, 0)),
        ],
        out_shape=[jax.ShapeDtypeStruct((m, n), F32), jax.ShapeDtypeStruct((m, LANES), F32)],
        scratch_shapes=[pltpu.VMEM((tm, d), BF16)],
        compiler_params=_params(2),
        name="proj_in",
    )(x, g, w_all, wg_all)


def _norm_mm_kernel(x_ref, g_ref, w_ref, o_ref, hn_sc):
    @pl.when(pl.program_id(1) == 0)
    def _():
        hn_sc[...] = _rms(x_ref[...], g_ref[...]).astype(BF16)

    o_ref[...] = jnp.dot(hn_sc[...], w_ref[...], preferred_element_type=F32)


def _norm_mm(x, g, w_all, layer, tm, tn):
    m, d = x.shape
    n = w_all.shape[2]
    return pl.pallas_call(
        _norm_mm_kernel,
        grid=(m // tm, n // tn),
        in_specs=[
            pl.BlockSpec((tm, d), lambda i, j: (i, 0)),
            _const_spec((1, d)),
            pl.BlockSpec((None, d, tn), lambda i, j: (layer, 0, j)),
        ],
        out_specs=pl.BlockSpec((tm, tn), lambda i, j: (i, j)),
        out_shape=jax.ShapeDtypeStruct((m, n), F32),
        scratch_shapes=[pltpu.VMEM((tm, d), BF16)],
        compiler_params=_params(2),
        name="norm_mm",
    )(x, g, w_all)


def _mix_out_kernel(au_ref, av_ref, bx_ref, cq_ref, ck_ref, cv_ref, co_ref, db_ref, dc_ref, dx_ref,
                    gc_ref, gr_ref, ws_ref, bs_ref, wp_ref, ps_ref, mn_ref, bgr_ref, bgc_ref, wd_ref,
                    x_ref, wo_ref, nc_ref, wq_ref,
                    x1_ref, q_ref, cs_ref, c_ref, n_ref, m_ref,
                    c_sc, n_sc, m_sc, xx_sc, xd_sc, y_ref, y_prev, pa_sc, pb_sc, *, tiles_per_seq):
    g = pl.program_id(0)
    n_steps = pl.num_programs(0)
    t = jnp.minimum(g, n_steps - 2) % tiles_per_seq
    rows = x1_ref.shape[0]

    @pl.when(g == 0)
    def _():
        y_prev[...] = jnp.zeros(y_prev.shape, y_prev.dtype)
    gw = GROUP_WIDTH
    pool_pad = 2 * SUBLANES
    conv_pad = SUBLANES

    @pl.when(t == 0)
    def _():
        c_sc[...] = jnp.zeros_like(c_sc)
        n_sc[...] = jnp.zeros_like(n_sc)
        m_sc[...] = jnp.zeros_like(m_sc)
        xx_sc[0:pool_pad, :] = jnp.zeros((pool_pad, gw), F32)
        xd_sc[0:conv_pad, :] = jnp.zeros((conv_pad, gw), F32)

    col_chunk = x_ref.shape[1] // 2

    def out_proj(c):
        cs = slice(c * col_chunk, (c + 1) * col_chunk)
        x1_ref[:, cs] = x_ref[:, cs] + jnp.dot(y_prev[...], wo_ref[:, cs], preferred_element_type=F32)

    def q_proj():
        hn = _rms(x1_ref[...], nc_ref[...]).astype(BF16)
        q_ref[...] = jnp.dot(hn, wq_ref[...], preferred_element_type=F32)

    def mixer_a():
        ri = lax.broadcasted_iota(jnp.int32, (HEAD_DIM, HEAD_DIM), 0)
        ci = lax.broadcasted_iota(jnp.int32, (HEAD_DIM, HEAD_DIM), 1)
        for h in range(N_HEADS):
            hs = slice(h * HEAD_DIM, (h + 1) * HEAD_DIM)
            w_tril = jnp.where(ri >= ci, ws_ref[h], 0.0).astype(BF16)
            for c in range(rows // HEAD_DIM):
                rs = slice(c * HEAD_DIM, (c + 1) * HEAD_DIM)
                v = _gelu(av_ref[rs, hs]).astype(BF16)
                mixed = jnp.dot(w_tril, v, preferred_element_type=F32) + bs_ref[h]
                y_ref[rs, hs] = (_gelu(au_ref[rs, hs]) * mixed).astype(y_ref.dtype)

    def mixer_b():
        xx_sc[pool_pad:pool_pad + rows, :] = bx_ref[...]
        avail = (t * rows + 1 + lax.broadcasted_iota(jnp.int32, (rows, 1), 0)).astype(F32)
        total = pool_pad + rows
        for g, win in enumerate(POOL_WINDOWS):
            gs = slice(g * HEAD_DIM, (g + 1) * HEAD_DIM)
            levels = win.bit_length() - 1
            src, cols, lo = xx_sc, gs, 0
            for lvl in range(levels):
                shift = 1 << lvl
                first = pool_pad if lvl == levels - 1 else lo + shift
                cur = src[first:total, cols] + src[first - shift:total - shift, cols]
                if lvl < levels - 1:
                    dst = pa_sc if lvl % 2 == 0 else pb_sc
                    dst[first:total, :] = cur
                    src, cols, lo = dst, slice(None), first
            pooled = cur / jnp.minimum(avail, float(win)) - bx_ref[:, gs]
            yb = jnp.dot(pooled.astype(BF16), wp_ref[g].astype(BF16), preferred_element_type=F32) * ps_ref[:, gs]
            y_ref[:, gw + g * HEAD_DIM:gw + (g + 1) * HEAD_DIM] = yb.astype(y_ref.dtype)
        xx_sc[0:pool_pad, :] = xx_sc[rows:rows + pool_pad, :]

    def mixer_d():
        for g in range(gw // LANES):
            gs = slice(g * LANES, (g + 1) * LANES)
            dcx = dc_ref[:, gs] * dx_ref[:, gs]
            xd_sc[conv_pad:conv_pad + rows, gs] = dcx
            conv = (wd_ref[0:1, gs] * xd_sc[conv_pad - 2:conv_pad - 2 + rows, gs]
                    + wd_ref[1:2, gs] * xd_sc[conv_pad - 1:conv_pad - 1 + rows, gs]
                    + wd_ref[2:3, gs] * dcx)
            y_ref[:, 3 * gw + g * LANES:3 * gw + (g + 1) * LANES] = (db_ref[:, gs] * conv).astype(y_ref.dtype)
        cs_ref[...] = xd_sc[conv_pad + rows - 2:conv_pad + rows, :]
        xd_sc[0:conv_pad, :] = xd_sc[rows:rows + conv_pad, :]

    gc = gc_ref[...] + bgr_ref[...]
    gr = gr_ref[...] + bgc_ref[...]
    rr = lax.broadcasted_iota(jnp.int32, (rows, rows), 0)
    cc = lax.broadcasted_iota(jnp.int32, (rows, rows), 1)
    causal = rr >= cc
    b_cols = jnp.dot(causal.astype(F32), _log_sigmoid(gc), precision=HIGHEST, preferred_element_type=F32)
    b_rows = jnp.dot(_log_sigmoid(gr), (rr <= cc).astype(F32), precision=HIGHEST, preferred_element_type=F32)
    scale = HEAD_DIM ** -0.5

    def mixer_c(h):
        hs = slice(h * HEAD_DIM, (h + 1) * HEAD_DIM)
        bc = _lane_col(b_cols, N_HEADS + h)
        ic = _lane_col(gc, h)
        br = b_rows[N_HEADS + h:N_HEADS + h + 1, :]
        ir = gr[h:h + 1, :]
        m_prev = m_sc[h:h + 1, 0:1]
        log_d = jnp.where(causal, bc - br + ir, -jnp.inf)
        log_inter = bc + m_prev
        m_t = jnp.maximum(log_inter, jnp.max(log_d, axis=-1, keepdims=True))
        dmat = jnp.exp(log_d - m_t)
        inter = jnp.exp(log_inter - m_t)
        qf = cq_ref[:, hs]
        kf = ck_ref[:, hs] * scale
        vf = cv_ref[:, hs]
        q = qf.astype(BF16)
        k = kf.astype(BF16)
        s = lax.dot_general(q, k, _NT, preferred_element_type=F32) * dmat
        c_prev = c_sc[h]
        n_prev = n_sc[h:h + 1, :]
        num = (jnp.dot(s.astype(BF16), vf.astype(BF16), preferred_element_type=F32)
               + inter * lax.dot_general(q, c_prev.astype(BF16), _NT, preferred_element_type=F32))
        den = jnp.sum(s, axis=-1, keepdims=True) + inter * jnp.sum(qf * n_prev, axis=-1, keepdims=True)
        hout = num / jnp.maximum(jnp.abs(den), jnp.exp(-m_t))
        hout = hout * lax.rsqrt(jnp.mean(hout * hout, axis=-1, keepdims=True) + EPS)
        yc = _sigmoid(co_ref[:, hs]) * hout * mn_ref[:, hs]
        y_ref[:, 2 * gw + h * HEAD_DIM:2 * gw + (h + 1) * HEAD_DIM] = yc.astype(y_ref.dtype)
        m_new = m_t[rows - 1:rows, :]
        b_last = bc[rows - 1:rows, :]
        decay = jnp.exp(b_last + m_prev - m_new)
        wts = jnp.exp(b_last - bc + ic - m_new)
        wv_t = jnp.transpose(wts * vf).astype(BF16)
        c_sc[h] = decay * c_prev + jnp.dot(wv_t, k, preferred_element_type=F32)
        n_sc[h:h + 1, :] = decay * n_prev + jnp.sum(wts * kf, axis=0, keepdims=True)
        m_sc[h:h + 1, :] = jnp.broadcast_to(m_new, (1, LANES))

    mixer_c(0)
    out_proj(0)
    mixer_a()
    mixer_c(1)
    out_proj(1)
    mixer_b()
    mixer_c(2)
    q_proj()
    mixer_d()
    mixer_c(3)

    y_prev[...] = y_ref[...]

    @pl.when(jnp.logical_and(t == tiles_per_seq - 1, g < n_steps - 1))
    def _():
        c_ref[...] = c_sc[...]
        n_ref[...] = n_sc[...]
        m_ref[...] = m_sc[...]


def _mix_out(z, gates, gates_t, lw, x, w_out_all, w_q_all, layer, n_seq, seq, rows=256):
    gw = GROUP_WIDTH
    d = x.shape[1]
    nq = w_q_all.shape[2]
    tps = seq // rows
    n_tiles = n_seq * tps
    last = n_tiles - 1

    def cur(g):
        return jnp.minimum(g, last)

    def prev(g):
        return jnp.maximum(g - 1, 0)

    zspecs = [pl.BlockSpec((rows, gw), functools.partial(lambda g, j: (cur(g), j), j=j)) for j in range(N_SLICES)]
    in_specs = zspecs + [
        pl.BlockSpec((rows, LANES), lambda g: (cur(g), 0)),
        pl.BlockSpec((SUBLANES, rows), lambda g: (0, cur(g))),
        _const_spec((N_HEADS, HEAD_DIM, HEAD_DIM)),
        _const_spec((N_HEADS, HEAD_DIM, 1)),
        _const_spec((N_HEADS, HEAD_DIM, HEAD_DIM)),
        _const_spec((1, gw)),
        _const_spec((1, gw)),
        _const_spec((1, LANES)),
        _const_spec((SUBLANES, 1)),
        _const_spec((3, gw)),
        pl.BlockSpec((rows, d), lambda g: (prev(g), 0)),
        _layer_spec(w_out_all.shape[1:], layer),
        _const_spec((1, d)),
        _layer_spec(w_q_all.shape[1:], layer),
    ]
    out_shape = [
        jax.ShapeDtypeStruct((n_seq * seq, d), F32),
        jax.ShapeDtypeStruct((n_seq * seq, nq), F32),
        jax.ShapeDtypeStruct((n_seq, 2, gw), F32),
        jax.ShapeDtypeStruct((n_seq, N_HEADS, HEAD_DIM, HEAD_DIM), F32),
        jax.ShapeDtypeStruct((n_seq, SUBLANES, HEAD_DIM), F32),
        jax.ShapeDtypeStruct((n_seq, SUBLANES, LANES), F32),
    ]
    out_specs = [
        pl.BlockSpec((rows, d), lambda g: (prev(g), 0)),
        pl.BlockSpec((rows, nq), lambda g: (prev(g), 0)),
        pl.BlockSpec((None, 2, gw), lambda g: (cur(g) // tps, 0, 0)),
        pl.BlockSpec((None, N_HEADS, HEAD_DIM, HEAD_DIM), lambda g: (cur(g) // tps, 0, 0, 0)),
        pl.BlockSpec((None, SUBLANES, HEAD_DIM), lambda g: (cur(g) // tps, 0, 0)),
        pl.BlockSpec((None, SUBLANES, LANES), lambda g: (cur(g) // tps, 0, 0)),
    ]
    scratch = [
        pltpu.VMEM((N_HEADS, HEAD_DIM, HEAD_DIM), F32),
        pltpu.VMEM((SUBLANES, HEAD_DIM), F32),
        pltpu.VMEM((SUBLANES, LANES), F32),
        pltpu.VMEM((2 * SUBLANES + rows, gw), F32),
        pltpu.VMEM((SUBLANES + rows, gw), F32),
        pltpu.VMEM((rows, 4 * gw), BF16),
        pltpu.VMEM((rows, 4 * gw), BF16),
        pltpu.VMEM((2 * SUBLANES + rows, LANES), F32),
        pltpu.VMEM((2 * SUBLANES + rows, LANES), F32),
    ]
    return pl.pallas_call(
        functools.partial(_mix_out_kernel, tiles_per_seq=tps),
        grid=(n_tiles + 1,),
        in_specs=in_specs,
        out_specs=out_specs,
        out_shape=out_shape,
        scratch_shapes=scratch,
        compiler_params=_params(1),
        name="mix_out",
    )(*([z] * N_SLICES), gates, gates_t, lw["w_s"], lw["b_s_col"], lw["w_pool"], lw["pool_scale"],
      lw["mlstm_norm"], lw["bg_row"], lw["bg_col"], lw["w_dconv"], x, w_out_all, lw["norm_cross"], w_q_all)


def _mix_sample_kernel(au_ref, av_ref, bx_ref, cq_ref, ck_ref, cv_ref, co_ref, db_ref, dc_ref, dx_ref,
                       g_ref, wst_ref, bst_ref, wp_ref, ps_ref, mn_ref, bgr_ref, wd_ref,
                       pp_ref, cp_ref, c0_ref, n0_ref, m0_ref, c_buf_ref,
                       y_ref, vr_ref, cs_ref, c1_ref, n1_ref, m1_ref):
    ts = au_ref.shape[0]
    nb = au_ref.shape[1]
    gw = GROUP_WIDTH

    v = [_gelu(av_ref[t]) for t in range(ts)]
    for t in range(ts):
        vr_ref[t] = v[t]
        mixed = bst_ref[t:t + 1, :]
        for s in range(t + 1):
            mixed = mixed + wst_ref[t * ts + s:t * ts + s + 1, :] * v[s]
        y_ref[t, :, 0:gw] = _gelu(au_ref[t]) * mixed

    bx = [bx_ref[t] for t in range(ts)]
    hist = [pp_ref[r] for r in range(POOL_STATE)] + bx
    pooled = []
    for t in range(ts):
        parts = []
        for g, win in enumerate(POOL_WINDOWS):
            gs = slice(g * HEAD_DIM, (g + 1) * HEAD_DIM)
            acc = hist[POOL_STATE + t][:, gs]
            for k in range(1, win):
                acc = acc + hist[POOL_STATE + t - k][:, gs]
            parts.append(acc / float(win) - bx[t][:, gs])
        pooled.append(parts)
    for g in range(len(POOL_WINDOWS)):
        gs = slice(g * HEAD_DIM, (g + 1) * HEAD_DIM)
        pg = jnp.concatenate([pooled[t][g] for t in range(ts)], axis=0).astype(BF16)
        yb = jnp.dot(pg, wp_ref[g].astype(BF16), preferred_element_type=F32) * ps_ref[:, gs]
        for t in range(ts):
            y_ref[t, :, gw + g * HEAD_DIM:gw + (g + 1) * HEAD_DIM] = yb[t * nb:(t + 1) * nb, :]

    dcx = [dc_ref[t] * dx_ref[t] for t in range(ts)]
    xx = [cp_ref[0], cp_ref[1]] + dcx
    for t in range(ts):
        conv = wd_ref[0:1, :] * xx[t] + wd_ref[1:2, :] * xx[t + 1] + wd_ref[2:3, :] * xx[t + 2]
        y_ref[t, :, 3 * gw:4 * gw] = db_ref[t] * conv
    cs_ref[0] = xx[ts]
    cs_ref[1] = xx[ts + 1]

    gate = [g_ref[t] + bgr_ref[...] for t in range(ts)]
    lf = [pltpu.roll(_log_sigmoid(gate[t]), LANES - N_HEADS, 1) for t in range(ts)]
    m0 = m0_ref[...]
    b = []
    for t in range(ts):
        b.append(lf[t] if t == 0 else b[t - 1] + lf[t])
    log_d = [[b[t] - b[s] + gate[s] for s in range(t + 1)] for t in range(ts)]
    m_t = []
    for t in range(ts):
        mt = b[t] + m0
        for s in range(t + 1):
            mt = jnp.maximum(mt, log_d[t][s])
        m_t.append(mt)
    dmat = [[jnp.exp(log_d[t][s] - m_t[t]) for s in range(t + 1)] for t in range(ts)]
    inter = [jnp.exp(b[t] + m0 - m_t[t]) for t in range(ts)]
    floor = [jnp.exp(-m_t[t]) for t in range(ts)]
    m_new = m_t[ts - 1]
    decay = jnp.exp(b[ts - 1] + m0 - m_new)
    wts = [jnp.exp(b[ts - 1] - b[s] + gate[s] - m_new) for s in range(ts)]
    m1_ref[...] = m_new

    scale = HEAD_DIM ** -0.5
    row_seq = lax.broadcasted_iota(jnp.int32, (ts * nb, HEAD_DIM), 0) % nb
    for h in range(N_HEADS):
        hs = slice(h * HEAD_DIM, (h + 1) * HEAD_DIM)
        q = [cq_ref[t][:, hs] for t in range(ts)]
        k = [ck_ref[t][:, hs] * scale for t in range(ts)]
        vv = [cv_ref[t][:, hs] for t in range(ts)]
        n_prev = n0_ref[:, hs]
        q_blk = jnp.concatenate(q, axis=0).astype(BF16)
        k_blk = jnp.concatenate(k, axis=0).astype(BF16)
        dec_col = _lane_col(decay, h)
        wt_col = [_lane_col(wts[s], h) for s in range(ts)]
        wv_blk = jnp.concatenate([wt_col[s] * vv[s] for s in range(ts)], axis=0)
        qc = jnp.zeros((ts * nb, HEAD_DIM), F32)
        for i in range(nb):
            c_prev = c0_ref[i, h]
            r = lax.dot_general(q_blk, c_prev.astype(BF16), _NT, preferred_element_type=F32)
            qc = jnp.where(row_seq == i, r, qc)
            wv_i = jnp.where(row_seq == i, wv_blk, 0.0).astype(BF16)
            upd = lax.dot_general(wv_i, k_blk, (((0,), (0,)), ((), ())), preferred_element_type=F32)
            c1_ref[i, h] = dec_col[i:i + 1, :] * c_prev + upd
        n_new = dec_col * n_prev
        for s in range(ts):
            n_new = n_new + wt_col[s] * k[s]
        n1_ref[:, hs] = n_new
        for t in range(ts):
            int_col = _lane_col(inter[t], h)
            num = int_col * qc[t * nb:(t + 1) * nb, :]
            den = int_col * jnp.sum(q[t] * n_prev, axis=-1, keepdims=True)
            for s in range(t + 1):
                s_ts = jnp.sum(q[t] * k[s], axis=-1, keepdims=True) * _lane_col(dmat[t][s], h)
                num = num + s_ts * vv[s]
                den = den + s_ts
            hout = num / jnp.maximum(jnp.abs(den), _lane_col(floor[t], h))
            hout = hout * lax.rsqrt(jnp.mean(hout * hout, axis=-1, keepdims=True) + EPS)
            y_ref[t, :, 2 * gw + h * HEAD_DIM:2 * gw + (h + 1) * HEAD_DIM] = (
                _sigmoid(co_ref[t][:, hs]) * hout * mn_ref[:, hs])


def _mix_sample(z3, gates3, lw, pool_prefix, conv_prefix, c_all, n_all, m_all, c_new_all, layer, nb=8):
    fresh = c_new_all is None
    if fresh:
        c_new_all = c_all
    ts, n_seq, _ = z3.shape
    n_in = N_SLICES + 13
    gw = GROUP_WIDTH
    zspecs = [pl.BlockSpec((ts, nb, gw), functools.partial(lambda i, j: (0, i, j), j=j)) for j in range(N_SLICES)]
    in_specs = zspecs + [
        pl.BlockSpec((ts, nb, LANES), lambda i: (0, i, 0)),
        _const_spec((ts * ts, gw)),
        _const_spec((ts, gw)),
        _const_spec((N_HEADS, HEAD_DIM, HEAD_DIM)),
        _const_spec((1, gw)),
        _const_spec((1, gw)),
        _const_spec((1, LANES)),
        _const_spec((3, gw)),
        pl.BlockSpec((POOL_STATE, nb, gw), lambda i: (0, i, 0)),
        pl.BlockSpec((2, nb, gw), lambda i: (0, i, 0)),
        pl.BlockSpec((None, nb, N_HEADS, HEAD_DIM, HEAD_DIM), lambda i: (layer, i, 0, 0, 0)),
        pl.BlockSpec((None, nb, gw), lambda i: (layer, i, 0)),
        pl.BlockSpec((None, nb, LANES), lambda i: (layer, i, 0)),
        pl.BlockSpec(memory_space=pl.ANY),
    ]
    assert len(in_specs) == n_in + 1
    out_shape = [
        jax.ShapeDtypeStruct((ts, n_seq, 4 * gw), F32),
        jax.ShapeDtypeStruct((ts, n_seq, gw), F32),
        jax.ShapeDtypeStruct((2, n_seq, gw), F32),
        jax.ShapeDtypeStruct(c_new_all.shape, F32),
        jax.ShapeDtypeStruct((n_seq, gw), F32),
        jax.ShapeDtypeStruct((n_seq, LANES), F32),
    ]
    out_specs = [
        pl.BlockSpec((ts, nb, 4 * gw), lambda i: (0, i, 0)),
        pl.BlockSpec((ts, nb, gw), lambda i: (0, i, 0)),
        pl.BlockSpec((2, nb, gw), lambda i: (0, i, 0)),
        pl.BlockSpec((None, nb, N_HEADS, HEAD_DIM, HEAD_DIM), lambda i: (layer, i, 0, 0, 0)),
        pl.BlockSpec((nb, gw), lambda i: (i, 0)),
        pl.BlockSpec((nb, LANES), lambda i: (i, 0)),
    ]
    return pl.pallas_call(
        _mix_sample_kernel,
        grid=(n_seq // nb,),
        in_specs=in_specs,
        out_specs=out_specs,
        out_shape=out_shape,
        input_output_aliases={} if fresh else {n_in: 3},
        compiler_params=_params(1),
        name="mix_sample",
    )(*([z3] * N_SLICES), gates3, lw["ws_tab"], lw["bs_tab"], lw["w_pool"], lw["pool_scale"],
      lw["mlstm_norm"], lw["bg_row"], lw["w_dconv"], pool_prefix, conv_prefix, c_all, n_all, m_all, c_new_all)


def _out_q_kernel(x_ref, y_ref, wo_ref, g_ref, wq_ref, x1_ref, q_ref):
    x1 = x_ref[...] + jnp.dot(y_ref[...].astype(BF16), wo_ref[...], preferred_element_type=F32)
    x1_ref[...] = x1
    q_ref[...] = jnp.dot(_rms(x1, g_ref[...]).astype(BF16), wq_ref[...], preferred_element_type=F32)


def _layer_spec(shape, layer):
    nd = len(shape)
    return pl.BlockSpec((None,) + tuple(shape), lambda *_: (layer,) + (0,) * nd, pipeline_mode=pl.Buffered(1))


def _out_q(x, y, w_out_all, g, w_q_all, layer, tm=512):
    m, d = x.shape
    tm = min(tm, m)
    nq = w_q_all.shape[2]
    return pl.pallas_call(
        _out_q_kernel,
        grid=(m // tm,),
        in_specs=[
            pl.BlockSpec((tm, d), lambda i: (i, 0)),
            pl.BlockSpec((tm, y.shape[1]), lambda i: (i, 0)),
            _layer_spec(w_out_all.shape[1:], layer),
            _const_spec((1, d)),
            _layer_spec(w_q_all.shape[1:], layer),
        ],
        out_specs=[pl.BlockSpec((tm, d), lambda i: (i, 0)), pl.BlockSpec((tm, nq), lambda i: (i, 0))],
        out_shape=[jax.ShapeDtypeStruct((m, d), F32), jax.ShapeDtypeStruct((m, nq), F32)],
        compiler_params=_params(1),
        name="out_q",
    )(x, y, w_out_all, g, w_q_all)


def _softmax_rows(s):
    e = jnp.exp(s - jnp.max(s, axis=-1, keepdims=True))
    return e / jnp.sum(e, axis=-1, keepdims=True)


def _attn_prompt_kernel(q_ref, k_ref, v_ref, o_ref):
    scale = HEAD_DIM ** -0.5
    for h in range(N_HEADS):
        hs = slice(h * HEAD_DIM, (h + 1) * HEAD_DIM)
        q = q_ref[:, hs].astype(BF16)
        s = lax.dot_general(q, k_ref[:, hs].astype(BF16), _NT, preferred_element_type=F32) * scale
        p = _softmax_rows(s).astype(BF16)
        o_ref[:, hs] = jnp.dot(p, v_ref[:, hs].astype(BF16), preferred_element_type=F32)


def _attn_prompt(q, k, v, n_seq, seq, tq=512):
    tps = seq // tq
    w = q.shape[1]
    return pl.pallas_call(
        _attn_prompt_kernel,
        grid=(n_seq, tps),
        in_specs=[
            pl.BlockSpec((tq, w), lambda n, t: (n * tps + t, 0)),
            pl.BlockSpec((None, N_MEM, w), lambda n, t: (n, 0, 0)),
            pl.BlockSpec((None, N_MEM, w), lambda n, t: (n, 0, 0)),
        ],
        out_specs=pl.BlockSpec((tq, w), lambda n, t: (n * tps + t, 0)),
        out_shape=jax.ShapeDtypeStruct(q.shape, F32),
        compiler_params=_params(2),
        name="attn_prompt",
    )(q, k, v)


def _attn_sample_kernel(q_ref, k_ref, v_ref, o_ref):
    nb, rows, _ = q_ref.shape
    n_kv = k_ref.shape[1]
    scale = HEAD_DIM ** -0.5
    row_head = lax.broadcasted_iota(jnp.int32, (rows, n_kv), 0) % N_HEADS
    col_head = lax.broadcasted_iota(jnp.int32, (rows, n_kv), 1) % N_HEADS
    same_head = row_head == col_head
    scores = []
    for i in range(nb):
        s = lax.dot_general(q_ref[i].astype(BF16), k_ref[i].astype(BF16), _NT, preferred_element_type=F32) * scale
        scores.append(jnp.where(same_head, s, -jnp.inf))
    p = _softmax_rows(jnp.concatenate(scores, axis=0)).astype(BF16)
    for i in range(nb):
        o_ref[i] = jnp.dot(p[i * rows:(i + 1) * rows, :], v_ref[i].astype(BF16), preferred_element_type=F32)


def _attn_sample(q, k_all, v_all, layer, nb=8):
    n_seq, rows, hd = q.shape
    n_kv = k_all.shape[2]
    return pl.pallas_call(
        _attn_sample_kernel,
        grid=(n_seq // nb,),
        in_specs=[
            pl.BlockSpec((nb, rows, hd), lambda i: (i, 0, 0)),
            pl.BlockSpec((None, nb, n_kv, hd), lambda i: (layer, i, 0, 0)),
            pl.BlockSpec((None, nb, n_kv, hd), lambda i: (layer, i, 0, 0)),
        ],
        out_specs=pl.BlockSpec((nb, rows, hd), lambda i: (i, 0, 0)),
        out_shape=jax.ShapeDtypeStruct(q.shape, F32),
        compiler_params=_params(1),
        name="attn_sample",
    )(q, k_all, v_all)


def _ffn_kernel(x1_ref, o_ref, wo_ref, g_ref, wg_ref, wu_ref, fg_ref, fu_ref, wd_ref, pg_ref, pu_ref, fn_ref,
                out_ref, tg_ref, tu_ref, hn_sc, eg_sc, eu_sc, cg_sc, cu_sc, *, shift, tiles_per_seq, chunk,
                apply_final_norm):
    i = pl.program_id(0)
    j = pl.program_id(1)
    tm = out_ref.shape[0]
    pad = pg_ref.shape[0]

    @pl.when(j == 0)
    def _():
        for r in range(tm // chunk):
            rs = slice(r * chunk, (r + 1) * chunk)
            x2 = x1_ref[rs, :] + jnp.dot(o_ref[rs, :].astype(BF16), wo_ref[...], preferred_element_type=F32)
            out_ref[rs, :] = x2
            hn_sc[rs, :] = _rms(x2, g_ref[...]).astype(BF16)

    first = (i % tiles_per_seq) == 0

    @pl.when(first)
    def _():
        eg_sc[0:pad, :] = pg_ref[...]
        eu_sc[0:pad, :] = pu_ref[...]

    @pl.when(jnp.logical_not(first))
    def _():
        eg_sc[0:pad, :] = cg_sc[j]
        eu_sc[0:pad, :] = cu_sc[j]

    def conv(e_sc, f_ref, r0, r1):
        return (f_ref[0:1, :] * e_sc[pad - 2 * shift + r0:pad - 2 * shift + r1, :]
                + f_ref[1:2, :] * e_sc[pad - shift + r0:pad - shift + r1, :]
                + f_ref[2:3, :] * e_sc[pad + r0:pad + r1, :])

    def up(r0, r1):
        hn = hn_sc[r0:r1, :]
        eg_sc[pad + r0:pad + r1, :] = jnp.dot(hn, wg_ref[...], preferred_element_type=F32)
        eu_sc[pad + r0:pad + r1, :] = jnp.dot(hn, wu_ref[...], preferred_element_type=F32)

    n_chunks = tm // chunk
    spans = [(r * chunk, (r + 1) * chunk) for r in range(n_chunks)]
    ahead = 2
    for span in spans[:ahead]:
        up(*span)
    for idx, (r0, r1) in enumerate(spans):
        if idx + ahead < len(spans):
            up(*spans[idx + ahead])
        gate = conv(eg_sc, fg_ref, r0, r1)
        act = (gate * _sigmoid(gate) * conv(eu_sc, fu_ref, r0, r1)).astype(BF16)
        out_ref[r0:r1, :] += jnp.dot(act, wd_ref[...], preferred_element_type=F32)

    tail_g = eg_sc[tm:tm + pad, :]
    tail_u = eu_sc[tm:tm + pad, :]
    cg_sc[j] = tail_g
    cu_sc[j] = tail_u
    tg_ref[...] = tail_g
    tu_ref[...] = tail_u

    if apply_final_norm:
        @pl.when(j == pl.num_programs(1) - 1)
        def _():
            for r in range(n_chunks):
                rs = slice(r * chunk, (r + 1) * chunk)
                out_ref[rs, :] = _rms(out_ref[rs, :], fn_ref[...])


def _ffn(x1, o, w_o_all, g, w_up_all, f_conv, w_down_all, prefix_all, final_g, layer, pre_layer, tm, shift,
         tiles_per_seq, apply_final_norm, chunk=256):
    m, d = x1.shape
    dfp = w_down_all.shape[1]
    tn = FF_TILE
    nj = dfp // tn
    pad = prefix_all.shape[1]
    n_tiles = m // tm
    return pl.pallas_call(
        functools.partial(_ffn_kernel, shift=shift, tiles_per_seq=tiles_per_seq, chunk=min(chunk, tm),
                          apply_final_norm=apply_final_norm),
        grid=(n_tiles, nj),
        in_specs=[
            pl.BlockSpec((tm, d), lambda i, j: (i, 0), pipeline_mode=pl.Buffered(1)),
            pl.BlockSpec((tm, o.shape[1]), lambda i, j: (i, 0)),
            _layer_spec(w_o_all.shape[1:], layer),
            _const_spec((1, d)),
            pl.BlockSpec((None, d, tn), lambda i, j: (layer, 0, j)),
            pl.BlockSpec((None, d, tn), lambda i, j: (layer, 0, nj + j)),
            pl.BlockSpec((3, tn), lambda i, j: (0, j)),
            pl.BlockSpec((3, tn), lambda i, j: (0, nj + j)),
            pl.BlockSpec((None, tn, d), lambda i, j: (layer, j, 0)),
            pl.BlockSpec((None, pad, tn), lambda i, j: (pre_layer, 0, j)),
            pl.BlockSpec((None, pad, tn), lambda i, j: (pre_layer, 0, nj + j)),
            _const_spec((1, d)),
        ],
        out_specs=[
            pl.BlockSpec((tm, d), lambda i, j: (i, 0)),
            pl.BlockSpec((pad, tn), lambda i, j: (i, j)),
            pl.BlockSpec((pad, tn), lambda i, j: (i, j)),
        ],
        out_shape=[
            jax.ShapeDtypeStruct((m, d), F32),
            jax.ShapeDtypeStruct((n_tiles * pad, dfp), F32),
            jax.ShapeDtypeStruct((n_tiles * pad, dfp), F32),
        ],
        scratch_shapes=[
            pltpu.VMEM((tm, d), BF16),
            pltpu.VMEM((pad + tm, tn), F32),
            pltpu.VMEM((pad + tm, tn), F32),
            pltpu.VMEM((nj, pad, tn), F32),
            pltpu.VMEM((nj, pad, tn), F32),
        ],
        compiler_params=_params(2),
        name="ffn",
    )(x1, o, w_o_all, g, w_up_all, w_up_all, f_conv, f_conv, w_down_all, prefix_all, prefix_all, final_g)


def _cast_up_kernel(w_ref, o_ref):
    d_ff = w_ref.shape[1] // 2
    dfp = o_ref.shape[1] // 2
    rows = w_ref.shape[0]
    for half in range(2):
        o_ref[:, half * dfp:half * dfp + d_ff] = w_ref[:, half * d_ff:(half + 1) * d_ff].astype(BF16)
        o_ref[:, half * dfp + d_ff:(half + 1) * dfp] = jnp.zeros((rows, dfp - d_ff), BF16)


def _cast_up(w_up, dfp, tr=128):
    depth, d, two_dff = w_up.shape
    return pl.pallas_call(
        _cast_up_kernel,
        grid=(depth, d // tr),
        in_specs=[pl.BlockSpec((None, tr, two_dff), lambda l, i: (l, i, 0))],
        out_specs=pl.BlockSpec((None, tr, 2 * dfp), lambda l, i: (l, i, 0)),
        out_shape=jax.ShapeDtypeStruct((depth, d, 2 * dfp), BF16),
        compiler_params=_params(2),
        name="cast_up",
    )(w_up)


def _cast_down_kernel(w_ref, rem_ref, o_ref):
    i = pl.program_id(1)
    last = pl.num_programs(1) - 1
    rem = rem_ref.shape[0]

    @pl.when(i < last)
    def _():
        o_ref[...] = w_ref[...].astype(BF16)

    @pl.when(i == last)
    def _():
        o_ref[0:rem, :] = rem_ref[...].astype(BF16)
        o_ref[rem:, :] = jnp.zeros((o_ref.shape[0] - rem, o_ref.shape[1]), BF16)


def _cast_down(w_down, dfp, tr=256):
    depth, d_ff, d = w_down.shape
    n_full = d_ff // tr
    rem = d_ff - n_full * tr
    assert 0 < rem < tr and d_ff % rem == 0 and dfp == (n_full + 1) * tr
    return pl.pallas_call(
        _cast_down_kernel,
        grid=(depth, n_full + 1),
        in_specs=[
            pl.BlockSpec((None, tr, d), lambda l, i: (l, jnp.minimum(i, n_full - 1), 0)),
            pl.BlockSpec((None, rem, d), lambda l, i: (l, d_ff // rem - 1, 0)),
        ],
        out_specs=pl.BlockSpec((None, tr, d), lambda l, i: (l, i, 0)),
        out_shape=jax.ShapeDtypeStruct((depth, dfp, d), BF16),
        compiler_params=_params(2),
        name="cast_down",
    )(w_down, w_down)


def _pad_cols(a, width):
    return jnp.pad(a, [(0, 0)] * (a.ndim - 1) + [(0, width - a.shape[-1])])


def _layer_params(l, ts, d_ff, dfp, norm_mix, w_s, b_s, w_pool, pool_scale, b_gate, mlstm_norm, w_dconv, norm_cross,
                  norm_ffn, w_fconv):
    ws_l, bs_l, fc = w_s[l], b_s[l], w_fconv[l]
    return {
        "norm_mix": norm_mix[l][None, :],
        "w_s": ws_l,
        "b_s_col": bs_l[:, :, None],
        "ws_tab": jnp.repeat(jnp.transpose(ws_l[:, :ts, :ts], (1, 2, 0)).reshape(ts * ts, N_HEADS), HEAD_DIM, axis=1),
        "bs_tab": jnp.repeat(jnp.transpose(bs_l[:, :ts], (1, 0)), HEAD_DIM, axis=1),
        "w_pool": w_pool[l],
        "pool_scale": pool_scale[l][None, :],
        "mlstm_norm": mlstm_norm[l][None, :],
        "bg_row": _pad_cols(b_gate[l][None, :], LANES),
        "bg_col": b_gate[l][:, None],
        "w_dconv": w_dconv[l],
        "norm_cross": norm_cross[l][None, :],
        "norm_ffn": norm_ffn[l][None, :],
        "f_conv": jnp.concatenate([_pad_cols(fc[:, :d_ff], dfp), _pad_cols(fc[:, d_ff:], dfp)], axis=1),
    }


def kernel(x_prompt, x_sample, mem_prompt, state_pool, state_conv, state_mlstm_c, state_mlstm_n, state_mlstm_m,
           state_ffn_conv, cache_mem_k, cache_mem_v, norm_mix, w_in, w_s, b_s, w_pool, pool_scale, b_gate,
           mlstm_norm, w_dconv, w_out, mem_norm, w_kv, norm_cross, w_q, w_o, norm_ffn, w_up, w_fconv, w_down,
           final_norm):
    depth = w_in.shape[0]
    nbp, seq, d = x_prompt.shape
    nbs, ts, _ = x_sample.shape
    gw = GROUP_WIDTH
    mem_w = w_q.shape[2]
    n_main = N_SLICES * gw
    d_ff = w_down.shape[1]
    dfp = -(-d_ff // FF_TILE) * FF_TILE

    w_gate_b = _pad_cols(w_in[:, :, n_main:], LANES).astype(BF16)
    w_in_b = w_in.astype(BF16)
    w_out_b = w_out.astype(BF16)
    w_q_b = w_q.astype(BF16)
    w_o_b = w_o.astype(BF16)
    w_kv_b = w_kv.astype(BF16)
    w_up_b = _cast_up(w_up, dfp)
    w_down_b = _cast_down(w_down, dfp)

    xp = x_prompt.reshape(nbp * seq, d)
    xs = jnp.transpose(x_sample, (1, 0, 2)).reshape(ts * nbs, d)
    mem2 = mem_prompt.reshape(nbp * N_MEM, d)
    kc_all = cache_mem_k.reshape(depth, nbs, N_MEM * N_HEADS, HEAD_DIM)
    vc_all = cache_mem_v.reshape(depth, nbs, N_MEM * N_HEADS, HEAD_DIM)
    n_all = state_mlstm_n.reshape(depth, nbs, gw)
    m_all = _pad_cols(state_mlstm_m, LANES)
    pool_pre_all = jnp.transpose(state_pool, (0, 2, 1, 3))
    conv_pre_all = jnp.transpose(state_conv, (0, 2, 1, 3))
    ffn_pre_t = jnp.transpose(state_ffn_conv, (0, 2, 1, 3))
    ffn_pre_all = jnp.concatenate([_pad_cols(ffn_pre_t[..., :d_ff], dfp), _pad_cols(ffn_pre_t[..., d_ff:], dfp)],
                                  axis=-1).reshape(depth, 2 * nbs, 2 * dfp)
    pad_p = SUBLANES
    zeros_pre = jnp.zeros((1, pad_p, 2 * dfp), F32)
    final_g = final_norm[None, :]
    s_c = None

    tm_p = min(1024, seq)
    tm_s = ts * nbs
    outs = {k: [] for k in ("p_pool", "p_conv", "p_c", "p_n", "p_m", "p_tg", "p_tu", "p_kv",
                            "s_v", "s_bx", "s_conv", "s_n", "s_m", "s_tg", "s_tu")}
    for l in range(depth):
        last = l == depth - 1
        lw = _layer_params(l, ts, d_ff, dfp, norm_mix, w_s, b_s, w_pool, pool_scale, b_gate, mlstm_norm, w_dconv,
                           norm_cross, norm_ffn, w_fconv)

        kv = _norm_mm(mem2, mem_norm[l][None, :], w_kv_b, l, tm=mem2.shape[0], tn=512)
        outs["p_kv"].append(kv)
        mk = kv[:, :mem_w].reshape(nbp, N_MEM, mem_w)
        mv = kv[:, mem_w:].reshape(nbp, N_MEM, mem_w)

        z, gates = _proj_in(xp, lw["norm_mix"], w_in_b, w_gate_b, l, n_main, tm=tm_p)
        gates_t = jnp.transpose(gates[:, :SUBLANES])
        x1, q, conv_st, c1, n1, m1 = _mix_out(z, gates, gates_t, lw, xp, w_out_b, w_q_b, l, nbp, seq)
        outs["p_pool"].append(z.reshape(nbp, seq, n_main)[:, seq - POOL_STATE:, 2 * gw:3 * gw])
        outs["p_conv"].append(conv_st)
        outs["p_c"].append(c1)
        outs["p_n"].append(n1[:, :N_HEADS, :])
        outs["p_m"].append(m1[:, :N_HEADS, 0])
        o = _attn_prompt(q, mk, mv, nbp, seq)
        xp, tail_g, tail_u = _ffn(x1, o, w_o_b, lw["norm_ffn"], w_up_b, lw["f_conv"], w_down_b, zeros_pre, final_g,
                                  l, 0, tm=tm_p, shift=1, tiles_per_seq=seq // tm_p, apply_final_norm=last,
                                  chunk=512)
        outs["p_tg"].append(tail_g)
        outs["p_tu"].append(tail_u)

        z, gates = _proj_in(xs, lw["norm_mix"], w_in_b, w_gate_b, l, n_main, tm=tm_s)
        z3 = z.reshape(ts, nbs, n_main)
        y3, v_rows, conv_st, s_c, n1, m1 = _mix_sample(
            z3, gates.reshape(ts, nbs, LANES), lw, pool_pre_all[l], conv_pre_all[l], state_mlstm_c, n_all, m_all,
            s_c, l)
        outs["s_v"].append(v_rows)
        outs["s_bx"].append(z3[:, :, 2 * gw:3 * gw])
        outs["s_conv"].append(conv_st)
        outs["s_n"].append(n1)
        outs["s_m"].append(m1)
        x1, q = _out_q(xs, y3.reshape(ts * nbs, 4 * gw), w_out_b, lw["norm_cross"], w_q_b, l)
        q_seq = jnp.transpose(q.reshape(ts, nbs, N_HEADS, HEAD_DIM), (1, 0, 2, 3)).reshape(nbs, ts * N_HEADS, HEAD_DIM)
        o_seq = _attn_sample(q_seq, kc_all, vc_all, l)
        o = jnp.transpose(o_seq.reshape(nbs, ts, N_HEADS, HEAD_DIM), (1, 0, 2, 3)).reshape(ts * nbs, mem_w)
        xs, tail_g, tail_u = _ffn(x1, o, w_o_b, lw["norm_ffn"], w_up_b, lw["f_conv"], w_down_b, ffn_pre_all, final_g,
                                  l, l, tm=tm_s, shift=nbs, tiles_per_seq=1, apply_final_norm=last)
        outs["s_tg"].append(tail_g)
        outs["s_tu"].append(tail_u)

    y_prompt = xp.reshape(nbp, seq, d)
    y_sample = jnp.transpose(xs.reshape(ts, nbs, d), (1, 0, 2))

    st = {k: jnp.stack(v) for k, v in outs.items()}
    tps = seq // tm_p

    def prompt_tail(t):
        return t.reshape(depth, nbp, tps, pad_p, dfp)[:, :, -1, pad_p - 2:, :d_ff]

    p_ffn = jnp.concatenate([prompt_tail(st["p_tg"]), prompt_tail(st["p_tu"])], axis=-1)
    p_kv = st["p_kv"].reshape(depth, nbp, N_MEM, 2, N_HEADS, HEAD_DIM)

    def sample_tail(t):
        return jnp.transpose(t.reshape(depth, 2, nbs, dfp)[..., :d_ff], (0, 2, 1, 3))

    s_ffn = jnp.concatenate([sample_tail(st["s_tg"]), sample_tail(st["s_tu"])], axis=-1)
    s_pool = jnp.concatenate([state_pool, jnp.transpose(st["s_bx"], (0, 2, 1, 3))], axis=2)[:, :, -POOL_STATE:]
    return (y_prompt, y_sample,
            st["p_pool"], st["p_conv"], st["p_c"], st["p_n"], st["p_m"], p_ffn, p_kv[:, :, :, 0], p_kv[:, :, :, 1],
            jnp.transpose(st["s_v"], (0, 2, 1, 3)), s_pool, jnp.transpose(st["s_conv"], (0, 2, 1, 3)),
            s_c, st["s_n"].reshape(depth, nbs, N_HEADS, HEAD_DIM), st["s_m"][:, :, :N_HEADS], s_ffn)
```
